```python
import math
import jax, jax.numpy as jnp
from jax import lax
import numpy as np

D_MODEL = 1024
BATCH = 32
SEQ = 256
DEPTH = 4
DEC_BATCH = 2
DEC_SEQ = 4096
PAST_LEN = 256

GRID_W = 64
ROPE_THETA = 10000.0
EPS = 1e-6
Q_BLOCK = 128
N_BRANCH = 4
BRANCH_W = D_MODEL // 4
GDN_HEADS = 4
GDN_DK = BRANCH_W // GDN_HEADS
GDN_DV = BRANCH_W // GDN_HEADS
GDN_CHUNK = 64
CONV_K = 5
DIFF_HEADS = 4
DIFF_DV = BRANCH_W // DIFF_HEADS
DIFF_DQK = DIFF_DV // 2
SGU_GROUPS = 4
SGU_DIM = BRANCH_W // SGU_GROUPS
SPATIAL_CHUNK = 128
GQA_HEADS = 4
GQA_KV_HEADS = 2
GQA_DH = BRANCH_W // GQA_HEADS
N_GROUPS = 4
EXPERTS_PER_GROUP = 4
N_EXPERTS = N_GROUPS * EXPERTS_PER_GROUP
TOP_K_INNER = 2
D_EXPERT = D_MODEL // 4
IN_WIDTHS = (2 * GDN_HEADS * GDN_DK + GDN_HEADS * GDN_DV, GDN_HEADS * GDN_DV, 2 * GDN_HEADS, 2 * GDN_HEADS,
             2 * DIFF_HEADS * DIFF_DQK, 2 * DIFF_HEADS * DIFF_DQK, DIFF_HEADS * DIFF_DV,
             SGU_GROUPS * SGU_DIM, SGU_GROUPS * SGU_DIM,
             GQA_HEADS * GQA_DH, GQA_KV_HEADS * GQA_DH, GQA_KV_HEADS * GQA_DH, N_BRANCH * D_MODEL)
IN_COLS = sum(IN_WIDTHS)

kernel_name = 'hybrid_diffusion_prefix_trunk'


def split_points():
    pts, acc = [], 0
    for w in IN_WIDTHS[:-1]:
        acc += w
        pts.append(acc)
    return pts


def rmsnorm(x, g):
    xf = x.astype(jnp.float32)
    y = xf * lax.rsqrt(jnp.mean(xf * xf, axis=-1, keepdims=True) + EPS)
    return (y * g.astype(jnp.float32)).astype(x.dtype)


def l2norm(x):
    return x * lax.rsqrt(jnp.sum(x * x, axis=-1, keepdims=True) + EPS)


def short_conv(x, w):
    pad = CONV_K // 2
    return lax.conv_general_dilated(x, w.astype(x.dtype)[:, None, :], window_strides=(1,),
                                    padding=[(pad, pad)], dimension_numbers=('NWC', 'WIO', 'NWC'),
                                    feature_group_count=x.shape[-1])


def axial_rope(x):
    T, d = x.shape[1], x.shape[-1]
    rows = T // GRID_W
    row = jnp.repeat(jnp.arange(rows, dtype=jnp.float32), GRID_W)
    col = jnp.tile(jnp.arange(GRID_W, dtype=jnp.float32), rows)
    half = d // 2
    nf = half // 2
    inv = ROPE_THETA ** (-jnp.arange(nf, dtype=jnp.float32) / nf)
    xf = x.astype(jnp.float32)

    def rot(xa, pos):
        ang = pos[:, None] * inv[None, :]
        cos = jnp.cos(ang)[None, :, None, :]
        sin = jnp.sin(ang)[None, :, None, :]
        x1, x2 = xa[..., :nf], xa[..., nf:]
        return jnp.concatenate([x1 * cos - x2 * sin, x1 * sin + x2 * cos], axis=-1)

    return jnp.concatenate([rot(xf[..., :half], row), rot(xf[..., half:], col)], axis=-1).astype(x.dtype)


def attend(q, k, v, scale):
    B, S, Hq, d = q.shape
    Hkv = k.shape[2]
    rep = Hq // Hkv
    dv = v.shape[-1]
    nb = S // Q_BLOCK
    qb = jnp.moveaxis(q.reshape(B, nb, Q_BLOCK, Hkv, rep, d), 1, 0).astype(jnp.float32)
    kf = k.astype(jnp.float32)

    def block(qi):
        s = jnp.einsum('bqgrd,bkgd->bgrqk', qi, kf) * scale
        p = jax.nn.softmax(s, axis=-1)
        return jnp.einsum('bgrqk,bkgv->bqgrv', p.astype(v.dtype), v)

    o = lax.map(block, qb)
    return jnp.moveaxis(o, 0, 1).reshape(B, S, Hq, dv)


def gdn_chunked(q, k, v, log_a, beta, s0):
    B, T, H, DK = q.shape
    DV = v.shape[-1]
    n = T // GDN_CHUNK

    def blocks(x):
        return jnp.moveaxis(x.reshape((B, n, GDN_CHUNK) + x.shape[2:]), 2, 3)

    qc, kc, vc, bc = blocks(q), blocks(k), blocks(v), blocks(beta)
    g = jnp.cumsum(blocks(log_a), axis=-1)
    idx = jnp.arange(GDN_CHUNK)
    incl = idx[:, None] >= idx[None, :]
    strict = idx[:, None] > idx[None, :]
    diff = g[..., :, None] - g[..., None, :]
    dec_incl = jnp.where(incl, jnp.exp(jnp.where(incl, diff, 0.0)), 0.0)
    dec_strict = jnp.where(strict, dec_incl, 0.0)
    kk = jnp.einsum('bnhid,bnhjd->bnhij', kc, kc)
    a_mat = jnp.eye(GDN_CHUNK, dtype=jnp.float32) + bc[..., :, None] * kk * dec_strict
    rhs = jnp.concatenate([bc[..., None] * vc, (bc * jnp.exp(g))[..., None] * kc], axis=-1)
    sol = lax.linalg.triangular_solve(a_mat, rhs, left_side=True, lower=True, unit_diagonal=True)
    u, wk = sol[..., :DV], sol[..., DV:]
    qk = jnp.einsum('bnhid,bnhjd->bnhij', qc, kc) * dec_incl
    qg = qc * jnp.exp(g)[..., None]
    g_last = g[..., -1]
    kd = kc * jnp.exp(g_last[..., None] - g)[..., None]

    def step(s, xs):
        u_i, wk_i, qk_i, qg_i, kd_i, gl_i = xs
        w = u_i - jnp.einsum('bhcd,bhdv->bhcv', wk_i, s)
        o = jnp.einsum('bhcd,bhdv->bhcv', qg_i, s) + jnp.einsum('bhij,bhjv->bhiv', qk_i, w)
        s = jnp.exp(gl_i)[..., None, None] * s + jnp.einsum('bhcd,bhcv->bhdv', kd_i, w)
        return s, o

    xs = tuple(jnp.moveaxis(t, 1, 0) for t in (u, wk, qk, qg, kd, g_last))
    s_fin, o = lax.scan(step, s0, xs)
    o = jnp.moveaxis(jnp.moveaxis(o, 0, 1), 3, 2).reshape(B, T, H, DV)
    return o, s_fin


def gdn_mixer(qkv, gate, a, b, conv_w, a_log, dt_bias, norm_g, s0):
    B, T, _ = qkv.shape
    qkv = jax.nn.silu(short_conv(qkv, conv_w)).astype(jnp.float32)
    q, k, v = jnp.split(qkv, [GDN_HEADS * GDN_DK, 2 * GDN_HEADS * GDN_DK], axis=-1)
    q = l2norm(q.reshape(B, T, GDN_HEADS, GDN_DK)) * (GDN_DK ** -0.5)
    k = l2norm(k.reshape(B, T, GDN_HEADS, GDN_DK))
    v = v.reshape(B, T, GDN_HEADS, GDN_DV)
    a = a.astype(jnp.float32).reshape(B, T, 2, GDN_HEADS)
    b = b.astype(jnp.float32).reshape(B, T, 2, GDN_HEADS)
    log_a = -jnp.exp(a_log.astype(jnp.float32)) * jax.nn.softplus(a + dt_bias.astype(jnp.float32))
    beta = jax.nn.sigmoid(b)
    s0 = s0.astype(jnp.float32)
    o_f, s_f = gdn_chunked(q, k, v, log_a[:, :, 0], beta[:, :, 0], s0[:, 0])
    fl = lambda t: jnp.flip(t, axis=1)
    o_b, s_b = gdn_chunked(fl(q), fl(k), fl(v), fl(log_a[:, :, 1]), fl(beta[:, :, 1]), s0[:, 1])
    o = (o_f + fl(o_b)).astype(gate.dtype)
    o = rmsnorm(o, norm_g) * jax.nn.silu(gate.reshape(B, T, GDN_HEADS, GDN_DV))
    return o.reshape(B, T, GDN_HEADS * GDN_DV), jnp.stack([s_f, s_b], axis=1).astype(gate.dtype)


def diff_attention(q, k, v, lam_p, norm_g, lam_init):
    B, T = q.shape[:2]
    tk = k.shape[1]
    q = q.reshape(B, T, DIFF_HEADS, 2, DIFF_DQK)
    k = k.reshape(B, tk, DIFF_HEADS, 2, DIFF_DQK)
    scale = DIFF_DQK ** -0.5
    a1 = attend(q[..., 0, :], k[..., 0, :], v, scale)
    a2 = attend(q[..., 1, :], k[..., 1, :], v, scale)
    lp32 = lam_p.astype(jnp.float32)
    lam = jnp.exp(jnp.sum(lp32[0] * lp32[1])) - jnp.exp(jnp.sum(lp32[2] * lp32[3])) + lam_init
    o = a1.astype(jnp.float32) - lam * a2.astype(jnp.float32)
    o = rmsnorm(o, norm_g) * (1.0 - lam_init)
    return o.astype(v.dtype).reshape(B, T, DIFF_HEADS * DIFF_DV)


def sgu_mixer(u, v, norm_g, ws, bs):
    B, T, _ = u.shape
    n = T // SPATIAL_CHUNK
    vn = rmsnorm(v, norm_g).reshape(B, n, SPATIAL_CHUNK, SGU_GROUPS, SGU_DIM)
    z = jnp.einsum('gpq,bnqgc->bnpgc', ws.astype(vn.dtype), vn) + bs.T.astype(vn.dtype)[None, None, :, :, None]
    return u * z.reshape(B, T, SGU_GROUPS * SGU_DIM)


def hier_moe(x, wg, bg, we, be, w1, w3, w2):
    B, T, _ = x.shape
    g_logits = (x @ wg + bg).astype(jnp.float32)
    g_prob = jax.nn.softmax(g_logits, axis=-1)
    g_sel = jnp.argmax(g_logits, axis=-1)
    g_onehot = jax.nn.one_hot(g_sel, N_GROUPS, dtype=jnp.float32)
    e_logits = (x @ we + be).astype(jnp.float32).reshape(B, T, N_GROUPS, EXPERTS_PER_GROUP)
    e_in = jnp.sum(e_logits * g_onehot[..., None], axis=2)
    top_v, top_i = lax.top_k(e_in, TOP_K_INNER)
    w_top = jax.nn.softmax(top_v, axis=-1) * jnp.max(g_prob, axis=-1, keepdims=True)
    eid = g_sel[..., None] * EXPERTS_PER_GROUP + top_i
    gates = jnp.sum(jax.nn.one_hot(eid, N_EXPERTS, dtype=jnp.float32) * w_top[..., None], axis=-2)
    hdn = jax.nn.silu(jnp.einsum('btd,edf->btef', x, w1)) * jnp.einsum('btd,edf->btef', x, w3)
    return jnp.einsum('btef,efd->btd', hdn * gates[..., None].astype(x.dtype), w2)


def trunk_layer(h, cond, lp, lam_init, ctx):
    B, T, _ = h.shape
    latent = ctx is not None
    mod = (jax.nn.silu(cond) @ lp['ada_w'] + lp['ada_b'])[:, None, :]
    sh1, sc1, g1, sh2, sc2, g2 = jnp.split(mod, 6, axis=-1)
    xm = rmsnorm(h, lp['norm1']) * (1.0 + sc1) + sh1
    proj = xm @ lp['w_in']
    (qkv_a, gate_a, a_a, b_a, q_b, k_b, v_b, u_c, v_c, q_d, k_d, v_d, mgate) = jnp.split(proj, split_points(), axis=-1)

    s0 = ctx[0] if latent else jnp.zeros((B, 2, GDN_HEADS, GDN_DK, GDN_DV), jnp.float32)
    o_a, s_gdn = gdn_mixer(qkv_a, gate_a, a_a, b_a, lp['gdn_conv'], lp['gdn_a_log'], lp['gdn_dt_bias'], lp['gdn_norm'], s0)

    qb = q_b.reshape(B, T, 2 * DIFF_HEADS, DIFF_DQK)
    kb = k_b.reshape(B, T, 2 * DIFF_HEADS, DIFF_DQK)
    vb = v_b.reshape(B, T, DIFF_HEADS, DIFF_DV)
    if latent:
        tc = ctx[1].shape[1]
        qb_use = axial_rope(qb)
        kb_all = jnp.concatenate([ctx[1].reshape(B, tc, 2 * DIFF_HEADS, DIFF_DQK), axial_rope(kb)], axis=1)
        vb_all = jnp.concatenate([ctx[2], vb], axis=1)
    else:
        qb_use, kb_all, vb_all = qb, kb, vb
    o_b = diff_attention(qb_use, kb_all, vb_all, lp['diff_lambda'], lp['diff_norm'], lam_init)

    o_c = sgu_mixer(u_c, v_c, lp['sgu_norm'], lp['sgu_ws'], lp['sgu_b'])

    qd = rmsnorm(q_d.reshape(B, T, GQA_HEADS, GQA_DH), lp['gqa_qnorm'])
    kd = rmsnorm(k_d.reshape(B, T, GQA_KV_HEADS, GQA_DH), lp['gqa_knorm'])
    vd = v_d.reshape(B, T, GQA_KV_HEADS, GQA_DH)
    if latent:
        qd_use = axial_rope(qd)
        kd_all = jnp.concatenate([ctx[3], axial_rope(kd)], axis=1)
        vd_all = jnp.concatenate([ctx[4], vd], axis=1)
    else:
        qd_use, kd_all, vd_all = qd, kd, vd
    o_d = attend(qd_use, kd_all, vd_all, GQA_DH ** -0.5).reshape(B, T, GQA_HEADS * GQA_DH)

    br = jnp.stack([o_a, o_b, o_c, o_d], axis=2)
    up = jnp.einsum('btnw,nwd->btnd', br, lp['w_branch'])
    gt = jax.nn.sigmoid(mgate.reshape(B, T, N_BRANCH, D_MODEL))
    merged = jnp.sum(gt * up, axis=2)
    h = h + g1 * (merged @ lp['w_out'])

    xm2 = rmsnorm(h, lp['norm2']) * (1.0 + sc2) + sh2
    h = h + g2 * hier_moe(xm2, lp['moe_wg'], lp['moe_bg'], lp['moe_we'], lp['moe_be'], lp['moe_w1'], lp['moe_w3'], lp['moe_w2'])

    if latent:
        return h, None
    new_ctx = (s_gdn, kb.reshape(B, T, DIFF_HEADS, 2 * DIFF_DQK), vb, kd, vd)
    return h, new_ctx


def setup_inputs(seed: int = 0) -> dict:
    key = jax.random.key(seed)
    ks = iter(list(jax.random.split(key, 48)))

    def nrm(shape, s):
        return jax.random.normal(next(ks), shape, jnp.float32) * s

    def gain(shape):
        return 1.0 + nrm(shape, 0.01)

    dt = jnp.exp(jax.random.uniform(next(ks), (DEPTH, 2, GDN_HEADS), jnp.float32, math.log(1e-3), math.log(1e-1)))
    return {
        'x_prompt': nrm((BATCH, SEQ, D_MODEL), 1.0),
        'x_sample': nrm((DEC_BATCH, DEC_SEQ, D_MODEL), 1.0),
        'state_gdn': nrm((DEC_BATCH, DEPTH, 2, GDN_HEADS, GDN_DK, GDN_DV), 0.1),
        'cache_diff_k': nrm((DEC_BATCH, DEPTH, PAST_LEN, DIFF_HEADS, 2 * DIFF_DQK), 1.0),
        'cache_diff_v': nrm((DEC_BATCH, DEPTH, PAST_LEN, DIFF_HEADS, DIFF_DV), 1.0),
        'cache_gqa_k': nrm((DEC_BATCH, DEPTH, PAST_LEN, GQA_KV_HEADS, GQA_DH), 1.0),
        'cache_gqa_v': nrm((DEC_BATCH, DEPTH, PAST_LEN, GQA_KV_HEADS, GQA_DH), 1.0),
        'c': nrm((DEC_BATCH, D_MODEL), 1.0),
        'c_ctx': nrm((D_MODEL,), 1.0),
        'ada_w': nrm((DEPTH, D_MODEL, 6 * D_MODEL), 0.5 * D_MODEL ** -0.5),
        'ada_b': nrm((DEPTH, 6 * D_MODEL), 0.01),
        'norm1': gain((DEPTH, D_MODEL)),
        'norm2': gain((DEPTH, D_MODEL)),
        'w_in': nrm((DEPTH, D_MODEL, IN_COLS), D_MODEL ** -0.5),
        'gdn_conv': nrm((DEPTH, CONV_K, 2 * GDN_HEADS * GDN_DK + GDN_HEADS * GDN_DV), CONV_K ** -0.5),
        'gdn_a_log': jnp.log(jax.random.uniform(next(ks), (DEPTH, 2, GDN_HEADS), jnp.float32, 1.0, 16.0)),
        'gdn_dt_bias': dt + jnp.log(-jnp.expm1(-dt)),
        'gdn_norm': gain((DEPTH, GDN_DV)),
        'diff_lambda': nrm((DEPTH, 4, DIFF_DQK), 0.1),
        'diff_norm': gain((DEPTH, DIFF_DV)),
        'sgu_norm': gain((DEPTH, SGU_GROUPS * SGU_DIM)),
        'sgu_ws': nrm((DEPTH, SGU_GROUPS, SPATIAL_CHUNK, SPATIAL_CHUNK), 0.5 * SPATIAL_CHUNK ** -0.5),
        'sgu_b': gain((DEPTH, SGU_GROUPS, SPATIAL_CHUNK)),
        'gqa_qnorm': gain((DEPTH, GQA_DH)),
        'gqa_knorm': gain((DEPTH, GQA_DH)),
        'w_branch': nrm((DEPTH, N_BRANCH, BRANCH_W, D_MODEL), BRANCH_W ** -0.5),
        'w_out': nrm((DEPTH, D_MODEL, D_MODEL), D_MODEL ** -0.5),
        'moe_wg': nrm((DEPTH, D_MODEL, N_GROUPS), D_MODEL ** -0.5),
        'moe_bg': nrm((DEPTH, N_GROUPS), 0.01),
        'moe_we': nrm((DEPTH, D_MODEL, N_EXPERTS), D_MODEL ** -0.5),
        'moe_be': nrm((DEPTH, N_EXPERTS), 0.01),
        'moe_w1': nrm((DEPTH, N_EXPERTS, D_MODEL, D_EXPERT), D_MODEL ** -0.5),
        'moe_w3': nrm((DEPTH, N_EXPERTS, D_MODEL, D_EXPERT), D_MODEL ** -0.5),
        'moe_w2': nrm((DEPTH, N_EXPERTS, D_EXPERT, D_MODEL), D_EXPERT ** -0.5),
        'final_norm': gain((D_MODEL,)),
    }


def reference(x_prompt, x_sample, state_gdn, cache_diff_k, cache_diff_v, cache_gqa_k, cache_gqa_v, c, c_ctx,
              ada_w, ada_b, norm1, norm2, w_in, gdn_conv, gdn_a_log, gdn_dt_bias, gdn_norm, diff_lambda, diff_norm,
              sgu_norm, sgu_ws, sgu_b, gqa_qnorm, gqa_knorm, w_branch, w_out, moe_wg, moe_bg, moe_we, moe_be,
              moe_w1, moe_w3, moe_w2, final_norm):
    hp, hs = x_prompt, x_sample
    new_s, new_dk, new_dv, new_gk, new_gv = [], [], [], [], []
    for l in range(DEPTH):
        lp = {'ada_w': ada_w[l], 'ada_b': ada_b[l], 'norm1': norm1[l], 'norm2': norm2[l], 'w_in': w_in[l],
              'gdn_conv': gdn_conv[l], 'gdn_a_log': gdn_a_log[l], 'gdn_dt_bias': gdn_dt_bias[l], 'gdn_norm': gdn_norm[l],
              'diff_lambda': diff_lambda[l], 'diff_norm': diff_norm[l],
              'sgu_norm': sgu_norm[l], 'sgu_ws': sgu_ws[l], 'sgu_b': sgu_b[l],
              'gqa_qnorm': gqa_qnorm[l], 'gqa_knorm': gqa_knorm[l], 'w_branch': w_branch[l], 'w_out': w_out[l],
              'moe_wg': moe_wg[l], 'moe_bg': moe_bg[l], 'moe_we': moe_we[l], 'moe_be': moe_be[l],
              'moe_w1': moe_w1[l], 'moe_w3': moe_w3[l], 'moe_w2': moe_w2[l]}
        lam_init = 0.8 - 0.6 * math.exp(-0.3 * l)
        hp, ctx_out = trunk_layer(hp, c_ctx[None, :], lp, lam_init, None)
        new_s.append(ctx_out[0])
        new_dk.append(ctx_out[1])
        new_dv.append(ctx_out[2])
        new_gk.append(ctx_out[3])
        new_gv.append(ctx_out[4])
        ctx_in = (state_gdn[:, l], cache_diff_k[:, l], cache_diff_v[:, l], cache_gqa_k[:, l], cache_gqa_v[:, l])
        hs, _ = trunk_layer(hs, c, lp, lam_init, ctx_in)
    y_prompt = rmsnorm(hp, final_norm)
    y_sample = rmsnorm(hs, final_norm)
    return (y_prompt, y_sample, jnp.stack(new_s, axis=1), jnp.stack(new_dk, axis=1), jnp.stack(new_dv, axis=1),
            jnp.stack(new_gk, axis=1), jnp.stack(new_gv, axis=1))
```

```python
import functools
import math

import jax
import jax.numpy as jnp
from jax import lax
from jax.experimental import pallas as pl
from jax.experimental.pallas import tpu as pltpu

F32 = jnp.float32
BF16 = jnp.bfloat16

GRID_W = 64
ROPE_THETA = 10000.0
EPS = 1e-6
D_MODEL = 1024
BRANCH_W = 256
HEAD_W = 64
DIFF_DQK = 32
N_HEADS = 4
GQA_KV_HEADS = 2
GDN_CHUNK = 64
CONV_K = 5
SGU_CHUNK = 128
SGU_GROUPS = 4
N_GROUPS = 4
EXPERTS_PER_GROUP = 4
N_EXPERTS = 16
D_EXPERT = 256
DIFF_SCALE = DIFF_DQK ** -0.5
GQA_SCALE = HEAD_W ** -0.5
GDN_Q_SCALE = HEAD_W ** -0.5

LANES = 128
SUBLANES = 8
VMEM_LIMIT = 56 * 1024 * 1024

_IN_WIDTHS = (768, 256, 8, 8, 256, 256, 256, 256, 256, 256, 128, 128, 4096)
_IN_OFFS = tuple(sum(_IN_WIDTHS[:i]) for i in range(len(_IN_WIDTHS)))
(_O_QKV, _O_GATE, _O_A, _O_B, _O_DQ, _O_DK, _O_DV, _O_SU, _O_SV, _O_GQ, _O_GK, _O_GV, _O_MG) = _IN_OFFS
MIX_COLS = _O_MG

C_QKV, C_GATE, C_AB, C_DQ, C_DK, C_DV, C_SU, C_SV, C_GQ, C_GK, C_GV, C_END = (
    0, 768, 1024, 1152, 1408, 1664, 1920, 2176, 2432, 2688, 2816, 2944)

TM = 512
PREP_R = 256
HALO = SUBLANES
SCAN_G = 4
ATTN_TQ = 256


def _bf(x):
    return x.astype(BF16)


def _dot(a, b):
    return jnp.dot(a, b, preferred_element_type=F32)


def _dot_nt(a, b):
    return lax.dot_general(a, b, (((1,), (1,)), ((), ())), preferred_element_type=F32)


def _dot_tn(a, b):
    return lax.dot_general(a, b, (((0,), (0,)), ((), ())), preferred_element_type=F32)


def _split_dot(x, m, terms):
    acc = None
    r = x
    for t in range(terms):
        p = _bf(r)
        d = _dot(p, m)
        acc = d if acc is None else acc + d
        if t + 1 < terms:
            r = r - p.astype(F32)
    return acc


def _split_dot_left(m, x, terms):
    acc = None
    r = x
    for t in range(terms):
        p = _bf(r)
        d = _dot(m, p)
        acc = d if acc is None else acc + d
        if t + 1 < terms:
            r = r - p.astype(F32)
    return acc


def _sigmoid(x):
    return 1.0 / (1.0 + jnp.exp(-x))


def _silu(x):
    return x * _sigmoid(x)


def _norm_mod(h, gamma, scale, shift):
    ms = jnp.mean(h * h, axis=-1, keepdims=True)
    return (h * lax.rsqrt(ms + EPS) * gamma) * (1.0 + scale) + shift


def _group_sum(sq, b64):
    return _split_dot(sq, b64, 2)


def _rope(x, cos, sin_signed, nf):
    lane = lax.broadcasted_iota(jnp.int32, (1, LANES), 1)
    first = (lane % (2 * nf)) < nf
    partner = jnp.where(first, pltpu.roll(x, LANES - nf, 1), pltpu.roll(x, nf, 1))
    return x * cos + partner * sin_signed


def _ada_kernel(c_ref, w_ref, b_ref, o_ref):
    x = _silu(c_ref[...])
    o_ref[...] = jnp.dot(x, w_ref[...], precision=lax.Precision.HIGHEST,
                         preferred_element_type=F32) + b_ref[...]


def _ada_call(cond, ada_w, ada_b):
    n_layers, d, n_out = ada_w.shape
    tn = 1536
    return pl.pallas_call(
        _ada_kernel,
        grid=(n_layers, n_out // tn),
        in_specs=[
            pl.BlockSpec((SUBLANES, d), lambda l, j: (0, 0)),
            pl.BlockSpec((None, d, tn), lambda l, j: (l, 0, j)),
            pl.BlockSpec((None, 1, tn), lambda l, j: (l, 0, j)),
        ],
        out_specs=pl.BlockSpec((None, SUBLANES, tn), lambda l, j: (l, 0, j)),
        out_shape=jax.ShapeDtypeStruct((n_layers, SUBLANES, n_out), F32),
        compiler_params=pltpu.CompilerParams(vmem_limit_bytes=VMEM_LIMIT),
        name="ada_mod",
    )(cond, ada_w, ada_b.reshape(n_layers, 1, n_out))


def _inproj_kernel(h_ref, mod_ref, n1_ref, w_ref, rope_ref, b64_ref, sgn_ref, sgw_ref, sgb_ref, gqn_ref, gkn_ref,
                   gqkv_ref, ggate_ref, gab_ref, dq_ref, dk32_ref, dv32_ref, dk16_ref, dv16_ref, oc_ref,
                   gq_ref, gk32_ref, gv32_ref, gk16_ref, gv16_ref):
    tm = h_ref.shape[0]
    xm = _bf(_norm_mod(h_ref[...], n1_ref[...], mod_ref[:, D_MODEL:2 * D_MODEL], mod_ref[:, 0:D_MODEL]))

    def proj(a, b):
        return _dot(xm, w_ref[:, a:b])

    gqkv_ref[...] = proj(C_QKV, C_GATE)
    ggate_ref[...] = proj(C_GATE, C_AB)
    gab_ref[...] = proj(C_AB, C_DQ)

    cos_d, sin_d, cos_g, sin_g = rope_ref[0], rope_ref[1], rope_ref[2], rope_ref[3]
    b64 = b64_ref[...]

    for j in range(BRANCH_W // LANES):
        sl = slice(j * LANES, (j + 1) * LANES)
        q = proj(C_DQ + j * LANES, C_DQ + (j + 1) * LANES)
        dq_ref[:, sl] = _bf(_rope(q, cos_d, sin_d, DIFF_DQK // 4) * DIFF_SCALE)
        k = _rope(proj(C_DK + j * LANES, C_DK + (j + 1) * LANES), cos_d, sin_d, DIFF_DQK // 4)
        dk32_ref[:, sl] = k
        dk16_ref[:, sl] = _bf(k)
    v = proj(C_DV, C_SU)
    dv32_ref[...] = v
    dv16_ref[...] = _bf(v)

    u = proj(C_SU, C_SV)
    sv = proj(C_SV, C_GQ)
    vn = _bf(sv * lax.rsqrt(jnp.mean(sv * sv, axis=-1, keepdims=True) + EPS) * sgn_ref[...])
    group = lax.broadcasted_iota(jnp.int32, (1, BRANCH_W), 1) // HEAD_W
    for r in range(tm // SGU_CHUNK):
        rows = slice(r * SGU_CHUNK, (r + 1) * SGU_CHUNK)
        vc = vn[rows]
        z = sgb_ref[...]
        for g in range(SGU_GROUPS):
            z = z + jnp.where(group == g, _dot(sgw_ref[g], vc), 0.0)
        oc_ref[rows, :] = _bf(u[rows] * z)

    for j in range(BRANCH_W // LANES):
        sl = slice(j * LANES, (j + 1) * LANES)
        q = proj(C_GQ + j * LANES, C_GQ + (j + 1) * LANES)
        qn = q * lax.rsqrt(_group_sum(q * q, b64) * (1.0 / HEAD_W) + EPS) * gqn_ref[...]
        gq_ref[:, sl] = _bf(_rope(qn, cos_g, sin_g, HEAD_W // 4) * GQA_SCALE)
    k = proj(C_GK, C_GV)
    kn = k * lax.rsqrt(_group_sum(k * k, b64) * (1.0 / HEAD_W) + EPS) * gkn_ref[...]
    kr = _rope(kn, cos_g, sin_g, HEAD_W // 4)
    gk32_ref[...] = kr
    gk16_ref[...] = _bf(kr)
    v = proj(C_GV, C_END)
    gv32_ref[...] = v
    gv16_ref[...] = _bf(v)


def _inproj_call(h, mod, n1, w_mix, rope_tab, b64, sgn, sgw, sgb, gqn, gkn, *, tiles_ctx, tiles_per_lat):
    t = h.shape[0]
    n_tiles = t // TM

    def mod_row(i):
        return jnp.where(i < tiles_ctx, 0, 1 + (i - tiles_ctx) // tiles_per_lat)

    def rope_blk(i):
        return jnp.where(i < tiles_ctx, 0, 1 + (i - tiles_ctx) % tiles_per_lat)

    def const(shape):
        return pl.BlockSpec(shape, lambda i: (0,) * len(shape))

    def rows(width):
        return pl.BlockSpec((TM, width), lambda i: (i, 0))

    out_widths_dtypes = [
        (768, F32), (256, F32), (128, F32),
        (256, BF16), (256, F32), (256, F32), (256, BF16), (256, BF16),
        (256, BF16),
        (256, BF16), (128, F32), (128, F32), (128, BF16), (128, BF16),
    ]
    return pl.pallas_call(
        _inproj_kernel,
        grid=(n_tiles,),
        in_specs=[
            rows(D_MODEL),
            pl.BlockSpec((None, 1, 6 * D_MODEL), lambda i: (mod_row(i), 0, 0)),
            const((1, D_MODEL)),
            const((D_MODEL, C_END)),
            pl.BlockSpec((4, TM, LANES), lambda i: (0, rope_blk(i), 0)),
            const((LANES, LANES)),
            const((1, BRANCH_W)),
            const((SGU_GROUPS, SGU_CHUNK, SGU_CHUNK)),
            const((SGU_CHUNK, BRANCH_W)),
            const((1, LANES)),
            const((1, LANES)),
        ],
        out_specs=[rows(w) for w, _ in out_widths_dtypes],
        out_shape=[jax.ShapeDtypeStruct((t, w), dt) for w, dt in out_widths_dtypes],
        compiler_params=pltpu.CompilerParams(dimension_semantics=("parallel",), vmem_limit_bytes=VMEM_LIMIT),
        name="inproj",
    )(h, mod, n1, w_mix, rope_tab, b64, sgn, sgw, sgb, gqn, gkn)


def _gdn_prep_kernel(x_ref, xp_ref, xn_ref, ab_ref, cw_ref, alog_ref, dtb_ref, b64_ref, tril_ref, triu_ref,
                     q_ref, k_ref, v_ref, gb_ref, *, ctx_blocks, ctx_bps, lat_bps):
    r = x_ref.shape[0]
    i = pl.program_id(0)
    is_ctx = i < ctx_blocks
    pos = jnp.where(is_ctx, i % ctx_bps, (i - ctx_blocks) % lat_bps)
    bps = jnp.where(is_ctx, ctx_bps, lat_bps)
    prev = jnp.where(pos > 0, xp_ref[...], 0.0)
    nxt = jnp.where(pos < bps - 1, xn_ref[...], 0.0)
    win = jnp.concatenate([prev, x_ref[...], nxt], axis=0)
    n = r + 2 * HALO
    y = None
    for j in range(CONV_K):
        s = (CONV_K // 2 - j) % n
        shifted = win if s == 0 else pltpu.roll(win, s, 0)
        term = cw_ref[j:j + 1, :] * shifted[HALO:HALO + r]
        y = term if y is None else y + term
    y = _silu(y)

    b64 = b64_ref[...]
    for part, ref, scale in ((0, q_ref, GDN_Q_SCALE), (1, k_ref, 1.0)):
        for j in range(BRANCH_W // LANES):
            x = y[:, part * BRANCH_W + j * LANES: part * BRANCH_W + (j + 1) * LANES]
            xn = x * (lax.rsqrt(_group_sum(x * x, b64) + EPS) * scale)
            for hh in range(LANES // HEAD_W):
                ref[j * (LANES // HEAD_W) + hh] = xn[:, hh * HEAD_W:(hh + 1) * HEAD_W]
    for h in range(N_HEADS):
        v_ref[h] = y[:, 2 * BRANCH_W + h * HEAD_W: 2 * BRANCH_W + (h + 1) * HEAD_W]

    ab = ab_ref[...]
    lane = lax.broadcasted_iota(jnp.int32, (1, LANES), 1)
    lm = lane % 4
    xa = ab + dtb_ref[...]
    softplus = jnp.maximum(xa, 0.0) + jnp.log1p(jnp.exp(-jnp.abs(xa)))
    la = jnp.where((lm < 2) & (lane < 4 * N_HEADS), -jnp.exp(alog_ref[...]) * softplus, 0.0)
    g_fwd = _split_dot_left(tril_ref[...], la, 3)
    g_bwd = _split_dot_left(triu_ref[...], la, 3)
    gb = jnp.where(lm == 0, g_fwd, jnp.where(lm == 1, g_bwd, _sigmoid(ab)))
    for h in range(N_HEADS):
        gb_ref[h] = gb if h == 0 else pltpu.roll(gb, LANES - 4 * h, 1)


def _gdn_prep_call(gqkv, gab, conv_w, alog_row, dtb_row, b64, tril, triu, *, ctx_rows, ctx_len, lat_len):
    t = gqkv.shape[0]
    r = PREP_R
    n_blocks = t // r
    rh = r // HALO
    last_halo = t // HALO - 1
    kern = functools.partial(_gdn_prep_kernel, ctx_blocks=ctx_rows // r, ctx_bps=ctx_len // r, lat_bps=lat_len // r)

    def const(shape):
        return pl.BlockSpec(shape, lambda i: (0,) * len(shape))

    head_major = pl.BlockSpec((N_HEADS, r, HEAD_W), lambda i: (0, i, 0))
    return pl.pallas_call(
        kern,
        grid=(n_blocks,),
        in_specs=[
            pl.BlockSpec((r, 3 * BRANCH_W), lambda i: (i, 0)),
            pl.BlockSpec((HALO, 3 * BRANCH_W), lambda i: (jnp.maximum(i * rh - 1, 0), 0)),
            pl.BlockSpec((HALO, 3 * BRANCH_W), lambda i: (jnp.minimum((i + 1) * rh, last_halo), 0)),
            pl.BlockSpec((r, LANES), lambda i: (i, 0)),
            const((SUBLANES, 3 * BRANCH_W)),
            const((1, LANES)),
            const((1, LANES)),
            const((LANES, LANES)),
            const((r, r)),
            const((r, r)),
        ],
        out_specs=[head_major, head_major, head_major,
                   pl.BlockSpec((N_HEADS, r, LANES), lambda i: (0, i, 0))],
        out_shape=[jax.ShapeDtypeStruct((N_HEADS, t, HEAD_W), F32)] * 3
        + [jax.ShapeDtypeStruct((N_HEADS, t, LANES), F32)],
        compiler_params=pltpu.CompilerParams(dimension_semantics=("parallel",), vmem_limit_bytes=VMEM_LIMIT),
        name="gdn_prep",
    )(gqkv, gqkv, gqkv, gab, conv_w, alog_row, dtb_row, b64, tril, triu)


def _gdn_scan_kernel(*refs, n_seq, seq_len, has_s0):
    if has_s0:
        q_ref, k_ref, v_ref, gb_ref, s0_ref, o_ref, s_ref, u_scr, wk_scr, qk_scr, qg_scr, kd_scr, o_scr = refs
    else:
        q_ref, k_ref, v_ref, gb_ref, o_ref, s_ref, u_scr, wk_scr, qk_scr, qg_scr, kd_scr, o_scr = refs
        s0_ref = None
    c = GDN_CHUNK
    n_rows = n_seq * seq_len
    nc = seq_len // c
    row_i = lax.broadcasted_iota(jnp.int32, (c, c), 0)
    col_i = lax.broadcasted_iota(jnp.int32, (c, c), 1)
    eye = jnp.where(row_i == col_i, 1.0, 0.0)
    incl = (col_i <= row_i, col_i >= row_i)
    strict = (col_i < row_i, col_i > row_i)
    blk16 = (row_i // 16) == (col_i // 16)
    blk32 = ((row_i // 32) == (col_i // 32)) & ~blk16
    blk64 = (row_i // 32) != (col_i // 32)

    def prep_chunk(r0):
        rows = pl.ds(r0, c)
        q = q_ref[rows, :]
        k = k_ref[rows, :]
        v = v_ref[rows, :]
        gbc = gb_ref[rows, :]
        gbt = gbc.T
        kb = _bf(k)
        kk = _dot_nt(kb, kb)
        qk = _dot_nt(_bf(q), kb)
        for d in range(2):
            gcol = gbc[:, d:d + 1]
            grow = gbt[d:d + 1, :]
            bcol = gbc[:, 2 + d:3 + d]
            dec = jnp.where(incl[d], jnp.exp(jnp.where(incl[d], gcol - grow, 0.0)), 0.0)
            nmat = jnp.where(strict[d], bcol * kk * dec, 0.0)
            n0 = jnp.where(blk16, nmat, 0.0)
            tmat = eye - n0
            p = n0
            for _ in range(3):
                pb = _bf(p)
                p = _dot(pb, pb)
                tmat = tmat + _dot(_bf(tmat), _bf(p))
            for level in (blk32, blk64):
                tb = _bf(tmat)
                tmat = tmat - _dot(_bf(_dot(tb, _bf(jnp.where(level, nmat, 0.0)))), tb)
            ainv = _bf(tmat)
            eg = jnp.exp(gcol)
            glast = gcol[c - 1:c, :] if d == 0 else gcol[0:1, :]
            u_scr[d, rows, :] = _dot(ainv, _bf(bcol * v))
            wk_scr[d, rows, :] = _bf(_dot(ainv, _bf((bcol * eg) * k)))
            qk_scr[d, rows, :] = _bf(qk * dec)
            qg_scr[d, rows, :] = _bf(q * eg)
            kd_scr[d, rows, :] = _bf(k * jnp.exp(glast - gcol))

    def prep_group(gi, carry):
        for j in range(SCAN_G):
            prep_chunk(pl.multiple_of((gi * SCAN_G + j) * c, c))
        return carry

    lax.fori_loop(0, n_rows // (c * SCAN_G), prep_group, 0)

    def scan_step(s, d, ci, state):
        r0 = s * seq_len + ci * c
        if not isinstance(r0, int):
            r0 = pl.multiple_of(r0, c)
        rows = pl.ds(r0, c)
        sb = _bf(state)
        w = u_scr[d, rows, :] - _dot(wk_scr[d, rows, :], sb)
        wb = _bf(w)
        o_scr[d, rows, :] = _dot(qg_scr[d, rows, :], sb) + _dot(qk_scr[d, rows, :], wb)
        last = r0 + (c - 1) if d == 0 else r0
        egl = jnp.exp(gb_ref[pl.ds(last, 1), d:d + 1])
        return egl * state + _dot_tn(kd_scr[d, rows, :], wb)

    def init_state(s, d):
        return s0_ref[s, d] if has_s0 else jnp.zeros((c, c), F32)

    chains = [(s, d) for s in range(n_seq) for d in range(2)]
    states = [init_state(s, d) for s, d in chains]
    if nc <= 8:
        for t in range(nc):
            states = [scan_step(s, d, t if d == 0 else nc - 1 - t, st) for (s, d), st in zip(chains, states)]
    else:
        def body(t, sts):
            return tuple(scan_step(s, d, t if d == 0 else nc - 1 - t, st) for (s, d), st in zip(chains, sts))
        states = lax.fori_loop(0, nc, body, tuple(states))
    for (s, d), st in zip(chains, states):
        s_ref[s, d] = st
    o_ref[...] = o_scr[0] + o_scr[1]


def _gdn_scan_call(q, k, v, gb, s0, *, row0, n_seq_total, seq_len, seqs_per_step):
    n_rows = seqs_per_step * seq_len
    blk0 = row0 // n_rows
    has_s0 = s0 is not None
    kern = functools.partial(_gdn_scan_kernel, n_seq=seqs_per_step, seq_len=seq_len, has_s0=has_s0)
    hm = pl.BlockSpec((None, n_rows, HEAD_W), lambda s, h: (h, blk0 + s, 0))
    st_spec = pl.BlockSpec((seqs_per_step, 2, None, HEAD_W, HEAD_W), lambda s, h: (s, 0, h, 0, 0))
    in_specs = [hm, hm, hm, pl.BlockSpec((None, n_rows, LANES), lambda s, h: (h, blk0 + s, 0))]
    args = [q, k, v, gb]
    if has_s0:
        in_specs.append(st_spec)
        args.append(s0)
    return pl.pallas_call(
        kern,
        grid=(n_seq_total // seqs_per_step, N_HEADS),
        in_specs=in_specs,
        out_specs=[pl.BlockSpec((None, n_rows, HEAD_W), lambda s, h: (h, s, 0)), st_spec],
        out_shape=[jax.ShapeDtypeStruct((N_HEADS, n_seq_total * seq_len, HEAD_W), F32),
                   jax.ShapeDtypeStruct((n_seq_total, 2, N_HEADS, HEAD_W, HEAD_W), F32)],
        scratch_shapes=[
            pltpu.VMEM((2, n_rows, HEAD_W), F32),
            pltpu.VMEM((2, n_rows, HEAD_W), BF16),
            pltpu.VMEM((2, n_rows, HEAD_W), BF16),
            pltpu.VMEM((2, n_rows, HEAD_W), BF16),
            pltpu.VMEM((2, n_rows, HEAD_W), BF16),
            pltpu.VMEM((2, n_rows, HEAD_W), F32),
        ],
        compiler_params=pltpu.CompilerParams(dimension_semantics=("parallel", "parallel"),
                                             vmem_limit_bytes=VMEM_LIMIT),
        name="gdn_scan",
    )(*args)


def _softmax_rows(scores):
    mx = None
    for s in scores:
        m = jnp.max(s, axis=-1, keepdims=True)
        mx = m if mx is None else jnp.maximum(mx, m)
    ps = [jnp.exp(s - mx) for s in scores]
    tot = None
    for p in ps:
        l = jnp.sum(p, axis=-1, keepdims=True)
        tot = l if tot is None else tot + l
    inv = 1.0 / tot
    return [p * inv for p in ps]


def _attn_kernel(*refs, has_cache, lam_init):
    if has_cache:
        (dq_ref, dk_ref, dv_ref, gq_ref, gk_ref, gv_ref, cdk_ref, cdv_ref, cgk_ref, cgv_ref,
         lp_ref, dn_ref, b64_ref, ob_ref, od_ref) = refs
        dks, dvs = [cdk_ref[...], dk_ref[...]], [cdv_ref[...], dv_ref[...]]
        gks, gvs = [cgk_ref[...], gk_ref[...]], [cgv_ref[...], gv_ref[...]]
    else:
        dq_ref, dk_ref, dv_ref, gq_ref, gk_ref, gv_ref, lp_ref, dn_ref, b64_ref, ob_ref, od_ref = refs
        dks, dvs = [dk_ref[...]], [dv_ref[...]]
        gks, gvs = [gk_ref[...]], [gv_ref[...]]

    lp = lp_ref[...]
    lam = (jnp.exp(jnp.sum(lp[0:1] * lp[1:2], axis=-1, keepdims=True))
           - jnp.exp(jnp.sum(lp[2:3] * lp[3:4], axis=-1, keepdims=True)) + lam_init)

    lane = lax.broadcasted_iota(jnp.int32, (1, BRANCH_W), 1)
    dq = dq_ref[...]
    o = None
    for h in range(N_HEADS):
        pn = []
        for m in range(2):
            lo = h * HEAD_W + m * DIFF_DQK
            qm = dq * jnp.where((lane >= lo) & (lane < lo + DIFF_DQK), 1.0, 0.0).astype(BF16)
            pn.append(_softmax_rows([_dot_nt(qm, ks) for ks in dks]))
        oh = None
        for p1, p2, vs in zip(pn[0], pn[1], dvs):
            t = _dot(_bf(p1 - lam * p2), vs)
            oh = t if oh is None else oh + t
        oh = jnp.where(lane // HEAD_W == h, oh, 0.0)
        o = oh if o is None else o + oh
    b64 = b64_ref[...]
    for j in range(BRANCH_W // LANES):
        sl = slice(j * LANES, (j + 1) * LANES)
        x = o[:, sl]
        ms = _group_sum(x * x, b64) * (1.0 / HEAD_W)
        ob_ref[:, sl] = _bf(x * lax.rsqrt(ms + EPS) * dn_ref[:, sl] * (1.0 - lam_init))

    lane_h = lax.broadcasted_iota(jnp.int32, (1, LANES), 1)
    for half in range(2):
        qh = gq_ref[:, half * LANES:(half + 1) * LANES]
        acc = None
        for g in range(GQA_KV_HEADS):
            sel = lane_h // HEAD_W == g
            qm = qh * jnp.where(sel, 1.0, 0.0).astype(BF16)
            pn = _softmax_rows([_dot_nt(qm, ks) for ks in gks])
            og = None
            for p, vs in zip(pn, gvs):
                t = _dot(_bf(p), vs)
                og = t if og is None else og + t
            og = jnp.where(sel, og, 0.0)
            acc = og if acc is None else acc + og
        od_ref[:, half * LANES:(half + 1) * LANES] = _bf(acc)


def _attn_ctx_call(dq, dk, dv, gq, gk, gv, lp, dn, b64, *, n_seq, seq_len, lam_init):
    kern = functools.partial(_attn_kernel, has_cache=False, lam_init=lam_init)

    def rows(width):
        return pl.BlockSpec((seq_len, width), lambda s: (s, 0))

    def const(shape):
        return pl.BlockSpec(shape, lambda s: (0,) * len(shape))

    return pl.pallas_call(
        kern,
        grid=(n_seq,),
        in_specs=[rows(256), rows(256), rows(256), rows(256), rows(128), rows(128),
                  const((4, DIFF_DQK)), const((1, BRANCH_W)), const((LANES, LANES))],
        out_specs=[rows(256), rows(256)],
        out_shape=[jax.ShapeDtypeStruct((n_seq * seq_len, BRANCH_W), BF16)] * 2,
        compiler_params=pltpu.CompilerParams(dimension_semantics=("parallel",), vmem_limit_bytes=VMEM_LIMIT),
        name="attn_ctx",
    )(dq, dk, dv, gq, gk, gv, lp, dn, b64)


def _attn_lat_call(dq, dk, dv, gq, gk, gv, cdk, cdv, cgk, cgv, lp, dn, b64, *, row0, n_seq, seq_len, lam_init):
    kern = functools.partial(_attn_kernel, has_cache=True, lam_init=lam_init)
    tq = ATTN_TQ
    qpb = seq_len // tq
    q0 = row0 // tq
    s0 = row0 // seq_len
    past = cdk.shape[1]

    def qrows(width):
        return pl.BlockSpec((tq, width), lambda b, i: (q0 + b * qpb + i, 0))

    def kvrows(width):
        return pl.BlockSpec((seq_len, width), lambda b, i: (s0 + b, 0))

    def cache(width):
        return pl.BlockSpec((None, past, width), lambda b, i: (b, 0, 0))

    def const(shape):
        return pl.BlockSpec(shape, lambda b, i: (0,) * len(shape))

    out = pl.BlockSpec((tq, BRANCH_W), lambda b, i: (b * qpb + i, 0))
    return pl.pallas_call(
        kern,
        grid=(n_seq, qpb),
        in_specs=[qrows(256), kvrows(256), kvrows(256), qrows(256), kvrows(128), kvrows(128),
                  cache(256), cache(256), cache(128), cache(128),
                  const((4, DIFF_DQK)), const((1, BRANCH_W)), const((LANES, LANES))],
        out_specs=[out, out],
        out_shape=[jax.ShapeDtypeStruct((n_seq * seq_len, BRANCH_W), BF16)] * 2,
        compiler_params=pltpu.CompilerParams(dimension_semantics=("parallel", "arbitrary"),
                                             vmem_limit_bytes=VMEM_LIMIT),
        name="attn_lat",
    )(dq, dk, dv, gq, gk, gv, cdk, cdv, cgk, cgv, lp, dn, b64)


def _merge_kernel(h_ref, mod_ref, n1_ref, oa_ref, gate_ref, gn_ref, ob_ref, oc_ref, od_ref,
                  wmg_ref, wb_ref, wo_ref, b64_ref, out_ref):
    h = h_ref[...]
    xm = _bf(_norm_mod(h, n1_ref[...], mod_ref[:, D_MODEL:2 * D_MODEL], mod_ref[:, 0:D_MODEL]))
    g1 = mod_ref[:, 2 * D_MODEL:3 * D_MODEL]
    b64 = b64_ref[...]
    oa_parts = []
    for j in range(BRANCH_W // LANES):
        sl = slice(j * LANES, (j + 1) * LANES)
        x = jnp.concatenate([oa_ref[2 * j], oa_ref[2 * j + 1]], axis=-1)
        ms = _group_sum(x * x, b64) * (1.0 / HEAD_W)
        oa_parts.append(_bf(x * lax.rsqrt(ms + EPS) * gn_ref[:, sl] * _silu(gate_ref[:, sl])))
    merged = None
    for n in range(4):
        mg = _dot(xm, wmg_ref[:, n * D_MODEL:(n + 1) * D_MODEL])
        if n == 0:
            up = _dot(oa_parts[0], wb_ref[0, 0:LANES, :]) + _dot(oa_parts[1], wb_ref[0, LANES:2 * LANES, :])
        else:
            up = _dot((ob_ref, oc_ref, od_ref)[n - 1][...], wb_ref[n])
        term = _sigmoid(mg) * up
        merged = term if merged is None else merged + term
    out_ref[...] = h + g1 * _dot(_bf(merged), wo_ref[...])


def _merge_call(h, mod, n1, oa, gate, gn, ob, oc, od, wmg, wb, wo, b64, *, tiles_ctx, tiles_per_lat):
    t = h.shape[0]

    def mod_row(i):
        return jnp.where(i < tiles_ctx, 0, 1 + (i - tiles_ctx) // tiles_per_lat)

    def const(shape):
        return pl.BlockSpec(shape, lambda i: (0,) * len(shape))

    def rows(width):
        return pl.BlockSpec((TM, width), lambda i: (i, 0))

    return pl.pallas_call(
        _merge_kernel,
        grid=(t // TM,),
        in_specs=[
            rows(D_MODEL),
            pl.BlockSpec((None, 1, 6 * D_MODEL), lambda i: (mod_row(i), 0, 0)),
            const((1, D_MODEL)),
            pl.BlockSpec((N_HEADS, TM, HEAD_W), lambda i: (0, i, 0)),
            rows(BRANCH_W),
            const((1, BRANCH_W)),
            rows(BRANCH_W), rows(BRANCH_W), rows(BRANCH_W),
            const((D_MODEL, 4 * D_MODEL)),
            const((4, BRANCH_W, D_MODEL)),
            const((D_MODEL, D_MODEL)),
            const((LANES, LANES)),
        ],
        out_specs=rows(D_MODEL),
        out_shape=jax.ShapeDtypeStruct((t, D_MODEL), F32),
        compiler_params=pltpu.CompilerParams(dimension_semantics=("parallel",), vmem_limit_bytes=VMEM_LIMIT),
        name="merge",
    )(h, mod, n1, oa, gate, gn, ob, oc, od, wmg, wb, wo, b64)


ROUTER_ROWS = 32
ROUTER_E0 = 8


def _moe_kernel(h_ref, mod_ref, n2_ref, wr_ref, br_ref, w1_ref, w3_ref, w2_ref, out_ref, xm_scr, gates_scr, acc_scr):
    e = pl.program_id(1)
    tm = h_ref.shape[0]

    @pl.when(e == 0)
    def _():
        xm = _norm_mod(h_ref[...], n2_ref[...], mod_ref[:, 4 * D_MODEL:5 * D_MODEL],
                       mod_ref[:, 3 * D_MODEL:4 * D_MODEL])
        xm_scr[...] = _bf(xm)
        lt = lax.dot_general(wr_ref[...], xm, (((1,), (1,)), ((), ())), precision=lax.Precision.HIGHEST,
                             preferred_element_type=F32) + br_ref[...]
        gl = lt[0:N_GROUPS]
        gidx = lax.broadcasted_iota(jnp.int32, (N_GROUPS, 1), 0)
        gmax = jnp.max(gl, axis=0, keepdims=True)
        gsel = jnp.min(jnp.where(gl == gmax, gidx, N_GROUPS), axis=0, keepdims=True)
        gprob = 1.0 / jnp.sum(jnp.exp(gl - gmax), axis=0, keepdims=True)
        el = lt[ROUTER_E0:ROUTER_E0 + N_EXPERTS]
        eidx = lax.broadcasted_iota(jnp.int32, (N_EXPERTS, 1), 0)
        neg = -jnp.inf
        cand = jnp.where(eidx // EXPERTS_PER_GROUP == gsel, el, neg)
        v1 = jnp.max(cand, axis=0, keepdims=True)
        i1 = jnp.min(jnp.where(cand == v1, eidx, N_EXPERTS), axis=0, keepdims=True)
        cand2 = jnp.where(eidx == i1, neg, cand)
        v2 = jnp.max(cand2, axis=0, keepdims=True)
        i2 = jnp.min(jnp.where(cand2 == v2, eidx, N_EXPERTS), axis=0, keepdims=True)
        ex = jnp.exp(v2 - v1)
        wa = 1.0 / (1.0 + ex)
        gates_t = (jnp.where(eidx == i1, wa, 0.0) + jnp.where(eidx == i2, ex * wa, 0.0)) * gprob
        full = jnp.concatenate([gates_t, jnp.zeros((LANES - N_EXPERTS, tm), F32)], axis=0)
        gates_scr[...] = full.T
        acc_scr[...] = jnp.zeros_like(acc_scr)

    xm = xm_scr[...]
    h1 = _dot(xm, w1_ref[...])
    h3 = _dot(xm, w3_ref[...])
    lane = lax.broadcasted_iota(jnp.int32, (1, LANES), 1)
    gate = jnp.sum(jnp.where(lane == e, gates_scr[...], 0.0), axis=-1, keepdims=True)
    hd = _silu(h1) * h3 * gate
    acc_scr[...] += _dot(_bf(hd), w2_ref[...])

    @pl.when(e == N_EXPERTS - 1)
    def _():
        out_ref[...] = h_ref[...] + mod_ref[:, 5 * D_MODEL:6 * D_MODEL] * acc_scr[...]


def _moe_call(h, mod, n2, wr, br, w1, w3, w2, *, tiles_ctx, tiles_per_lat):
    t = h.shape[0]

    def mod_row(i):
        return jnp.where(i < tiles_ctx, 0, 1 + (i - tiles_ctx) // tiles_per_lat)

    return pl.pallas_call(
        _moe_kernel,
        grid=(t // TM, N_EXPERTS),
        in_specs=[
            pl.BlockSpec((TM, D_MODEL), lambda i, e: (i, 0)),
            pl.BlockSpec((None, 1, 6 * D_MODEL), lambda i, e: (mod_row(i), 0, 0)),
            pl.BlockSpec((1, D_MODEL), lambda i, e: (0, 0)),
            pl.BlockSpec((ROUTER_ROWS, D_MODEL), lambda i, e: (0, 0)),
            pl.BlockSpec((ROUTER_ROWS, 1), lambda i, e: (0, 0)),
            pl.BlockSpec((None, D_MODEL, D_EXPERT), lambda i, e: (e, 0, 0)),
            pl.BlockSpec((None, D_MODEL, D_EXPERT), lambda i, e: (e, 0, 0)),
            pl.BlockSpec((None, D_EXPERT, D_MODEL), lambda i, e: (e, 0, 0)),
        ],
        out_specs=pl.BlockSpec((TM, D_MODEL), lambda i, e: (i, 0)),
        out_shape=jax.ShapeDtypeStruct((t, D_MODEL), F32),
        scratch_shapes=[pltpu.VMEM((TM, D_MODEL), BF16), pltpu.VMEM((TM, LANES), F32),
                        pltpu.VMEM((TM, D_MODEL), F32)],
        compiler_params=pltpu.CompilerParams(dimension_semantics=("parallel", "arbitrary"),
                                             vmem_limit_bytes=VMEM_LIMIT),
        name="moe",
    )(h, mod, n2, wr, br, w1, w3, w2)


def _final_norm_kernel(h_ref, g_ref, o_ref):
    h = h_ref[...]
    o_ref[...] = h * lax.rsqrt(jnp.mean(h * h, axis=-1, keepdims=True) + EPS) * g_ref[...]


def _final_norm_call(h, g):
    t = h.shape[0]
    return pl.pallas_call(
        _final_norm_kernel,
        grid=(t // TM,),
        in_specs=[pl.BlockSpec((TM, D_MODEL), lambda i: (i, 0)), pl.BlockSpec((1, D_MODEL), lambda i: (0, 0))],
        out_specs=pl.BlockSpec((TM, D_MODEL), lambda i: (i, 0)),
        out_shape=jax.ShapeDtypeStruct((t, D_MODEL), F32),
        compiler_params=pltpu.CompilerParams(dimension_semantics=("parallel",)),
        name="final_norm",
    )(h, g)


def _mix_column_perm():
    cols = list(range(_O_QKV, _O_QKV + 768)) + list(range(_O_GATE, _O_GATE + 256))
    ab = []
    for h in range(N_HEADS):
        ab += [_O_A + h, _O_A + N_HEADS + h, _O_B + h, _O_B + N_HEADS + h]
    cols += ab + [-1] * (LANES - len(ab))
    cols += list(range(_O_DQ, _O_DQ + 768))
    cols += list(range(_O_SU, _O_SU + 512))
    for h in (0, 2, 1, 3):
        cols += list(range(_O_GQ + h * HEAD_W, _O_GQ + (h + 1) * HEAD_W))
    cols += list(range(_O_GK, _O_GK + 256))
    assert len(cols) == C_END
    return cols


def _rope_tables(lat_len):
    pos = jnp.arange(lat_len, dtype=jnp.int32)
    row = (pos // GRID_W).astype(F32)[:, None]
    col = (pos % GRID_W).astype(F32)[:, None]
    lane = jnp.arange(LANES, dtype=jnp.int32)[None, :]
    tabs = []
    for head_w in (DIFF_DQK, HEAD_W):
        half = head_w // 2
        nf = half // 2
        p = lane % head_w
        f = (p % half) % nf
        inv = ROPE_THETA ** (-f.astype(F32) / nf)
        ang = jnp.where(p < half, row, col) * inv
        sign = jnp.where((p % half) < nf, -1.0, 1.0)
        tabs += [jnp.cos(ang), jnp.sin(ang) * sign]
    tab = jnp.stack(tabs)
    ident = jnp.stack([jnp.ones((TM, LANES), F32), jnp.zeros((TM, LANES), F32)] * 2)
    return jnp.concatenate([ident, tab], axis=1)


def _chunk_tri(n, chunk, upper):
    i = jnp.arange(n)[:, None]
    j = jnp.arange(n)[None, :]
    same = (i // chunk) == (j // chunk)
    tri = (j >= i) if upper else (j <= i)
    return jnp.where(same & tri, 1.0, 0.0).astype(BF16)


def kernel(x_prompt, x_sample, state_gdn, cache_diff_k, cache_diff_v, cache_gqa_k, cache_gqa_v, c, c_ctx,
           ada_w, ada_b, norm1, norm2, w_in, gdn_conv, gdn_a_log, gdn_dt_bias, gdn_norm, diff_lambda, diff_norm,
           sgu_norm, sgu_ws, sgu_b, gqa_qnorm, gqa_knorm, w_branch, w_out, moe_wg, moe_bg, moe_we, moe_be,
           moe_w1, moe_w3, moe_w2, final_norm):
    nb, seq, d = x_prompt.shape
    nl, lat_len, _ = x_sample.shape
    depth = ada_w.shape[0]
    past = cache_diff_k.shape[2]
    ctx_rows = nb * seq
    lat_rows = nl * lat_len
    assert d == D_MODEL and seq % PREP_R == 0 and lat_len % TM == 0 and ctx_rows % lat_len == 0
    assert ctx_rows % TM == 0 and 1 + nl <= SUBLANES and lat_len % GRID_W == 0
    tiles_ctx = ctx_rows // TM
    tiles_per_lat = lat_len // TM
    tile_kw = dict(tiles_ctx=tiles_ctx, tiles_per_lat=tiles_per_lat)

    lane = jnp.arange(LANES)
    b64 = jnp.where((lane[:, None] // HEAD_W) == (lane[None, :] // HEAD_W), 1.0, 0.0).astype(BF16)
    tril = _chunk_tri(PREP_R, GDN_CHUNK, upper=False)
    triu = _chunk_tri(PREP_R, GDN_CHUNK, upper=True)
    rope_tab = _rope_tables(lat_len)

    perm = jnp.asarray(_mix_column_perm(), dtype=jnp.int32)
    w_mix = _bf(jnp.where(perm[None, None, :] >= 0, jnp.take(w_in, jnp.maximum(perm, 0), axis=2), 0.0))
    w_mg = _bf(w_in[:, :, _O_MG:])
    head_perm = jnp.asarray([r for h in (0, 2, 1, 3) for r in range(h * HEAD_W, (h + 1) * HEAD_W)], dtype=jnp.int32)
    wb = _bf(jnp.concatenate([w_branch[:, :3], jnp.take(w_branch[:, 3], head_perm, axis=1)[:, None]], axis=1))
    wo = _bf(w_out)
    w1, w3, w2 = _bf(moe_w1), _bf(moe_w3), _bf(moe_w2)
    wr = jnp.zeros((depth, ROUTER_ROWS, d), F32)
    wr = wr.at[:, 0:N_GROUPS].set(jnp.swapaxes(moe_wg, 1, 2))
    wr = wr.at[:, ROUTER_E0:ROUTER_E0 + N_EXPERTS].set(jnp.swapaxes(moe_we, 1, 2))
    br = jnp.zeros((depth, ROUTER_ROWS, 1), F32)
    br = br.at[:, 0:N_GROUPS, 0].set(moe_bg).at[:, ROUTER_E0:ROUTER_E0 + N_EXPERTS, 0].set(moe_be)
    conv_w = jnp.zeros((depth, SUBLANES, 3 * BRANCH_W), F32).at[:, :CONV_K].set(gdn_conv)
    ab_lane = jnp.zeros((depth, 1, LANES), F32)
    alog_row, dtb_row = ab_lane, ab_lane
    for h in range(N_HEADS):
        for dd in range(2):
            alog_row = alog_row.at[:, 0, 4 * h + dd].set(gdn_a_log[:, dd, h])
            dtb_row = dtb_row.at[:, 0, 4 * h + dd].set(gdn_dt_bias[:, dd, h])
    sgw = _bf(sgu_ws)
    sgb = jnp.repeat(jnp.swapaxes(sgu_b, 1, 2), HEAD_W, axis=2)
    gdn_gn = jnp.tile(gdn_norm, (1, N_HEADS))[:, None, :]
    diff_gn = jnp.tile(diff_norm, (1, N_HEADS))[:, None, :]
    gqn = jnp.tile(gqa_qnorm, (1, LANES // HEAD_W))[:, None, :]
    gkn = jnp.tile(gqa_knorm, (1, LANES // HEAD_W))[:, None, :]

    cond = jnp.zeros((SUBLANES, d), F32).at[0].set(c_ctx).at[1:1 + nl].set(c)
    mods = _ada_call(cond, ada_w, ada_b)

    h = jnp.concatenate([x_prompt.reshape(ctx_rows, d), x_sample.reshape(lat_rows, d)], axis=0)
    new_s, new_dk, new_dv, new_gk, new_gv = [], [], [], [], []
    for l in range(depth):
        lam_init = 0.8 - 0.6 * math.exp(-0.3 * l)
        mod = mods[l][:, None, :]
        n1 = norm1[l][None, :]
        (gqkv, ggate, gab, dq, dk32, dv32, dk16, dv16, oc, gq, gk32, gv32, gk16, gv16) = _inproj_call(
            h, mod, n1, w_mix[l], rope_tab, b64, sgu_norm[l][None, :], sgw[l], sgb[l], gqn[l], gkn[l], **tile_kw)

        qh, kh, vh, gb = _gdn_prep_call(gqkv, gab, conv_w[l], alog_row[l], dtb_row[l], b64, tril, triu,
                                        ctx_rows=ctx_rows, ctx_len=seq, lat_len=lat_len)
        oa_ctx, s_ctx = _gdn_scan_call(qh, kh, vh, gb, None, row0=0, n_seq_total=nb, seq_len=seq,
                                       seqs_per_step=8 if nb % 8 == 0 else 1)
        oa_lat, _ = _gdn_scan_call(qh, kh, vh, gb, state_gdn[:, l], row0=ctx_rows, n_seq_total=nl,
                                   seq_len=lat_len, seqs_per_step=1)
        oa = jnp.concatenate([oa_ctx, oa_lat], axis=1)

        lp = diff_lambda[l]
        ob_ctx, od_ctx = _attn_ctx_call(dq, dk16, dv16, gq, gk16, gv16, lp, diff_gn[l], b64,
                                        n_seq=nb, seq_len=seq, lam_init=lam_init)
        ob_lat, od_lat = _attn_lat_call(
            dq, dk16, dv16, gq, gk16, gv16,
            _bf(cache_diff_k[:, l].reshape(nl, past, BRANCH_W)), _bf(cache_diff_v[:, l].reshape(nl, past, BRANCH_W)),
            _bf(cache_gqa_k[:, l].reshape(nl, past, LANES)), _bf(cache_gqa_v[:, l].reshape(nl, past, LANES)),
            lp, diff_gn[l], b64, row0=ctx_rows, n_seq=nl, seq_len=lat_len, lam_init=lam_init)
        ob = jnp.concatenate([ob_ctx, ob_lat], axis=0)
        od = jnp.concatenate([od_ctx, od_lat], axis=0)

        h = _merge_call(h, mod, n1, oa, ggate, gdn_gn[l], ob, oc, od, w_mg[l], wb[l], wo[l], b64, **tile_kw)
        h = _moe_call(h, mod, norm2[l][None, :], wr[l], br[l], w1[l], w3[l], w2[l], **tile_kw)

        new_s.append(s_ctx)
        new_dk.append(dk32[:ctx_rows].reshape(nb, seq, N_HEADS, HEAD_W))
        new_dv.append(dv32[:ctx_rows].reshape(nb, seq, N_HEADS, HEAD_W))
        new_gk.append(gk32[:ctx_rows].reshape(nb, seq, GQA_KV_HEADS, HEAD_W))
        new_gv.append(gv32[:ctx_rows].reshape(nb, seq, GQA_KV_HEADS, HEAD_W))

    y = _final_norm_call(h, final_norm[None, :])
    return (y[:ctx_rows].reshape(nb, seq, d), y[ctx_rows:].reshape(nl, lat_len, d),
            jnp.stack(new_s, axis=1), jnp.stack(new_dk, axis=1), jnp.stack(new_dv, axis=1),
            jnp.stack(new_gk, axis=1), jnp.stack(new_gv, axis=1))
```

```python
import functools
import math

import jax
import jax.numpy as jnp
from jax import lax
from jax.experimental import pallas as pl
from jax.experimental.pallas import tpu as pltpu

F32 = jnp.float32
BF16 = jnp.bfloat16

GRID_W = 64
ROPE_THETA = 10000.0
EPS = 1e-6
D_MODEL = 1024
BRANCH_W = 256
HEAD_W = 64
DIFF_DQK = 32
N_HEADS = 4
GQA_KV_HEADS = 2
GDN_CHUNK = 64
CONV_K = 5
SGU_CHUNK = 128
SGU_GROUPS = 4
N_GROUPS = 4
EXPERTS_PER_GROUP = 4
N_EXPERTS = 16
D_EXPERT = 256
DIFF_SCALE = DIFF_DQK ** -0.5
GQA_SCALE = HEAD_W ** -0.5
GDN_Q_SCALE = HEAD_W ** -0.5

LANES = 128
SUBLANES = 8
VMEM_LIMIT = 56 * 1024 * 1024

_IN_WIDTHS = (768, 256, 8, 8, 256, 256, 256, 256, 256, 256, 128, 128, 4096)
_IN_OFFS = tuple(sum(_IN_WIDTHS[:i]) for i in range(len(_IN_WIDTHS)))
(_O_QKV, _O_GATE, _O_A, _O_B, _O_DQ, _O_DK, _O_DV, _O_SU, _O_SV, _O_GQ, _O_GK, _O_GV, _O_MG) = _IN_OFFS
MIX_COLS = _O_MG

C_QKV, C_GATE, C_AB, C_DQ, C_DK, C_DV, C_SU, C_SV, C_GQ, C_GK, C_GV, C_END = (
    0, 768, 1024, 1152, 1408, 1664, 1920, 2176, 2432, 2688, 2816, 2944)

TM = 512
PREP_R = 256
HALO = SUBLANES
SCAN_G = 4
ATTN_TQ = 256


def _bf(x):
    return x.astype(BF16)


def _dot(a, b):
    return jnp.dot(a, b, preferred_element_type=F32)


def _dot_nt(a, b):
    return lax.dot_general(a, b, (((1,), (1,)), ((), ())), preferred_element_type=F32)


def _dot_tn(a, b):
    return lax.dot_general(a, b, (((0,), (0,)), ((), ())), preferred_element_type=F32)


def _split_dot(x, m, terms):
    acc = None
    r = x
    for t in range(terms):
        p = _bf(r)
        d = _dot(p, m)
        acc = d if acc is None else acc + d
        if t + 1 < terms:
            r = r - p.astype(F32)
    return acc


def _split_dot_left(m, x, terms):
    acc = None
    r = x
    for t in range(terms):
        p = _bf(r)
        d = _dot(m, p)
        acc = d if acc is None else acc + d
        if t + 1 < terms:
            r = r - p.astype(F32)
    return acc


def _sigmoid(x):
    return 1.0 / (1.0 + jnp.exp(-x))


def _silu(x):
    return x * _sigmoid(x)


def _norm_mod(h, gamma, scale, shift):
    ms = jnp.mean(h * h, axis=-1, keepdims=True)
    return (h * lax.rsqrt(ms + EPS) * gamma) * (1.0 + scale) + shift


def _group_sum(sq, b64):
    return _split_dot(sq, b64, 2)


def _rope(x, cos, sin_signed, nf):
    lane = lax.broadcasted_iota(jnp.int32, (1, LANES), 1)
    first = (lane % (2 * nf)) < nf
    partner = jnp.where(first, pltpu.roll(x, LANES - nf, 1), pltpu.roll(x, nf, 1))
    return x * cos + partner * sin_signed


def _ada_kernel(c_ref, w_ref, b_ref, o_ref):
    x = _silu(c_ref[...])
    o_ref[...] = jnp.dot(x, w_ref[...], precision=lax.Precision.HIGHEST,
                         preferred_element_type=F32) + b_ref[...]


def _ada_call(cond, ada_w, ada_b):
    n_layers, d, n_out = ada_w.shape
    tn = 1536
    return pl.pallas_call(
        _ada_kernel,
        grid=(n_layers, n_out // tn),
        in_specs=[
            pl.BlockSpec((SUBLANES, d), lambda l, j: (0, 0)),
            pl.BlockSpec((None, d, tn), lambda l, j: (l, 0, j)),
            pl.BlockSpec((None, 1, tn), lambda l, j: (l, 0, j)),
        ],
        out_specs=pl.BlockSpec((None, SUBLANES, tn), lambda l, j: (l, 0, j)),
        out_shape=jax.ShapeDtypeStruct((n_layers, SUBLANES, n_out), F32),
        compiler_params=pltpu.CompilerParams(vmem_limit_bytes=VMEM_LIMIT),
        name="ada_mod",
    )(cond, ada_w, ada_b.reshape(n_layers, 1, n_out))


def _inproj_kernel(h_ref, mod_ref, n1_ref, w_ref, rope_ref, b64_ref, sgn_ref, sgw_ref, sgb_ref, gqn_ref, gkn_ref,
                   gqkv_ref, ggate_ref, gab_ref, dq_ref, dk32_ref, dv32_ref, dk16_ref, dv16_ref, oc_ref,
                   gq_ref, gk32_ref, gv32_ref, gk16_ref, gv16_ref):
    tm = h_ref.shape[0]
    xm = _bf(_norm_mod(h_ref[...], n1_ref[...], mod_ref[:, D_MODEL:2 * D_MODEL], mod_ref[:, 0:D_MODEL]))

    def proj(a, b):
        return _dot(xm, w_ref[:, a:b])

    gqkv_ref[...] = proj(C_QKV, C_GATE)
    ggate_ref[...] = proj(C_GATE, C_AB)
    gab_ref[...] = proj(C_AB, C_DQ)

    cos_d, sin_d, cos_g, sin_g = rope_ref[0], rope_ref[1], rope_ref[2], rope_ref[3]
    b64 = b64_ref[...]

    for j in range(BRANCH_W // LANES):
        sl = slice(j * LANES, (j + 1) * LANES)
        q = proj(C_DQ + j * LANES, C_DQ + (j + 1) * LANES)
        dq_ref[:, sl] = _bf(_rope(q, cos_d, sin_d, DIFF_DQK // 4) * DIFF_SCALE)
        k = _rope(proj(C_DK + j * LANES, C_DK + (j + 1) * LANES), cos_d, sin_d, DIFF_DQK // 4)
        dk32_ref[:, sl] = k
        dk16_ref[:, sl] = _bf(k)
    v = proj(C_DV, C_SU)
    dv32_ref[...] = v
    dv16_ref[...] = _bf(v)

    u = proj(C_SU, C_SV)
    sv = proj(C_SV, C_GQ)
    vn = _bf(sv * lax.rsqrt(jnp.mean(sv * sv, axis=-1, keepdims=True) + EPS) * sgn_ref[...])
    group = lax.broadcasted_iota(jnp.int32, (1, BRANCH_W), 1) // HEAD_W
    for r in range(tm // SGU_CHUNK):
        rows = slice(r * SGU_CHUNK, (r + 1) * SGU_CHUNK)
        vc = vn[rows]
        z = sgb_ref[...]
        for g in range(SGU_GROUPS):
            z = z + jnp.where(group == g, _dot(sgw_ref[g], vc), 0.0)
        oc_ref[rows, :] = _bf(u[rows] * z)

    for j in range(BRANCH_W // LANES):
        sl = slice(j * LANES, (j + 1) * LANES)
        q = proj(C_GQ + j * LANES, C_GQ + (j + 1) * LANES)
        qn = q * lax.rsqrt(_group_sum(q * q, b64) * (1.0 / HEAD_W) + EPS) * gqn_ref[...]
        gq_ref[:, sl] = _bf(_rope(qn, cos_g, sin_g, HEAD_W // 4) * GQA_SCALE)
    k = proj(C_GK, C_GV)
    kn = k * lax.rsqrt(_group_sum(k * k, b64) * (1.0 / HEAD_W) + EPS) * gkn_ref[...]
    kr = _rope(kn, cos_g, sin_g, HEAD_W // 4)
    gk32_ref[...] = kr
    gk16_ref[...] = _bf(kr)
    v = proj(C_GV, C_END)
    gv32_ref[...] = v
    gv16_ref[...] = _bf(v)


def _inproj_call(h, mod, n1, w_mix, rope_tab, b64, sgn, sgw, sgb, gqn, gkn, *, tiles_ctx, tiles_per_lat):
    t = h.shape[0]
    n_tiles = t // TM

    def mod_row(i):
        return jnp.where(i < tiles_ctx, 0, 1 + (i - tiles_ctx) // tiles_per_lat)

    def rope_blk(i):
        return jnp.where(i < tiles_ctx, 0, 1 + (i - tiles_ctx) % tiles_per_lat)

    def const(shape):
        return pl.BlockSpec(shape, lambda i: (0,) * len(shape))

    def rows(width):
        return pl.BlockSpec((TM, width), lambda i: (i, 0))

    out_widths_dtypes = [
        (768, F32), (256, F32), (128, F32),
        (256, BF16), (256, F32), (256, F32), (256, BF16), (256, BF16),
        (256, BF16),
        (256, BF16), (128, F32), (128, F32), (128, BF16), (128, BF16),
    ]
    return pl.pallas_call(
        _inproj_kernel,
        grid=(n_tiles,),
        in_specs=[
            rows(D_MODEL),
            pl.BlockSpec((None, 1, 6 * D_MODEL), lambda i: (mod_row(i), 0, 0)),
            const((1, D_MODEL)),
            const((D_MODEL, C_END)),
            pl.BlockSpec((4, TM, LANES), lambda i: (0, rope_blk(i), 0)),
            const((LANES, LANES)),
            const((1, BRANCH_W)),
            const((SGU_GROUPS, SGU_CHUNK, SGU_CHUNK)),
            const((SGU_CHUNK, BRANCH_W)),
            const((1, LANES)),
            const((1, LANES)),
        ],
        out_specs=[rows(w) for w, _ in out_widths_dtypes],
        out_shape=[jax.ShapeDtypeStruct((t, w), dt) for w, dt in out_widths_dtypes],
        compiler_params=pltpu.CompilerParams(dimension_semantics=("parallel",), vmem_limit_bytes=VMEM_LIMIT),
        name="inproj",
    )(h, mod, n1, w_mix, rope_tab, b64, sgn, sgw, sgb, gqn, gkn)


def _gdn_prep_kernel(x_ref, xp_ref, xn_ref, ab_ref, cw_ref, alog_ref, dtb_ref, b64_ref, tril_ref, triu_ref,
                     q_ref, k_ref, v_ref, gb_ref, *, ctx_blocks, ctx_bps, lat_bps):
    r = x_ref.shape[0]
    i = pl.program_id(0)
    is_ctx = i < ctx_blocks
    pos = jnp.where(is_ctx, i % ctx_bps, (i - ctx_blocks) % lat_bps)
    bps = jnp.where(is_ctx, ctx_bps, lat_bps)
    prev = jnp.where(pos > 0, xp_ref[...], 0.0)
    nxt = jnp.where(pos < bps - 1, xn_ref[...], 0.0)
    win = jnp.concatenate([prev, x_ref[...], nxt], axis=0)
    n = r + 2 * HALO
    y = None
    for j in range(CONV_K):
        s = (CONV_K // 2 - j) % n
        shifted = win if s == 0 else pltpu.roll(win, s, 0)
        term = cw_ref[j:j + 1, :] * shifted[HALO:HALO + r]
        y = term if y is None else y + term
    y = _silu(y)

    b64 = b64_ref[...]
    for part, ref, scale in ((0, q_ref, GDN_Q_SCALE), (1, k_ref, 1.0)):
        for j in range(BRANCH_W // LANES):
            x = y[:, part * BRANCH_W + j * LANES: part * BRANCH_W + (j + 1) * LANES]
            xn = x * (lax.rsqrt(_group_sum(x * x, b64) + EPS) * scale)
            for hh in range(LANES // HEAD_W):
                ref[j * (LANES // HEAD_W) + hh] = xn[:, hh * HEAD_W:(hh + 1) * HEAD_W]
    for h in range(N_HEADS):
        v_ref[h] = y[:, 2 * BRANCH_W + h * HEAD_W: 2 * BRANCH_W + (h + 1) * HEAD_W]

    ab = ab_ref[...]
    lane = lax.broadcasted_iota(jnp.int32, (1, LANES), 1)
    lm = lane % 4
    xa = ab + dtb_ref[...]
    softplus = jnp.maximum(xa, 0.0) + jnp.log1p(jnp.exp(-jnp.abs(xa)))
    la = jnp.where((lm < 2) & (lane < 4 * N_HEADS), -jnp.exp(alog_ref[...]) * softplus, 0.0)
    g_fwd = _split_dot_left(tril_ref[...], la, 3)
    g_bwd = _split_dot_left(triu_ref[...], la, 3)
    gb = jnp.where(lm == 0, g_fwd, jnp.where(lm == 1, g_bwd, _sigmoid(ab)))
    for h in range(N_HEADS):
        gb_ref[h] = gb if h == 0 else pltpu.roll(gb, LANES - 4 * h, 1)


def _gdn_prep_call(gqkv, gab, conv_w, alog_row, dtb_row, b64, tril, triu, *, ctx_rows, ctx_len, lat_len):
    t = gqkv.shape[0]
    r = PREP_R
    n_blocks = t // r
    rh = r // HALO
    last_halo = t // HALO - 1
    kern = functools.partial(_gdn_prep_kernel, ctx_blocks=ctx_rows // r, ctx_bps=ctx_len // r, lat_bps=lat_len // r)

    def const(shape):
        return pl.BlockSpec(shape, lambda i: (0,) * len(shape))

    head_major = pl.BlockSpec((N_HEADS, r, HEAD_W), lambda i: (0, i, 0))
    return pl.pallas_call(
        kern,
        grid=(n_blocks,),
        in_specs=[
            pl.BlockSpec((r, 3 * BRANCH_W), lambda i: (i, 0)),
            pl.BlockSpec((HALO, 3 * BRANCH_W), lambda i: (jnp.maximum(i * rh - 1, 0), 0)),
            pl.BlockSpec((HALO, 3 * BRANCH_W), lambda i: (jnp.minimum((i + 1) * rh, last_halo), 0)),
            pl.BlockSpec((r, LANES), lambda i: (i, 0)),
            const((SUBLANES, 3 * BRANCH_W)),
            const((1, LANES)),
            const((1, LANES)),
            const((LANES, LANES)),
            const((r, r)),
            const((r, r)),
        ],
        out_specs=[head_major, head_major, head_major,
                   pl.BlockSpec((N_HEADS, r, LANES), lambda i: (0, i, 0))],
        out_shape=[jax.ShapeDtypeStruct((N_HEADS, t, HEAD_W), F32)] * 3
        + [jax.ShapeDtypeStruct((N_HEADS, t, LANES), F32)],
        compiler_params=pltpu.CompilerParams(dimension_semantics=("parallel",), vmem_limit_bytes=VMEM_LIMIT),
        name="gdn_prep",
    )(gqkv, gqkv, gqkv, gab, conv_w, alog_row, dtb_row, b64, tril, triu)


def _gdn_scan_kernel(*refs, n_seq, seq_len, has_s0):
    if has_s0:
        q_ref, k_ref, v_ref, gb_ref, s0_ref, o_ref, s_ref, oi_scr, qg_scr, m_scr, b_scr = refs
    else:
        q_ref, k_ref, v_ref, gb_ref, o_ref, s_ref, oi_scr, qg_scr, m_scr, b_scr = refs
        s0_ref = None
    c = GDN_CHUNK
    gr = SCAN_G * c
    n_rows = n_seq * seq_len
    nc = seq_len // c
    row_i = lax.broadcasted_iota(jnp.int32, (gr, gr), 0)
    col_i = lax.broadcasted_iota(jnp.int32, (gr, gr), 1)
    eye = jnp.where(row_i == col_i, 1.0, 0.0)
    same = (row_i // c) == (col_i // c)
    incl = (same & (col_i <= row_i), same & (col_i >= row_i))
    strict = (same & (col_i < row_i), same & (col_i > row_i))
    blk16 = (row_i // 16) == (col_i // 16)
    blk32 = ((row_i // 32) == (col_i // 32)) & ~blk16
    blk64 = same & ((row_i // 32) != (col_i // 32))

    def prep_group(gi, carry):
        r0 = pl.multiple_of(gi * gr, gr)
        rows = pl.ds(r0, gr)
        q = q_ref[rows, :]
        k = k_ref[rows, :]
        v = v_ref[rows, :]
        gbc = gb_ref[rows, :]
        gbt = gbc.T
        kb = _bf(k)
        kk = _dot_nt(kb, kb)
        qk = _dot_nt(_bf(q), kb)
        for d in range(2):
            gcol = gbc[:, d:d + 1]
            grow = gbt[d:d + 1, :]
            bcol = gbc[:, 2 + d:3 + d]
            dec = jnp.where(incl[d], jnp.exp(jnp.where(incl[d], gcol - grow, 0.0)), 0.0)
            nmat = jnp.where(strict[d], bcol * kk * dec, 0.0)
            n0 = jnp.where(blk16, nmat, 0.0)
            tmat = eye - n0
            p = n0
            for _ in range(3):
                pb = _bf(p)
                p = _dot(pb, pb)
                tmat = tmat + _dot(_bf(tmat), _bf(p))
            for level in (blk32, blk64):
                tb = _bf(tmat)
                tmat = tmat - _dot(_bf(_dot(tb, _bf(jnp.where(level, nmat, 0.0)))), tb)
            ainv = _bf(tmat)
            eg = jnp.exp(gcol)
            ub = _bf(_dot(ainv, _bf(bcol * v)))
            wkb = _bf(_dot(ainv, _bf((bcol * eg) * k)))
            qkm = _bf(qk * dec)
            oi_scr[d, rows, :] = _dot(qkm, ub)
            qg_scr[d, rows, :] = _bf(q * eg - _dot(qkm, wkb))
            glast = jnp.concatenate(
                [jnp.broadcast_to(gcol[j * c + (c - 1 if d == 0 else 0):j * c + (c if d == 0 else 1), :], (c, 1))
                 for j in range(SCAN_G)], axis=0)
            kdb = _bf(k * jnp.exp(glast - gcol))
            for j in range(SCAN_G):
                cs = slice(j * c, (j + 1) * c)
                crow = pl.ds(r0 + j * c, c)
                m_scr[d, crow, :] = _bf(_dot_tn(kdb[cs], wkb[cs]))
                b_scr[d, crow, :] = _dot_tn(kdb[cs], ub[cs])
        return carry

    lax.fori_loop(0, n_rows // gr, prep_group, 0)

    def scan_step(s, d, ci, state):
        r0 = s * seq_len + ci * c
        if not isinstance(r0, int):
            r0 = pl.multiple_of(r0, c)
        rows = pl.ds(r0, c)
        sb = _bf(state)
        oi_scr[d, rows, :] = oi_scr[d, rows, :] + _dot(qg_scr[d, rows, :], sb)
        last = r0 + (c - 1) if d == 0 else r0
        egl = jnp.exp(gb_ref[pl.ds(last, 1), d:d + 1])
        return egl * state - _dot(m_scr[d, rows, :], sb) + b_scr[d, rows, :]

    def init_state(s, d):
        return s0_ref[s, d] if has_s0 else jnp.zeros((c, c), F32)

    chains = [(s, d) for s in range(n_seq) for d in range(2)]
    states = [init_state(s, d) for s, d in chains]
    if nc <= 8:
        for t in range(nc):
            states = [scan_step(s, d, t if d == 0 else nc - 1 - t, st) for (s, d), st in zip(chains, states)]
    else:
        def body(t, sts):
            return tuple(scan_step(s, d, t if d == 0 else nc - 1 - t, st) for (s, d), st in zip(chains, sts))
        states = lax.fori_loop(0, nc, body, tuple(states))
    for (s, d), st in zip(chains, states):
        s_ref[s, d] = st
    o_ref[...] = oi_scr[0] + oi_scr[1]


def _gdn_scan_call(q, k, v, gb, s0, *, row0, n_seq_total, seq_len, seqs_per_step):
    n_rows = seqs_per_step * seq_len
    blk0 = row0 // n_rows
    has_s0 = s0 is not None
    kern = functools.partial(_gdn_scan_kernel, n_seq=seqs_per_step, seq_len=seq_len, has_s0=has_s0)
    hm = pl.BlockSpec((None, n_rows, HEAD_W), lambda s, h: (h, blk0 + s, 0))
    st_spec = pl.BlockSpec((seqs_per_step, 2, None, HEAD_W, HEAD_W), lambda s, h: (s, 0, h, 0, 0))
    in_specs = [hm, hm, hm, pl.BlockSpec((None, n_rows, LANES), lambda s, h: (h, blk0 + s, 0))]
    args = [q, k, v, gb]
    if has_s0:
        in_specs.append(st_spec)
        args.append(s0)
    return pl.pallas_call(
        kern,
        grid=(n_seq_total // seqs_per_step, N_HEADS),
        in_specs=in_specs,
        out_specs=[pl.BlockSpec((None, n_rows, HEAD_W), lambda s, h: (h, s, 0)), st_spec],
        out_shape=[jax.ShapeDtypeStruct((N_HEADS, n_seq_total * seq_len, HEAD_W), F32),
                   jax.ShapeDtypeStruct((n_seq_total, 2, N_HEADS, HEAD_W, HEAD_W), F32)],
        scratch_shapes=[
            pltpu.VMEM((2, n_rows, HEAD_W), F32),
            pltpu.VMEM((2, n_rows, HEAD_W), BF16),
            pltpu.VMEM((2, n_rows, HEAD_W), BF16),
            pltpu.VMEM((2, n_rows, HEAD_W), F32),
        ],
        compiler_params=pltpu.CompilerParams(dimension_semantics=("parallel", "parallel"),
                                             vmem_limit_bytes=VMEM_LIMIT),
        name="gdn_scan",
    )(*args)


def _softmax_rows(scores):
    mx = None
    for s in scores:
        m = jnp.max(s, axis=-1, keepdims=True)
        mx = m if mx is None else jnp.maximum(mx, m)
    ps = [jnp.exp(s - mx) for s in scores]
    tot = None
    for p in ps:
        l = jnp.sum(p, axis=-1, keepdims=True)
        tot = l if tot is None else tot + l
    inv = 1.0 / tot
    return [p * inv for p in ps]


def _attn_kernel(*refs, has_cache, lam_init):
    if has_cache:
        (dq_ref, dk_ref, dv_ref, gq_ref, gk_ref, gv_ref, cdk_ref, cdv_ref, cgk_ref, cgv_ref,
         lp_ref, dn_ref, b64_ref, ob_ref, od_ref) = refs
        dks, dvs = [cdk_ref[...], dk_ref[...]], [cdv_ref[...], dv_ref[...]]
        gks, gvs = [cgk_ref[...], gk_ref[...]], [cgv_ref[...], gv_ref[...]]
    else:
        dq_ref, dk_ref, dv_ref, gq_ref, gk_ref, gv_ref, lp_ref, dn_ref, b64_ref, ob_ref, od_ref = refs
        dks, dvs = [dk_ref[...]], [dv_ref[...]]
        gks, gvs = [gk_ref[...]], [gv_ref[...]]

    lp = lp_ref[...]
    lam = (jnp.exp(jnp.sum(lp[0:1] * lp[1:2], axis=-1, keepdims=True))
           - jnp.exp(jnp.sum(lp[2:3] * lp[3:4], axis=-1, keepdims=True)) + lam_init)

    lane = lax.broadcasted_iota(jnp.int32, (1, BRANCH_W), 1)
    dq = dq_ref[...]
    o = None
    for h in range(N_HEADS):
        pn = []
        for m in range(2):
            lo = h * HEAD_W + m * DIFF_DQK
            qm = dq * jnp.where((lane >= lo) & (lane < lo + DIFF_DQK), 1.0, 0.0).astype(BF16)
            pn.append(_softmax_rows([_dot_nt(qm, ks) for ks in dks]))
        oh = None
        for p1, p2, vs in zip(pn[0], pn[1], dvs):
            t = _dot(_bf(p1 - lam * p2), vs)
            oh = t if oh is None else oh + t
        oh = jnp.where(lane // HEAD_W == h, oh, 0.0)
        o = oh if o is None else o + oh
    b64 = b64_ref[...]
    for j in range(BRANCH_W // LANES):
        sl = slice(j * LANES, (j + 1) * LANES)
        x = o[:, sl]
        ms = _group_sum(x * x, b64) * (1.0 / HEAD_W)
        ob_ref[:, sl] = _bf(x * lax.rsqrt(ms + EPS) * dn_ref[:, sl] * (1.0 - lam_init))

    lane_h = lax.broadcasted_iota(jnp.int32, (1, LANES), 1)
    for half in range(2):
        qh = gq_ref[:, half * LANES:(half + 1) * LANES]
        acc = None
        for g in range(GQA_KV_HEADS):
            sel = lane_h // HEAD_W == g
            qm = qh * jnp.where(sel, 1.0, 0.0).astype(BF16)
            pn = _softmax_rows([_dot_nt(qm, ks) for ks in gks])
            og = None
            for p, vs in zip(pn, gvs):
                t = _dot(_bf(p), vs)
                og = t if og is None else og + t
            og = jnp.where(sel, og, 0.0)
            acc = og if acc is None else acc + og
        od_ref[:, half * LANES:(half + 1) * LANES] = _bf(acc)


def _attn_ctx_call(dq, dk, dv, gq, gk, gv, lp, dn, b64, *, n_seq, seq_len, lam_init):
    kern = functools.partial(_attn_kernel, has_cache=False, lam_init=lam_init)

    def rows(width):
        return pl.BlockSpec((seq_len, width), lambda s: (s, 0))

    def const(shape):
        return pl.BlockSpec(shape, lambda s: (0,) * len(shape))

    return pl.pallas_call(
        kern,
        grid=(n_seq,),
        in_specs=[rows(256), rows(256), rows(256), rows(256), rows(128), rows(128),
                  const((4, DIFF_DQK)), const((1, BRANCH_W)), const((LANES, LANES))],
        out_specs=[rows(256), rows(256)],
        out_shape=[jax.ShapeDtypeStruct((n_seq * seq_len, BRANCH_W), BF16)] * 2,
        compiler_params=pltpu.CompilerParams(dimension_semantics=("parallel",), vmem_limit_bytes=VMEM_LIMIT),
        name="attn_ctx",
    )(dq, dk, dv, gq, gk, gv, lp, dn, b64)


def _attn_lat_call(dq, dk, dv, gq, gk, gv, cdk, cdv, cgk, cgv, lp, dn, b64, *, row0, n_seq, seq_len, lam_init):
    kern = functools.partial(_attn_kernel, has_cache=True, lam_init=lam_init)
    tq = ATTN_TQ
    qpb = seq_len // tq
    q0 = row0 // tq
    s0 = row0 // seq_len
    past = cdk.shape[1]

    def qrows(width):
        return pl.BlockSpec((tq, width), lambda b, i: (q0 + b * qpb + i, 0))

    def kvrows(width):
        return pl.BlockSpec((seq_len, width), lambda b, i: (s0 + b, 0))

    def cache(width):
        return pl.BlockSpec((None, past, width), lambda b, i: (b, 0, 0))

    def const(shape):
        return pl.BlockSpec(shape, lambda b, i: (0,) * len(shape))

    out = pl.BlockSpec((tq, BRANCH_W), lambda b, i: (b * qpb + i, 0))
    return pl.pallas_call(
        kern,
        grid=(n_seq, qpb),
        in_specs=[qrows(256), kvrows(256), kvrows(256), qrows(256), kvrows(128), kvrows(128),
                  cache(256), cache(256), cache(128), cache(128),
                  const((4, DIFF_DQK)), const((1, BRANCH_W)), const((LANES, LANES))],
        out_specs=[out, out],
        out_shape=[jax.ShapeDtypeStruct((n_seq * seq_len, BRANCH_W), BF16)] * 2,
        compiler_params=pltpu.CompilerParams(dimension_semantics=("parallel", "arbitrary"),
                                             vmem_limit_bytes=VMEM_LIMIT),
        name="attn_lat",
    )(dq, dk, dv, gq, gk, gv, cdk, cdv, cgk, cgv, lp, dn, b64)


def _merge_kernel(h_ref, mod_ref, n1_ref, oa_ref, gate_ref, gn_ref, ob_ref, oc_ref, od_ref,
                  wmg_ref, wb_ref, wo_ref, b64_ref, out_ref):
    h = h_ref[...]
    xm = _bf(_norm_mod(h, n1_ref[...], mod_ref[:, D_MODEL:2 * D_MODEL], mod_ref[:, 0:D_MODEL]))
    g1 = mod_ref[:, 2 * D_MODEL:3 * D_MODEL]
    b64 = b64_ref[...]
    oa_parts = []
    for j in range(BRANCH_W // LANES):
        sl = slice(j * LANES, (j + 1) * LANES)
        x = jnp.concatenate([oa_ref[2 * j], oa_ref[2 * j + 1]], axis=-1)
        ms = _group_sum(x * x, b64) * (1.0 / HEAD_W)
        oa_parts.append(_bf(x * lax.rsqrt(ms + EPS) * gn_ref[:, sl] * _silu(gate_ref[:, sl])))
    merged = None
    for n in range(4):
        mg = _dot(xm, wmg_ref[:, n * D_MODEL:(n + 1) * D_MODEL])
        if n == 0:
            up = _dot(oa_parts[0], wb_ref[0, 0:LANES, :]) + _dot(oa_parts[1], wb_ref[0, LANES:2 * LANES, :])
        else:
            up = _dot((ob_ref, oc_ref, od_ref)[n - 1][...], wb_ref[n])
        term = _sigmoid(mg) * up
        merged = term if merged is None else merged + term
    out_ref[...] = h + g1 * _dot(_bf(merged), wo_ref[...])


def _merge_call(h, mod, n1, oa, gate, gn, ob, oc, od, wmg, wb, wo, b64, *, tiles_ctx, tiles_per_lat):
    t = h.shape[0]

    def mod_row(i):
        return jnp.where(i < tiles_ctx, 0, 1 + (i - tiles_ctx) // tiles_per_lat)

    def const(shape):
        return pl.BlockSpec(shape, lambda i: (0,) * len(shape))

    def rows(width):
        return pl.BlockSpec((TM, width), lambda i: (i, 0))

    return pl.pallas_call(
        _merge_kernel,
        grid=(t // TM,),
        in_specs=[
            rows(D_MODEL),
            pl.BlockSpec((None, 1, 6 * D_MODEL), lambda i: (mod_row(i), 0, 0)),
            const((1, D_MODEL)),
            pl.BlockSpec((N_HEADS, TM, HEAD_W), lambda i: (0, i, 0)),
            rows(BRANCH_W),
            const((1, BRANCH_W)),
            rows(BRANCH_W), rows(BRANCH_W), rows(BRANCH_W),
            const((D_MODEL, 4 * D_MODEL)),
            const((4, BRANCH_W, D_MODEL)),
            const((D_MODEL, D_MODEL)),
            const((LANES, LANES)),
        ],
        out_specs=rows(D_MODEL),
        out_shape=jax.ShapeDtypeStruct((t, D_MODEL), F32),
        compiler_params=pltpu.CompilerParams(dimension_semantics=("parallel",), vmem_limit_bytes=VMEM_LIMIT),
        name="merge",
    )(h, mod, n1, oa, gate, gn, ob, oc, od, wmg, wb, wo, b64)


ROUTER_ROWS = 32
ROUTER_E0 = 8


def _moe_kernel(h_ref, mod_ref, n2_ref, wr_ref, br_ref, w1_ref, w3_ref, w2_ref, out_ref, xm_scr, gates_scr, acc_scr):
    e = pl.program_id(1)
    tm = h_ref.shape[0]

    @pl.when(e == 0)
    def _():
        xm = _norm_mod(h_ref[...], n2_ref[...], mod_ref[:, 4 * D_MODEL:5 * D_MODEL],
                       mod_ref[:, 3 * D_MODEL:4 * D_MODEL])
        xm_scr[...] = _bf(xm)
        lt = lax.dot_general(wr_ref[...], xm, (((1,), (1,)), ((), ())), precision=lax.Precision.HIGHEST,
                             preferred_element_type=F32) + br_ref[...]
        gl = lt[0:N_GROUPS]
        gidx = lax.broadcasted_iota(jnp.int32, (N_GROUPS, 1), 0)
        gmax = jnp.max(gl, axis=0, keepdims=True)
        gsel = jnp.min(jnp.where(gl == gmax, gidx, N_GROUPS), axis=0, keepdims=True)
        gprob = 1.0 / jnp.sum(jnp.exp(gl - gmax), axis=0, keepdims=True)
        el = lt[ROUTER_E0:ROUTER_E0 + N_EXPERTS]
        eidx = lax.broadcasted_iota(jnp.int32, (N_EXPERTS, 1), 0)
        neg = -jnp.inf
        cand = jnp.where(eidx // EXPERTS_PER_GROUP == gsel, el, neg)
        v1 = jnp.max(cand, axis=0, keepdims=True)
        i1 = jnp.min(jnp.where(cand == v1, eidx, N_EXPERTS), axis=0, keepdims=True)
        cand2 = jnp.where(eidx == i1, neg, cand)
        v2 = jnp.max(cand2, axis=0, keepdims=True)
        i2 = jnp.min(jnp.where(cand2 == v2, eidx, N_EXPERTS), axis=0, keepdims=True)
        ex = jnp.exp(v2 - v1)
        wa = 1.0 / (1.0 + ex)
        gates_t = (jnp.where(eidx == i1, wa, 0.0) + jnp.where(eidx == i2, ex * wa, 0.0)) * gprob
        full = jnp.concatenate([gates_t, jnp.zeros((LANES - N_EXPERTS, tm), F32)], axis=0)
        gates_scr[...] = full.T
        acc_scr[...] = jnp.zeros_like(acc_scr)

    xm = xm_scr[...]
    h1 = _dot(xm, w1_ref[...])
    h3 = _dot(xm, w3_ref[...])
    lane = lax.broadcasted_iota(jnp.int32, (1, LANES), 1)
    gate = jnp.sum(jnp.where(lane == e, gates_scr[...], 0.0), axis=-1, keepdims=True)
    hd = _silu(h1) * h3 * gate
    acc_scr[...] += _dot(_bf(hd), w2_ref[...])

    @pl.when(e == N_EXPERTS - 1)
    def _():
        out_ref[...] = h_ref[...] + mod_ref[:, 5 * D_MODEL:6 * D_MODEL] * acc_scr[...]


def _moe_call(h, mod, n2, wr, br, w1, w3, w2, *, tiles_ctx, tiles_per_lat):
    t = h.shape[0]

    def mod_row(i):
        return jnp.where(i < tiles_ctx, 0, 1 + (i - tiles_ctx) // tiles_per_lat)

    return pl.pallas_call(
        _moe_kernel,
        grid=(t // TM, N_EXPERTS),
        in_specs=[
            pl.BlockSpec((TM, D_MODEL), lambda i, e: (i, 0)),
            pl.BlockSpec((None, 1, 6 * D_MODEL), lambda i, e: (mod_row(i), 0, 0)),
            pl.BlockSpec((1, D_MODEL), lambda i, e: (0, 0)),
            pl.BlockSpec((ROUTER_ROWS, D_MODEL), lambda i, e: (0, 0)),
            pl.BlockSpec((ROUTER_ROWS, 1), lambda i, e: (0, 0)),
            pl.BlockSpec((None, D_MODEL, D_EXPERT), lambda i, e: (e, 0, 0)),
            pl.BlockSpec((None, D_MODEL, D_EXPERT), lambda i, e: (e, 0, 0)),
            pl.BlockSpec((None, D_EXPERT, D_MODEL), lambda i, e: (e, 0, 0)),
        ],
        out_specs=pl.BlockSpec((TM, D_MODEL), lambda i, e: (i, 0)),
        out_shape=jax.ShapeDtypeStruct((t, D_MODEL), F32),
        scratch_shapes=[pltpu.VMEM((TM, D_MODEL), BF16), pltpu.VMEM((TM, LANES), F32),
                        pltpu.VMEM((TM, D_MODEL), F32)],
        compiler_params=pltpu.CompilerParams(dimension_semantics=("parallel", "arbitrary"),
                                             vmem_limit_bytes=VMEM_LIMIT),
        name="moe",
    )(h, mod, n2, wr, br, w1, w3, w2)


def _final_norm_kernel(h_ref, g_ref, o_ref):
    h = h_ref[...]
    o_ref[...] = h * lax.rsqrt(jnp.mean(h * h, axis=-1, keepdims=True) + EPS) * g_ref[...]


def _final_norm_call(h, g):
    t = h.shape[0]
    return pl.pallas_call(
        _final_norm_kernel,
        grid=(t // TM,),
        in_specs=[pl.BlockSpec((TM, D_MODEL), lambda i: (i, 0)), pl.BlockSpec((1, D_MODEL), lambda i: (0, 0))],
        out_specs=pl.BlockSpec((TM, D_MODEL), lambda i: (i, 0)),
        out_shape=jax.ShapeDtypeStruct((t, D_MODEL), F32),
        compiler_params=pltpu.CompilerParams(dimension_semantics=("parallel",)),
        name="final_norm",
    )(h, g)


def _mix_column_perm():
    cols = list(range(_O_QKV, _O_QKV + 768)) + list(range(_O_GATE, _O_GATE + 256))
    ab = []
    for h in range(N_HEADS):
        ab += [_O_A + h, _O_A + N_HEADS + h, _O_B + h, _O_B + N_HEADS + h]
    cols += ab + [-1] * (LANES - len(ab))
    cols += list(range(_O_DQ, _O_DQ + 768))
    cols += list(range(_O_SU, _O_SU + 512))
    for h in (0, 2, 1, 3):
        cols += list(range(_O_GQ + h * HEAD_W, _O_GQ + (h + 1) * HEAD_W))
    cols += list(range(_O_GK, _O_GK + 256))
    assert len(cols) == C_END
    return cols


def _rope_tables(lat_len):
    pos = jnp.arange(lat_len, dtype=jnp.int32)
    row = (pos // GRID_W).astype(F32)[:, None]
    col = (pos % GRID_W).astype(F32)[:, None]
    lane = jnp.arange(LANES, dtype=jnp.int32)[None, :]
    tabs = []
    for head_w in (DIFF_DQK, HEAD_W):
        half = head_w // 2
        nf = half // 2
        p = lane % head_w
        f = (p % half) % nf
        inv = ROPE_THETA ** (-f.astype(F32) / nf)
        ang = jnp.where(p < half, row, col) * inv
        sign = jnp.where((p % half) < nf, -1.0, 1.0)
        tabs += [jnp.cos(ang), jnp.sin(ang) * sign]
    tab = jnp.stack(tabs)
    ident = jnp.stack([jnp.ones((TM, LANES), F32), jnp.zeros((TM, LANES), F32)] * 2)
    return jnp.concatenate([ident, tab], axis=1)


def _chunk_tri(n, chunk, upper):
    i = jnp.arange(n)[:, None]
    j = jnp.arange(n)[None, :]
    same = (i // chunk) == (j // chunk)
    tri = (j >= i) if upper else (j <= i)
    return jnp.where(same & tri, 1.0, 0.0).astype(BF16)


def kernel(x_prompt, x_sample, state_gdn, cache_diff_k, cache_diff_v, cache_gqa_k, cache_gqa_v, c, c_ctx,
           ada_w, ada_b, norm1, norm2, w_in, gdn_conv, gdn_a_log, gdn_dt_bias, gdn_norm, diff_lambda, diff_norm,
           sgu_norm, sgu_ws, sgu_b, gqa_qnorm, gqa_knorm, w_branch, w_out, moe_wg, moe_bg, moe_we, moe_be,
           moe_w1, moe_w3, moe_w2, final_norm):
    nb, seq, d = x_prompt.shape
    nl, lat_len, _ = x_sample.shape
    depth = ada_w.shape[0]
    past = cache_diff_k.shape[2]
    ctx_rows = nb * seq
    lat_rows = nl * lat_len
    assert d == D_MODEL and seq % PREP_R == 0 and lat_len % TM == 0 and ctx_rows % lat_len == 0
    assert ctx_rows % TM == 0 and 1 + nl <= SUBLANES and lat_len % GRID_W == 0
    tiles_ctx = ctx_rows // TM
    tiles_per_lat = lat_len // TM
    tile_kw = dict(tiles_ctx=tiles_ctx, tiles_per_lat=tiles_per_lat)

    lane = jnp.arange(LANES)
    b64 = jnp.where((lane[:, None] // HEAD_W) == (lane[None, :] // HEAD_W), 1.0, 0.0).astype(BF16)
    tril = _chunk_tri(PREP_R, GDN_CHUNK, upper=False)
    triu = _chunk_tri(PREP_R, GDN_CHUNK, upper=True)
    rope_tab = _rope_tables(lat_len)

    perm = jnp.asarray(_mix_column_perm(), dtype=jnp.int32)
    w_mix = _bf(jnp.where(perm[None, None, :] >= 0, jnp.take(w_in, jnp.maximum(perm, 0), axis=2), 0.0))
    w_mg = _bf(w_in[:, :, _O_MG:])
    head_perm = jnp.asarray([r for h in (0, 2, 1, 3) for r in range(h * HEAD_W, (h + 1) * HEAD_W)], dtype=jnp.int32)
    wb = _bf(jnp.concatenate([w_branch[:, :3], jnp.take(w_branch[:, 3], head_perm, axis=1)[:, None]], axis=1))
    wo = _bf(w_out)
    w1, w3, w2 = _bf(moe_w1), _bf(moe_w3), _bf(moe_w2)
    wr = jnp.zeros((depth, ROUTER_ROWS, d), F32)
    wr = wr.at[:, 0:N_GROUPS].set(jnp.swapaxes(moe_wg, 1, 2))
    wr = wr.at[:, ROUTER_E0:ROUTER_E0 + N_EXPERTS].set(jnp.swapaxes(moe_we, 1, 2))
    br = jnp.zeros((depth, ROUTER_ROWS, 1), F32)
    br = br.at[:, 0:N_GROUPS, 0].set(moe_bg).at[:, ROUTER_E0:ROUTER_E0 + N_EXPERTS, 0].set(moe_be)
    conv_w = jnp.zeros((depth, SUBLANES, 3 * BRANCH_W), F32).at[:, :CONV_K].set(gdn_conv)
    ab_lane = jnp.zeros((depth, 1, LANES), F32)
    alog_row, dtb_row = ab_lane, ab_lane
    for h in range(N_HEADS):
        for dd in range(2):
            alog_row = alog_row.at[:, 0, 4 * h + dd].set(gdn_a_log[:, dd, h])
            dtb_row = dtb_row.at[:, 0, 4 * h + dd].set(gdn_dt_bias[:, dd, h])
    sgw = _bf(sgu_ws)
    sgb = jnp.repeat(jnp.swapaxes(sgu_b, 1, 2), HEAD_W, axis=2)
    gdn_gn = jnp.tile(gdn_norm, (1, N_HEADS))[:, None, :]
    diff_gn = jnp.tile(diff_norm, (1, N_HEADS))[:, None, :]
    gqn = jnp.tile(gqa_qnorm, (1, LANES // HEAD_W))[:, None, :]
    gkn = jnp.tile(gqa_knorm, (1, LANES // HEAD_W))[:, None, :]

    cond = jnp.zeros((SUBLANES, d), F32).at[0].set(c_ctx).at[1:1 + nl].set(c)
    mods = _ada_call(cond, ada_w, ada_b)

    h = jnp.concatenate([x_prompt.reshape(ctx_rows, d), x_sample.reshape(lat_rows, d)], axis=0)
    new_s, new_dk, new_dv, new_gk, new_gv = [], [], [], [], []
    for l in range(depth):
        lam_init = 0.8 - 0.6 * math.exp(-0.3 * l)
        mod = mods[l][:, None, :]
        n1 = norm1[l][None, :]
        (gqkv, ggate, gab, dq, dk32, dv32, dk16, dv16, oc, gq, gk32, gv32, gk16, gv16) = _inproj_call(
            h, mod, n1, w_mix[l], rope_tab, b64, sgu_norm[l][None, :], sgw[l], sgb[l], gqn[l], gkn[l], **tile_kw)

        qh, kh, vh, gb = _gdn_prep_call(gqkv, gab, conv_w[l], alog_row[l], dtb_row[l], b64, tril, triu,
                                        ctx_rows=ctx_rows, ctx_len=seq, lat_len=lat_len)
        oa_ctx, s_ctx = _gdn_scan_call(qh, kh, vh, gb, None, row0=0, n_seq_total=nb, seq_len=seq,
                                       seqs_per_step=8 if nb % 8 == 0 else 1)
        oa_lat, _ = _gdn_scan_call(qh, kh, vh, gb, state_gdn[:, l], row0=ctx_rows, n_seq_total=nl,
                                   seq_len=lat_len, seqs_per_step=1)
        oa = jnp.concatenate([oa_ctx, oa_lat], axis=1)

        lp = diff_lambda[l]
        ob_ctx, od_ctx = _attn_ctx_call(dq, dk16, dv16, gq, gk16, gv16, lp, diff_gn[l], b64,
                                        n_seq=nb, seq_len=seq, lam_init=lam_init)
        ob_lat, od_lat = _attn_lat_call(
            dq, dk16, dv16, gq, gk16, gv16,
            _bf(cache_diff_k[:, l].reshape(nl, past, BRANCH_W)), _bf(cache_diff_v[:, l].reshape(nl, past, BRANCH_W)),
            _bf(cache_gqa_k[:, l].reshape(nl, past, LANES)), _bf(cache_gqa_v[:, l].reshape(nl, past, LANES)),
            lp, diff_gn[l], b64, row0=ctx_rows, n_seq=nl, seq_len=lat_len, lam_init=lam_init)
        ob = jnp.concatenate([ob_ctx, ob_lat], axis=0)
        od = jnp.concatenate([od_ctx, od_lat], axis=0)

        h = _merge_call(h, mod, n1, oa, ggate, gdn_gn[l], ob, oc, od, w_mg[l], wb[l], wo[l], b64, **tile_kw)
        h = _moe_call(h, mod, norm2[l][None, :], wr[l], br[l], w1[l], w3[l], w2[l], **tile_kw)

        new_s.append(s_ctx)
        new_dk.append(dk32[:ctx_rows].reshape(nb, seq, N_HEADS, HEAD_W))
        new_dv.append(dv32[:ctx_rows].reshape(nb, seq, N_HEADS, HEAD_W))
        new_gk.append(gk32[:ctx_rows].reshape(nb, seq, GQA_KV_HEADS, HEAD_W))
        new_gv.append(gv32[:ctx_rows].reshape(nb, seq, GQA_KV_HEADS, HEAD_W))

    y = _final_norm_call(h, final_norm[None, :])
    return (y[:ctx_rows].reshape(nb, seq, d), y[ctx_rows:].reshape(nl, lat_len, d),
            jnp.stack(new_s, axis=1), jnp.stack(new_dk, axis=1), jnp.stack(new_dv, axis=1),
            jnp.stack(new_gk, axis=1), jnp.stack(new_gv, axis=1))
```

```python
import functools
import math

import jax
import jax.numpy as jnp
from jax import lax
from jax.experimental import pallas as pl
from jax.experimental.pallas import tpu as pltpu

F32 = jnp.float32
BF16 = jnp.bfloat16

GRID_W = 64
ROPE_THETA = 10000.0
EPS = 1e-6
D_MODEL = 1024
BRANCH_W = 256
HEAD_W = 64
DIFF_DQK = 32
N_HEADS = 4
GQA_KV_HEADS = 2
GDN_CHUNK = 64
CONV_K = 5
SGU_CHUNK = 128
SGU_GROUPS = 4
N_GROUPS = 4
EXPERTS_PER_GROUP = 4
N_EXPERTS = 16
D_EXPERT = 256
LOG2E = 1.4426950408889634
DIFF_SCALE = DIFF_DQK ** -0.5 * LOG2E
GQA_SCALE = HEAD_W ** -0.5 * LOG2E
GDN_Q_SCALE = HEAD_W ** -0.5

LANES = 128
SUBLANES = 8
VMEM_LIMIT = 56 * 1024 * 1024

_IN_WIDTHS = (768, 256, 8, 8, 256, 256, 256, 256, 256, 256, 128, 128, 4096)
_IN_OFFS = tuple(sum(_IN_WIDTHS[:i]) for i in range(len(_IN_WIDTHS)))
(_O_QKV, _O_GATE, _O_A, _O_B, _O_DQ, _O_DK, _O_DV, _O_SU, _O_SV, _O_GQ, _O_GK, _O_GV, _O_MG) = _IN_OFFS
MIX_COLS = _O_MG

C_QKV, C_GATE, C_AB, C_DQ, C_DK, C_DV, C_SU, C_SV, C_GQ, C_GK, C_GV, C_END = (
    0, 768, 1024, 1152, 1408, 1664, 1920, 2176, 2432, 2688, 2816, 2944)

TM = 512
PREP_R = 256
HALO = SUBLANES
SCAN_G = 4
PREP_PAR = 2
ATTN_TQ = 256


def _bf(x):
    return x.astype(BF16)


def _dot(a, b):
    return jnp.dot(a, b, preferred_element_type=F32)


def _dot_nt(a, b):
    return lax.dot_general(a, b, (((1,), (1,)), ((), ())), preferred_element_type=F32)


def _dot_tn(a, b):
    return lax.dot_general(a, b, (((0,), (0,)), ((), ())), preferred_element_type=F32)


def _split_dot(x, m, terms):
    acc = None
    r = x
    for t in range(terms):
        p = _bf(r)
        d = _dot(p, m)
        acc = d if acc is None else acc + d
        if t + 1 < terms:
            r = r - p.astype(F32)
    return acc


def _split_dot_left(m, x, terms):
    acc = None
    r = x
    for t in range(terms):
        p = _bf(r)
        d = _dot(m, p)
        acc = d if acc is None else acc + d
        if t + 1 < terms:
            r = r - p.astype(F32)
    return acc


def _sigmoid(x):
    return 1.0 / (1.0 + jnp.exp(-x))


def _silu(x):
    return x * _sigmoid(x)


def _norm_mod(h, gamma, scale, shift):
    ms = jnp.mean(h * h, axis=-1, keepdims=True)
    return (h * lax.rsqrt(ms + EPS) * gamma) * (1.0 + scale) + shift


def _group_sum(sq, b64):
    return _split_dot(sq, b64, 2)


def _rope(x, cos, sin_signed, nf):
    lane = lax.broadcasted_iota(jnp.int32, (1, LANES), 1)
    first = (lane % (2 * nf)) < nf
    partner = jnp.where(first, pltpu.roll(x, LANES - nf, 1), pltpu.roll(x, nf, 1))
    return x * cos + partner * sin_signed


def _ada_kernel(c_ref, w_ref, b_ref, o_ref):
    x = _silu(c_ref[...])
    o_ref[...] = jnp.dot(x, w_ref[...], precision=lax.Precision.HIGHEST,
                         preferred_element_type=F32) + b_ref[...]


def _ada_call(cond, ada_w, ada_b):
    n_layers, d, n_out = ada_w.shape
    tn = 1536
    return pl.pallas_call(
        _ada_kernel,
        grid=(n_layers, n_out // tn),
        in_specs=[
            pl.BlockSpec((SUBLANES, d), lambda l, j: (0, 0)),
            pl.BlockSpec((None, d, tn), lambda l, j: (l, 0, j)),
            pl.BlockSpec((None, 1, tn), lambda l, j: (l, 0, j)),
        ],
        out_specs=pl.BlockSpec((None, SUBLANES, tn), lambda l, j: (l, 0, j)),
        out_shape=jax.ShapeDtypeStruct((n_layers, SUBLANES, n_out), F32),
        compiler_params=pltpu.CompilerParams(vmem_limit_bytes=VMEM_LIMIT),
        name="ada_mod",
    )(cond, ada_w, ada_b.reshape(n_layers, 1, n_out))


def _inproj_kernel(h_ref, mod_ref, n1_ref, w_ref, rope_ref, b64_ref, sgn_ref, sgw_ref, sgb_ref, gqn_ref, gkn_ref,
                   gqkv_ref, ggate_ref, gab_ref, dq_ref, dk32_ref, dv32_ref, dk16_ref, dv16_ref, oc_ref,
                   gq_ref, gk32_ref, gv32_ref, gk16_ref, gv16_ref):
    tm = h_ref.shape[0]
    xm = _bf(_norm_mod(h_ref[...], n1_ref[...], mod_ref[:, D_MODEL:2 * D_MODEL], mod_ref[:, 0:D_MODEL]))

    def proj(a, b):
        return _dot(xm, w_ref[:, a:b])

    gqkv_ref[...] = proj(C_QKV, C_GATE)
    ggate_ref[...] = proj(C_GATE, C_AB)
    gab_ref[...] = proj(C_AB, C_DQ)

    cos_d, sin_d, cos_g, sin_g = rope_ref[0], rope_ref[1], rope_ref[2], rope_ref[3]
    b64 = b64_ref[...]

    for j in range(BRANCH_W // LANES):
        sl = slice(j * LANES, (j + 1) * LANES)
        q = proj(C_DQ + j * LANES, C_DQ + (j + 1) * LANES)
        dq_ref[:, sl] = _bf(_rope(q, cos_d, sin_d, DIFF_DQK // 4) * DIFF_SCALE)
        k = _rope(proj(C_DK + j * LANES, C_DK + (j + 1) * LANES), cos_d, sin_d, DIFF_DQK // 4)
        dk32_ref[:, sl] = k
        dk16_ref[:, sl] = _bf(k)
    v = proj(C_DV, C_SU)
    dv32_ref[...] = v
    dv16_ref[...] = _bf(v)

    u = proj(C_SU, C_SV)
    sv = proj(C_SV, C_GQ)
    vn = _bf(sv * lax.rsqrt(jnp.mean(sv * sv, axis=-1, keepdims=True) + EPS) * sgn_ref[...])
    group = lax.broadcasted_iota(jnp.int32, (1, BRANCH_W), 1) // HEAD_W
    for r in range(tm // SGU_CHUNK):
        rows = slice(r * SGU_CHUNK, (r + 1) * SGU_CHUNK)
        vc = vn[rows]
        z = sgb_ref[...]
        for g in range(SGU_GROUPS):
            z = z + jnp.where(group == g, _dot(sgw_ref[g], vc), 0.0)
        oc_ref[rows, :] = _bf(u[rows] * z)

    for j in range(BRANCH_W // LANES):
        sl = slice(j * LANES, (j + 1) * LANES)
        q = proj(C_GQ + j * LANES, C_GQ + (j + 1) * LANES)
        qn = q * lax.rsqrt(_group_sum(q * q, b64) * (1.0 / HEAD_W) + EPS) * gqn_ref[...]
        gq_ref[:, sl] = _bf(_rope(qn, cos_g, sin_g, HEAD_W // 4) * GQA_SCALE)
    k = proj(C_GK, C_GV)
    kn = k * lax.rsqrt(_group_sum(k * k, b64) * (1.0 / HEAD_W) + EPS) * gkn_ref[...]
    kr = _rope(kn, cos_g, sin_g, HEAD_W // 4)
    gk32_ref[...] = kr
    gk16_ref[...] = _bf(kr)
    v = proj(C_GV, C_END)
    gv32_ref[...] = v
    gv16_ref[...] = _bf(v)


def _inproj_call(h, mod, n1, w_mix, rope_tab, b64, sgn, sgw, sgb, gqn, gkn, *, tiles_ctx, tiles_per_lat):
    t = h.shape[0]
    n_tiles = t // TM

    def mod_row(i):
        return jnp.where(i < tiles_ctx, 0, 1 + (i - tiles_ctx) // tiles_per_lat)

    def rope_blk(i):
        return jnp.where(i < tiles_ctx, 0, 1 + (i - tiles_ctx) % tiles_per_lat)

    def const(shape):
        return pl.BlockSpec(shape, lambda i: (0,) * len(shape))

    def rows(width):
        return pl.BlockSpec((TM, width), lambda i: (i, 0))

    out_widths_dtypes = [
        (768, F32), (256, F32), (128, F32),
        (256, BF16), (256, F32), (256, F32), (256, BF16), (256, BF16),
        (256, BF16),
        (256, BF16), (128, F32), (128, F32), (128, BF16), (128, BF16),
    ]
    return pl.pallas_call(
        _inproj_kernel,
        grid=(n_tiles,),
        in_specs=[
            rows(D_MODEL),
            pl.BlockSpec((None, 1, 6 * D_MODEL), lambda i: (mod_row(i), 0, 0)),
            const((1, D_MODEL)),
            const((D_MODEL, C_END)),
            pl.BlockSpec((4, TM, LANES), lambda i: (0, rope_blk(i), 0)),
            const((LANES, LANES)),
            const((1, BRANCH_W)),
            const((SGU_GROUPS, SGU_CHUNK, SGU_CHUNK)),
            const((SGU_CHUNK, BRANCH_W)),
            const((1, LANES)),
            const((1, LANES)),
        ],
        out_specs=[rows(w) for w, _ in out_widths_dtypes],
        out_shape=[jax.ShapeDtypeStruct((t, w), dt) for w, dt in out_widths_dtypes],
        compiler_params=pltpu.CompilerParams(dimension_semantics=("parallel",), vmem_limit_bytes=VMEM_LIMIT),
        name="inproj",
    )(h, mod, n1, w_mix, rope_tab, b64, sgn, sgw, sgb, gqn, gkn)


def _gdn_prep_kernel(x_ref, xp_ref, xn_ref, ab_ref, cw_ref, alog_ref, dtb_ref, b64_ref, tril_ref, triu_ref,
                     q_ref, k_ref, v_ref, gb_ref, *, ctx_blocks, ctx_bps, lat_bps):
    r = x_ref.shape[0]
    i = pl.program_id(0)
    is_ctx = i < ctx_blocks
    pos = jnp.where(is_ctx, i % ctx_bps, (i - ctx_blocks) % lat_bps)
    bps = jnp.where(is_ctx, ctx_bps, lat_bps)
    prev = jnp.where(pos > 0, xp_ref[...], 0.0)
    nxt = jnp.where(pos < bps - 1, xn_ref[...], 0.0)
    win = jnp.concatenate([prev, x_ref[...], nxt], axis=0)
    n = r + 2 * HALO
    y = None
    for j in range(CONV_K):
        s = (CONV_K // 2 - j) % n
        shifted = win if s == 0 else pltpu.roll(win, s, 0)
        term = cw_ref[j:j + 1, :] * shifted[HALO:HALO + r]
        y = term if y is None else y + term
    y = _silu(y)

    b64 = b64_ref[...]
    for part, ref, scale in ((0, q_ref, GDN_Q_SCALE), (1, k_ref, 1.0)):
        for j in range(BRANCH_W // LANES):
            x = y[:, part * BRANCH_W + j * LANES: part * BRANCH_W + (j + 1) * LANES]
            xn = x * (lax.rsqrt(_group_sum(x * x, b64) + EPS) * scale)
            for hh in range(LANES // HEAD_W):
                ref[j * (LANES // HEAD_W) + hh] = xn[:, hh * HEAD_W:(hh + 1) * HEAD_W]
    for h in range(N_HEADS):
        v_ref[h] = y[:, 2 * BRANCH_W + h * HEAD_W: 2 * BRANCH_W + (h + 1) * HEAD_W]

    ab = ab_ref[...]
    lane = lax.broadcasted_iota(jnp.int32, (1, LANES), 1)
    lm = lane % 4
    xa = ab + dtb_ref[...]
    softplus = jnp.maximum(xa, 0.0) + jnp.log1p(jnp.exp(-jnp.abs(xa)))
    la = jnp.where((lm < 2) & (lane < 4 * N_HEADS), -jnp.exp(alog_ref[...]) * softplus, 0.0)
    g_fwd = _split_dot_left(tril_ref[...], la, 3)
    g_bwd = _split_dot_left(triu_ref[...], la, 3)
    gb = jnp.where(lm == 0, g_fwd, jnp.where(lm == 1, g_bwd, _sigmoid(ab)))
    for h in range(N_HEADS):
        gb_ref[h] = gb if h == 0 else pltpu.roll(gb, LANES - 4 * h, 1)


def _gdn_prep_call(gqkv, gab, conv_w, alog_row, dtb_row, b64, tril, triu, *, ctx_rows, ctx_len, lat_len):
    t = gqkv.shape[0]
    r = PREP_R
    n_blocks = t // r
    rh = r // HALO
    last_halo = t // HALO - 1
    kern = functools.partial(_gdn_prep_kernel, ctx_blocks=ctx_rows // r, ctx_bps=ctx_len // r, lat_bps=lat_len // r)

    def const(shape):
        return pl.BlockSpec(shape, lambda i: (0,) * len(shape))

    head_major = pl.BlockSpec((N_HEADS, r, HEAD_W), lambda i: (0, i, 0))
    return pl.pallas_call(
        kern,
        grid=(n_blocks,),
        in_specs=[
            pl.BlockSpec((r, 3 * BRANCH_W), lambda i: (i, 0)),
            pl.BlockSpec((HALO, 3 * BRANCH_W), lambda i: (jnp.maximum(i * rh - 1, 0), 0)),
            pl.BlockSpec((HALO, 3 * BRANCH_W), lambda i: (jnp.minimum((i + 1) * rh, last_halo), 0)),
            pl.BlockSpec((r, LANES), lambda i: (i, 0)),
            const((SUBLANES, 3 * BRANCH_W)),
            const((1, LANES)),
            const((1, LANES)),
            const((LANES, LANES)),
            const((r, r)),
            const((r, r)),
        ],
        out_specs=[head_major, head_major, head_major,
                   pl.BlockSpec((N_HEADS, r, LANES), lambda i: (0, i, 0))],
        out_shape=[jax.ShapeDtypeStruct((N_HEADS, t, HEAD_W), F32)] * 3
        + [jax.ShapeDtypeStruct((N_HEADS, t, LANES), F32)],
        compiler_params=pltpu.CompilerParams(dimension_semantics=("parallel",), vmem_limit_bytes=VMEM_LIMIT),
        name="gdn_prep",
    )(gqkv, gqkv, gqkv, gab, conv_w, alog_row, dtb_row, b64, tril, triu)


def _gdn_scan_kernel(*refs, n_seq, seq_len, has_s0):
    if has_s0:
        q_ref, k_ref, v_ref, gb_ref, s0_ref, o_ref, s_ref, oi_scr, qg_scr, m_scr, b_scr = refs
    else:
        q_ref, k_ref, v_ref, gb_ref, o_ref, s_ref, oi_scr, qg_scr, m_scr, b_scr = refs
        s0_ref = None
    c = GDN_CHUNK
    gr = SCAN_G * c
    n_rows = n_seq * seq_len
    nc = seq_len // c
    row_i = lax.broadcasted_iota(jnp.int32, (gr, gr), 0)
    col_i = lax.broadcasted_iota(jnp.int32, (gr, gr), 1)
    eye = jnp.where(row_i == col_i, 1.0, 0.0)
    same = (row_i // c) == (col_i // c)
    incl = (same & (col_i <= row_i), same & (col_i >= row_i))
    strict = (same & (col_i < row_i), same & (col_i > row_i))
    blk16 = (row_i // 16) == (col_i // 16)
    blk32 = ((row_i // 32) == (col_i // 32)) & ~blk16
    blk64 = same & ((row_i // 32) != (col_i // 32))

    def prep_groups(gi, carry):
        probs = []
        for j in range(PREP_PAR):
            r0 = pl.multiple_of((gi * PREP_PAR + j) * gr, gr)
            rows = pl.ds(r0, gr)
            q = q_ref[rows, :]
            k = k_ref[rows, :]
            v = v_ref[rows, :]
            gbc = gb_ref[rows, :]
            gbt = gbc.T
            kb = _bf(k)
            kk = _dot_nt(kb, kb)
            qk = _dot_nt(_bf(q), kb)
            for d in range(2):
                gcol = gbc[:, d:d + 1]
                bcol = gbc[:, 2 + d:3 + d]
                dec = jnp.where(incl[d], jnp.exp(jnp.where(incl[d], gcol - gbt[d:d + 1, :], 0.0)), 0.0)
                probs.append(dict(d=d, r0=r0, rows=rows, q=q, k=k, v=v, gcol=gcol, bcol=bcol, dec=dec,
                                  qkm=_bf(qk * dec), nmat=jnp.where(strict[d], bcol * kk * dec, 0.0)))
        ps = [jnp.where(blk16, pr["nmat"], 0.0) for pr in probs]
        ts = [eye - p for p in ps]
        for _ in range(3):
            pbs = [_bf(p) for p in ps]
            ps = [_dot(pb, pb) for pb in pbs]
            ts = [t + _dot(_bf(t), _bf(p)) for t, p in zip(ts, ps)]
        for level in (blk32, blk64):
            tbs = [_bf(t) for t in ts]
            xs = [_dot(tb, _bf(jnp.where(level, pr["nmat"], 0.0))) for tb, pr in zip(tbs, probs)]
            ts = [t - _dot(_bf(x), tb) for t, x, tb in zip(ts, xs, tbs)]
        ainvs = [_bf(t) for t in ts]
        egs = [jnp.exp(pr["gcol"]) for pr in probs]
        ubs = [_bf(_dot(ai, _bf(pr["bcol"] * pr["v"]))) for ai, pr in zip(ainvs, probs)]
        wkbs = [_bf(_dot(ai, _bf((pr["bcol"] * eg) * pr["k"]))) for ai, pr, eg in zip(ainvs, probs, egs)]
        for pr, eg, ub, wkb in zip(probs, egs, ubs, wkbs):
            d, r0, rows, gcol = pr["d"], pr["r0"], pr["rows"], pr["gcol"]
            oi_scr[d, rows, :] = _dot(pr["qkm"], ub)
            qg_scr[d, rows, :] = _bf(pr["q"] * eg - _dot(pr["qkm"], wkb))
            glast = jnp.concatenate(
                [jnp.broadcast_to(gcol[j * c + (c - 1 if d == 0 else 0):j * c + (c if d == 0 else 1), :], (c, 1))
                 for j in range(SCAN_G)], axis=0)
            kdb = _bf(pr["k"] * jnp.exp(glast - gcol))
            for j in range(SCAN_G):
                cs = slice(j * c, (j + 1) * c)
                crow = pl.ds(r0 + j * c, c)
                m_scr[d, crow, :] = _bf(_dot_tn(kdb[cs], wkb[cs]))
                b_scr[d, crow, :] = _dot_tn(kdb[cs], ub[cs])
        return carry

    lax.fori_loop(0, n_rows // (gr * PREP_PAR), prep_groups, 0)

    def scan_step(s, d, ci, state):
        r0 = s * seq_len + ci * c
        if not isinstance(r0, int):
            r0 = pl.multiple_of(r0, c)
        rows = pl.ds(r0, c)
        sb = _bf(state)
        oi_scr[d, rows, :] = oi_scr[d, rows, :] + _dot(qg_scr[d, rows, :], sb)
        last = r0 + (c - 1) if d == 0 else r0
        egl = jnp.exp(gb_ref[pl.ds(last, 1), d:d + 1])
        return egl * state - _dot(m_scr[d, rows, :], sb) + b_scr[d, rows, :]

    def init_state(s, d):
        return s0_ref[s, d] if has_s0 else jnp.zeros((c, c), F32)

    chains = [(s, d) for s in range(n_seq) for d in range(2)]
    states = [init_state(s, d) for s, d in chains]
    if nc <= 8:
        for t in range(nc):
            states = [scan_step(s, d, t if d == 0 else nc - 1 - t, st) for (s, d), st in zip(chains, states)]
    else:
        def body(t, sts):
            return tuple(scan_step(s, d, t if d == 0 else nc - 1 - t, st) for (s, d), st in zip(chains, sts))
        states = lax.fori_loop(0, nc, body, tuple(states))
    for (s, d), st in zip(chains, states):
        s_ref[s, d] = st
    o_ref[...] = oi_scr[0] + oi_scr[1]


def _gdn_scan_call(q, k, v, gb, s0, *, row0, n_seq_total, seq_len, seqs_per_step):
    n_rows = seqs_per_step * seq_len
    blk0 = row0 // n_rows
    has_s0 = s0 is not None
    kern = functools.partial(_gdn_scan_kernel, n_seq=seqs_per_step, seq_len=seq_len, has_s0=has_s0)
    hm = pl.BlockSpec((None, n_rows, HEAD_W), lambda s, h: (h, blk0 + s, 0))
    st_spec = pl.BlockSpec((seqs_per_step, 2, None, HEAD_W, HEAD_W), lambda s, h: (s, 0, h, 0, 0))
    in_specs = [hm, hm, hm, pl.BlockSpec((None, n_rows, LANES), lambda s, h: (h, blk0 + s, 0))]
    args = [q, k, v, gb]
    if has_s0:
        in_specs.append(st_spec)
        args.append(s0)
    return pl.pallas_call(
        kern,
        grid=(n_seq_total // seqs_per_step, N_HEADS),
        in_specs=in_specs,
        out_specs=[pl.BlockSpec((None, n_rows, HEAD_W), lambda s, h: (h, s, 0)), st_spec],
        out_shape=[jax.ShapeDtypeStruct((N_HEADS, n_seq_total * seq_len, HEAD_W), F32),
                   jax.ShapeDtypeStruct((n_seq_total, 2, N_HEADS, HEAD_W, HEAD_W), F32)],
        scratch_shapes=[
            pltpu.VMEM((2, n_rows, HEAD_W), F32),
            pltpu.VMEM((2, n_rows, HEAD_W), BF16),
            pltpu.VMEM((2, n_rows, HEAD_W), BF16),
            pltpu.VMEM((2, n_rows, HEAD_W), F32),
        ],
        compiler_params=pltpu.CompilerParams(dimension_semantics=("parallel", "parallel"),
                                             vmem_limit_bytes=VMEM_LIMIT),
        name="gdn_scan",
    )(*args)


def _softmax_rows(scores):
    mx = None
    for s in scores:
        m = jnp.max(s, axis=-1, keepdims=True)
        mx = m if mx is None else jnp.maximum(mx, m)
    ps = [jnp.exp2(s - mx) for s in scores]
    tot = None
    for p in ps:
        l = jnp.sum(p, axis=-1, keepdims=True)
        tot = l if tot is None else tot + l
    return ps, tot


def _attn_kernel(*refs, has_cache, lam_init):
    if has_cache:
        (dq_ref, dk_ref, dv_ref, gq_ref, gk_ref, gv_ref, cdk_ref, cdv_ref, cgk_ref, cgv_ref,
         lp_ref, dn_ref, b64_ref, ob_ref, od_ref) = refs
        dks, dvs = [cdk_ref[...], dk_ref[...]], [cdv_ref[...], dv_ref[...]]
        gks, gvs = [cgk_ref[...], gk_ref[...]], [cgv_ref[...], gv_ref[...]]
    else:
        dq_ref, dk_ref, dv_ref, gq_ref, gk_ref, gv_ref, lp_ref, dn_ref, b64_ref, ob_ref, od_ref = refs
        dks, dvs = [dk_ref[...]], [dv_ref[...]]
        gks, gvs = [gk_ref[...]], [gv_ref[...]]

    lp = lp_ref[...]
    lam = (jnp.exp(jnp.sum(lp[0:1] * lp[1:2], axis=-1, keepdims=True))
           - jnp.exp(jnp.sum(lp[2:3] * lp[3:4], axis=-1, keepdims=True)) + lam_init)

    lane = lax.broadcasted_iota(jnp.int32, (1, BRANCH_W), 1)
    dq = dq_ref[...]
    o = None
    for h in range(N_HEADS):
        pn = []
        for m in range(2):
            lo = h * HEAD_W + m * DIFF_DQK
            qm = dq * jnp.where((lane >= lo) & (lane < lo + DIFF_DQK), 1.0, 0.0).astype(BF16)
            pn.append(_softmax_rows([_dot_nt(qm, ks) for ks in dks]))
        (p1s, l1), (p2s, l2) = pn
        inv1 = 1.0 / l1
        coef = lam * l1 / l2
        oh = None
        for p1, p2, vs in zip(p1s, p2s, dvs):
            t = _dot(_bf(p1 - coef * p2), vs)
            oh = t if oh is None else oh + t
        oh = oh * inv1
        oh = jnp.where(lane // HEAD_W == h, oh, 0.0)
        o = oh if o is None else o + oh
    b64 = b64_ref[...]
    for j in range(BRANCH_W // LANES):
        sl = slice(j * LANES, (j + 1) * LANES)
        x = o[:, sl]
        ms = _group_sum(x * x, b64) * (1.0 / HEAD_W)
        ob_ref[:, sl] = _bf(x * lax.rsqrt(ms + EPS) * dn_ref[:, sl] * (1.0 - lam_init))

    lane_h = lax.broadcasted_iota(jnp.int32, (1, LANES), 1)
    for half in range(2):
        qh = gq_ref[:, half * LANES:(half + 1) * LANES]
        acc = None
        for g in range(GQA_KV_HEADS):
            sel = lane_h // HEAD_W == g
            qm = qh * jnp.where(sel, 1.0, 0.0).astype(BF16)
            ps, tot = _softmax_rows([_dot_nt(qm, ks) for ks in gks])
            og = None
            for p, vs in zip(ps, gvs):
                t = _dot(_bf(p), vs)
                og = t if og is None else og + t
            og = og * (1.0 / tot)
            og = jnp.where(sel, og, 0.0)
            acc = og if acc is None else acc + og
        od_ref[:, half * LANES:(half + 1) * LANES] = _bf(acc)


def _attn_ctx_call(dq, dk, dv, gq, gk, gv, lp, dn, b64, *, n_seq, seq_len, lam_init):
    kern = functools.partial(_attn_kernel, has_cache=False, lam_init=lam_init)

    def rows(width):
        return pl.BlockSpec((seq_len, width), lambda s: (s, 0))

    def const(shape):
        return pl.BlockSpec(shape, lambda s: (0,) * len(shape))

    return pl.pallas_call(
        kern,
        grid=(n_seq,),
        in_specs=[rows(256), rows(256), rows(256), rows(256), rows(128), rows(128),
                  const((4, DIFF_DQK)), const((1, BRANCH_W)), const((LANES, LANES))],
        out_specs=[rows(256), rows(256)],
        out_shape=[jax.ShapeDtypeStruct((n_seq * seq_len, BRANCH_W), BF16)] * 2,
        compiler_params=pltpu.CompilerParams(dimension_semantics=("parallel",), vmem_limit_bytes=VMEM_LIMIT),
        name="attn_ctx",
    )(dq, dk, dv, gq, gk, gv, lp, dn, b64)


def _attn_lat_call(dq, dk, dv, gq, gk, gv, cdk, cdv, cgk, cgv, lp, dn, b64, *, row0, n_seq, seq_len, lam_init):
    kern = functools.partial(_attn_kernel, has_cache=True, lam_init=lam_init)
    tq = ATTN_TQ
    qpb = seq_len // tq
    q0 = row0 // tq
    s0 = row0 // seq_len
    past = cdk.shape[1]

    def qrows(width):
        return pl.BlockSpec((tq, width), lambda b, i: (q0 + b * qpb + i, 0))

    def kvrows(width):
        return pl.BlockSpec((seq_len, width), lambda b, i: (s0 + b, 0))

    def cache(width):
        return pl.BlockSpec((None, past, width), lambda b, i: (b, 0, 0))

    def const(shape):
        return pl.BlockSpec(shape, lambda b, i: (0,) * len(shape))

    out = pl.BlockSpec((tq, BRANCH_W), lambda b, i: (b * qpb + i, 0))
    return pl.pallas_call(
        kern,
        grid=(n_seq, qpb),
        in_specs=[qrows(256), kvrows(256), kvrows(256), qrows(256), kvrows(128), kvrows(128),
                  cache(256), cache(256), cache(128), cache(128),
                  const((4, DIFF_DQK)), const((1, BRANCH_W)), const((LANES, LANES))],
        out_specs=[out, out],
        out_shape=[jax.ShapeDtypeStruct((n_seq * seq_len, BRANCH_W), BF16)] * 2,
        compiler_params=pltpu.CompilerParams(dimension_semantics=("parallel", "arbitrary"),
                                             vmem_limit_bytes=VMEM_LIMIT),
        name="attn_lat",
    )(dq, dk, dv, gq, gk, gv, cdk, cdv, cgk, cgv, lp, dn, b64)


def _merge_kernel(h_ref, mod_ref, n1_ref, oa_ref, gate_ref, gn_ref, ob_ref, oc_ref, od_ref,
                  wmg_ref, wb_ref, wo_ref, b64_ref, out_ref):
    h = h_ref[...]
    xm = _bf(_norm_mod(h, n1_ref[...], mod_ref[:, D_MODEL:2 * D_MODEL], mod_ref[:, 0:D_MODEL]))
    g1 = mod_ref[:, 2 * D_MODEL:3 * D_MODEL]
    b64 = b64_ref[...]
    oa_parts = []
    for j in range(BRANCH_W // LANES):
        sl = slice(j * LANES, (j + 1) * LANES)
        x = jnp.concatenate([oa_ref[2 * j], oa_ref[2 * j + 1]], axis=-1)
        ms = _group_sum(x * x, b64) * (1.0 / HEAD_W)
        oa_parts.append(_bf(x * lax.rsqrt(ms + EPS) * gn_ref[:, sl] * _silu(gate_ref[:, sl])))
    merged = None
    for n in range(4):
        mg = _dot(xm, wmg_ref[:, n * D_MODEL:(n + 1) * D_MODEL])
        if n == 0:
            up = _dot(oa_parts[0], wb_ref[0, 0:LANES, :]) + _dot(oa_parts[1], wb_ref[0, LANES:2 * LANES, :])
        else:
            up = _dot((ob_ref, oc_ref, od_ref)[n - 1][...], wb_ref[n])
        term = _sigmoid(mg) * up
        merged = term if merged is None else merged + term
    out_ref[...] = h + g1 * _dot(_bf(merged), wo_ref[...])


def _merge_call(h, mod, n1, oa, gate, gn, ob, oc, od, wmg, wb, wo, b64, *, tiles_ctx, tiles_per_lat):
    t = h.shape[0]

    def mod_row(i):
        return jnp.where(i < tiles_ctx, 0, 1 + (i - tiles_ctx) // tiles_per_lat)

    def const(shape):
        return pl.BlockSpec(shape, lambda i: (0,) * len(shape))

    def rows(width):
        return pl.BlockSpec((TM, width), lambda i: (i, 0))

    return pl.pallas_call(
        _merge_kernel,
        grid=(t // TM,),
        in_specs=[
            rows(D_MODEL),
            pl.BlockSpec((None, 1, 6 * D_MODEL), lambda i: (mod_row(i), 0, 0)),
            const((1, D_MODEL)),
            pl.BlockSpec((N_HEADS, TM, HEAD_W), lambda i: (0, i, 0)),
            rows(BRANCH_W),
            const((1, BRANCH_W)),
            rows(BRANCH_W), rows(BRANCH_W), rows(BRANCH_W),
            const((D_MODEL, 4 * D_MODEL)),
            const((4, BRANCH_W, D_MODEL)),
            const((D_MODEL, D_MODEL)),
            const((LANES, LANES)),
        ],
        out_specs=rows(D_MODEL),
        out_shape=jax.ShapeDtypeStruct((t, D_MODEL), F32),
        compiler_params=pltpu.CompilerParams(dimension_semantics=("parallel",), vmem_limit_bytes=VMEM_LIMIT),
        name="merge",
    )(h, mod, n1, oa, gate, gn, ob, oc, od, wmg, wb, wo, b64)


ROUTER_ROWS = 32
ROUTER_E0 = 8


def _moe_kernel(h_ref, mod_ref, n2_ref, wr_ref, br_ref, w1_ref, w3_ref, w2_ref, out_ref, xm_scr, gates_scr, acc_scr):
    e = pl.program_id(1)
    tm = h_ref.shape[0]

    @pl.when(e == 0)
    def _():
        xm = _norm_mod(h_ref[...], n2_ref[...], mod_ref[:, 4 * D_MODEL:5 * D_MODEL],
                       mod_ref[:, 3 * D_MODEL:4 * D_MODEL])
        xm_scr[...] = _bf(xm)
        lt = lax.dot_general(wr_ref[...], xm, (((1,), (1,)), ((), ())), precision=lax.Precision.HIGHEST,
                             preferred_element_type=F32) + br_ref[...]
        gl = lt[0:N_GROUPS]
        gidx = lax.broadcasted_iota(jnp.int32, (N_GROUPS, 1), 0)
        gmax = jnp.max(gl, axis=0, keepdims=True)
        gsel = jnp.min(jnp.where(gl == gmax, gidx, N_GROUPS), axis=0, keepdims=True)
        gprob = 1.0 / jnp.sum(jnp.exp(gl - gmax), axis=0, keepdims=True)
        el = lt[ROUTER_E0:ROUTER_E0 + N_EXPERTS]
        eidx = lax.broadcasted_iota(jnp.int32, (N_EXPERTS, 1), 0)
        neg = -jnp.inf
        cand = jnp.where(eidx // EXPERTS_PER_GROUP == gsel, el, neg)
        v1 = jnp.max(cand, axis=0, keepdims=True)
        i1 = jnp.min(jnp.where(cand == v1, eidx, N_EXPERTS), axis=0, keepdims=True)
        cand2 = jnp.where(eidx == i1, neg, cand)
        v2 = jnp.max(cand2, axis=0, keepdims=True)
        i2 = jnp.min(jnp.where(cand2 == v2, eidx, N_EXPERTS), axis=0, keepdims=True)
        ex = jnp.exp(v2 - v1)
        wa = 1.0 / (1.0 + ex)
        gates_t = (jnp.where(eidx == i1, wa, 0.0) + jnp.where(eidx == i2, ex * wa, 0.0)) * gprob
        full = jnp.concatenate([gates_t, jnp.zeros((LANES - N_EXPERTS, tm), F32)], axis=0)
        gates_scr[...] = full.T
        acc_scr[...] = jnp.zeros_like(acc_scr)

    xm = xm_scr[...]
    h1 = _dot(xm, w1_ref[...])
    h3 = _dot(xm, w3_ref[...])
    lane = lax.broadcasted_iota(jnp.int32, (1, LANES), 1)
    gate = jnp.sum(jnp.where(lane == e, gates_scr[...], 0.0), axis=-1, keepdims=True)
    hd = _silu(h1) * h3 * gate
    acc_scr[...] += _dot(_bf(hd), w2_ref[...])

    @pl.when(e == N_EXPERTS - 1)
    def _():
        out_ref[...] = h_ref[...] + mod_ref[:, 5 * D_MODEL:6 * D_MODEL] * acc_scr[...]


def _moe_call(h, mod, n2, wr, br, w1, w3, w2, *, tiles_ctx, tiles_per_lat):
    t = h.shape[0]

    def mod_row(i):
        return jnp.where(i < tiles_ctx, 0, 1 + (i - tiles_ctx) // tiles_per_lat)

    return pl.pallas_call(
        _moe_kernel,
        grid=(t // TM, N_EXPERTS),
        in_specs=[
            pl.BlockSpec((TM, D_MODEL), lambda i, e: (i, 0)),
            pl.BlockSpec((None, 1, 6 * D_MODEL), lambda i, e: (mod_row(i), 0, 0)),
            pl.BlockSpec((1, D_MODEL), lambda i, e: (0, 0)),
            pl.BlockSpec((ROUTER_ROWS, D_MODEL), lambda i, e: (0, 0)),
            pl.BlockSpec((ROUTER_ROWS, 1), lambda i, e: (0, 0)),
            pl.BlockSpec((None, D_MODEL, D_EXPERT), lambda i, e: (e, 0, 0)),
            pl.BlockSpec((None, D_MODEL, D_EXPERT), lambda i, e: (e, 0, 0)),
            pl.BlockSpec((None, D_EXPERT, D_MODEL), lambda i, e: (e, 0, 0)),
        ],
        out_specs=pl.BlockSpec((TM, D_MODEL), lambda i, e: (i, 0)),
        out_shape=jax.ShapeDtypeStruct((t, D_MODEL), F32),
        scratch_shapes=[pltpu.VMEM((TM, D_MODEL), BF16), pltpu.VMEM((TM, LANES), F32),
                        pltpu.VMEM((TM, D_MODEL), F32)],
        compiler_params=pltpu.CompilerParams(dimension_semantics=("parallel", "arbitrary"),
                                             vmem_limit_bytes=VMEM_LIMIT),
        name="moe",
    )(h, mod, n2, wr, br, w1, w3, w2)


def _final_norm_kernel(h_ref, g_ref, o_ref):
    h = h_ref[...]
    o_ref[...] = h * lax.rsqrt(jnp.mean(h * h, axis=-1, keepdims=True) + EPS) * g_ref[...]


def _final_norm_call(h, g):
    t = h.shape[0]
    return pl.pallas_call(
        _final_norm_kernel,
        grid=(t // TM,),
        in_specs=[pl.BlockSpec((TM, D_MODEL), lambda i: (i, 0)), pl.BlockSpec((1, D_MODEL), lambda i: (0, 0))],
        out_specs=pl.BlockSpec((TM, D_MODEL), lambda i: (i, 0)),
        out_shape=jax.ShapeDtypeStruct((t, D_MODEL), F32),
        compiler_params=pltpu.CompilerParams(dimension_semantics=("parallel",)),
        name="final_norm",
    )(h, g)


def _mix_column_perm():
    cols = list(range(_O_QKV, _O_QKV + 768)) + list(range(_O_GATE, _O_GATE + 256))
    ab = []
    for h in range(N_HEADS):
        ab += [_O_A + h, _O_A + N_HEADS + h, _O_B + h, _O_B + N_HEADS + h]
    cols += ab + [-1] * (LANES - len(ab))
    cols += list(range(_O_DQ, _O_DQ + 768))
    cols += list(range(_O_SU, _O_SU + 512))
    for h in (0, 2, 1, 3):
        cols += list(range(_O_GQ + h * HEAD_W, _O_GQ + (h + 1) * HEAD_W))
    cols += list(range(_O_GK, _O_GK + 256))
    assert len(cols) == C_END
    return cols


def _rope_tables(lat_len):
    pos = jnp.arange(lat_len, dtype=jnp.int32)
    row = (pos // GRID_W).astype(F32)[:, None]
    col = (pos % GRID_W).astype(F32)[:, None]
    lane = jnp.arange(LANES, dtype=jnp.int32)[None, :]
    tabs = []
    for head_w in (DIFF_DQK, HEAD_W):
        half = head_w // 2
        nf = half // 2
        p = lane % head_w
        f = (p % half) % nf
        inv = ROPE_THETA ** (-f.astype(F32) / nf)
        ang = jnp.where(p < half, row, col) * inv
        sign = jnp.where((p % half) < nf, -1.0, 1.0)
        tabs += [jnp.cos(ang), jnp.sin(ang) * sign]
    tab = jnp.stack(tabs)
    ident = jnp.stack([jnp.ones((TM, LANES), F32), jnp.zeros((TM, LANES), F32)] * 2)
    return jnp.concatenate([ident, tab], axis=1)


def _chunk_tri(n, chunk, upper):
    i = jnp.arange(n)[:, None]
    j = jnp.arange(n)[None, :]
    same = (i // chunk) == (j // chunk)
    tri = (j >= i) if upper else (j <= i)
    return jnp.where(same & tri, 1.0, 0.0).astype(BF16)


def kernel(x_prompt, x_sample, state_gdn, cache_diff_k, cache_diff_v, cache_gqa_k, cache_gqa_v, c, c_ctx,
           ada_w, ada_b, norm1, norm2, w_in, gdn_conv, gdn_a_log, gdn_dt_bias, gdn_norm, diff_lambda, diff_norm,
           sgu_norm, sgu_ws, sgu_b, gqa_qnorm, gqa_knorm, w_branch, w_out, moe_wg, moe_bg, moe_we, moe_be,
           moe_w1, moe_w3, moe_w2, final_norm):
    nb, seq, d = x_prompt.shape
    nl, lat_len, _ = x_sample.shape
    depth = ada_w.shape[0]
    past = cache_diff_k.shape[2]
    ctx_rows = nb * seq
    lat_rows = nl * lat_len
    assert d == D_MODEL and seq % PREP_R == 0 and lat_len % TM == 0 and ctx_rows % lat_len == 0
    assert ctx_rows % TM == 0 and 1 + nl <= SUBLANES and lat_len % GRID_W == 0
    tiles_ctx = ctx_rows // TM
    tiles_per_lat = lat_len // TM
    tile_kw = dict(tiles_ctx=tiles_ctx, tiles_per_lat=tiles_per_lat)

    lane = jnp.arange(LANES)
    b64 = jnp.where((lane[:, None] // HEAD_W) == (lane[None, :] // HEAD_W), 1.0, 0.0).astype(BF16)
    tril = _chunk_tri(PREP_R, GDN_CHUNK, upper=False)
    triu = _chunk_tri(PREP_R, GDN_CHUNK, upper=True)
    rope_tab = _rope_tables(lat_len)

    perm = jnp.asarray(_mix_column_perm(), dtype=jnp.int32)
    w_mix = _bf(jnp.where(perm[None, None, :] >= 0, jnp.take(w_in, jnp.maximum(perm, 0), axis=2), 0.0))
    w_mg = _bf(w_in[:, :, _O_MG:])
    head_perm = jnp.asarray([r for h in (0, 2, 1, 3) for r in range(h * HEAD_W, (h + 1) * HEAD_W)], dtype=jnp.int32)
    wb = _bf(jnp.concatenate([w_branch[:, :3], jnp.take(w_branch[:, 3], head_perm, axis=1)[:, None]], axis=1))
    wo = _bf(w_out)
    w1, w3, w2 = _bf(moe_w1), _bf(moe_w3), _bf(moe_w2)
    wr = jnp.zeros((depth, ROUTER_ROWS, d), F32)
    wr = wr.at[:, 0:N_GROUPS].set(jnp.swapaxes(moe_wg, 1, 2))
    wr = wr.at[:, ROUTER_E0:ROUTER_E0 + N_EXPERTS].set(jnp.swapaxes(moe_we, 1, 2))
    br = jnp.zeros((depth, ROUTER_ROWS, 1), F32)
    br = br.at[:, 0:N_GROUPS, 0].set(moe_bg).at[:, ROUTER_E0:ROUTER_E0 + N_EXPERTS, 0].set(moe_be)
    conv_w = jnp.zeros((depth, SUBLANES, 3 * BRANCH_W), F32).at[:, :CONV_K].set(gdn_conv)
    ab_lane = jnp.zeros((depth, 1, LANES), F32)
    alog_row, dtb_row = ab_lane, ab_lane
    for h in range(N_HEADS):
        for dd in range(2):
            alog_row = alog_row.at[:, 0, 4 * h + dd].set(gdn_a_log[:, dd, h])
            dtb_row = dtb_row.at[:, 0, 4 * h + dd].set(gdn_dt_bias[:, dd, h])
    sgw = _bf(sgu_ws)
    sgb = jnp.repeat(jnp.swapaxes(sgu_b, 1, 2), HEAD_W, axis=2)
    gdn_gn = jnp.tile(gdn_norm, (1, N_HEADS))[:, None, :]
    diff_gn = jnp.tile(diff_norm, (1, N_HEADS))[:, None, :]
    gqn = jnp.tile(gqa_qnorm, (1, LANES // HEAD_W))[:, None, :]
    gkn = jnp.tile(gqa_knorm, (1, LANES // HEAD_W))[:, None, :]

    cond = jnp.zeros((SUBLANES, d), F32).at[0].set(c_ctx).at[1:1 + nl].set(c)
    mods = _ada_call(cond, ada_w, ada_b)

    h = jnp.concatenate([x_prompt.reshape(ctx_rows, d), x_sample.reshape(lat_rows, d)], axis=0)
    new_s, new_dk, new_dv, new_gk, new_gv = [], [], [], [], []
    for l in range(depth):
        lam_init = 0.8 - 0.6 * math.exp(-0.3 * l)
        mod = mods[l][:, None, :]
        n1 = norm1[l][None, :]
        (gqkv, ggate, gab, dq, dk32, dv32, dk16, dv16, oc, gq, gk32, gv32, gk16, gv16) = _inproj_call(
            h, mod, n1, w_mix[l], rope_tab, b64, sgu_norm[l][None, :], sgw[l], sgb[l], gqn[l], gkn[l], **tile_kw)

        qh, kh, vh, gb = _gdn_prep_call(gqkv, gab, conv_w[l], alog_row[l], dtb_row[l], b64, tril, triu,
                                        ctx_rows=ctx_rows, ctx_len=seq, lat_len=lat_len)
        oa_ctx, s_ctx = _gdn_scan_call(qh, kh, vh, gb, None, row0=0, n_seq_total=nb, seq_len=seq,
                                       seqs_per_step=8 if nb % 8 == 0 else 1)
        oa_lat, _ = _gdn_scan_call(qh, kh, vh, gb, state_gdn[:, l], row0=ctx_rows, n_seq_total=nl,
                                   seq_len=lat_len, seqs_per_step=1)
        oa = jnp.concatenate([oa_ctx, oa_lat], axis=1)

        lp = diff_lambda[l]
        ob_ctx, od_ctx = _attn_ctx_call(dq, dk16, dv16, gq, gk16, gv16, lp, diff_gn[l], b64,
                                        n_seq=nb, seq_len=seq, lam_init=lam_init)
        ob_lat, od_lat = _attn_lat_call(
            dq, dk16, dv16, gq, gk16, gv16,
            _bf(cache_diff_k[:, l].reshape(nl, past, BRANCH_W)), _bf(cache_diff_v[:, l].reshape(nl, past, BRANCH_W)),
            _bf(cache_gqa_k[:, l].reshape(nl, past, LANES)), _bf(cache_gqa_v[:, l].reshape(nl, past, LANES)),
            lp, diff_gn[l], b64, row0=ctx_rows, n_seq=nl, seq_len=lat_len, lam_init=lam_init)
        ob = jnp.concatenate([ob_ctx, ob_lat], axis=0)
        od = jnp.concatenate([od_ctx, od_lat], axis=0)

        h = _merge_call(h, mod, n1, oa, ggate, gdn_gn[l], ob, oc, od, w_mg[l], wb[l], wo[l], b64, **tile_kw)
        h = _moe_call(h, mod, norm2[l][None, :], wr[l], br[l], w1[l], w3[l], w2[l], **tile_kw)

        new_s.append(s_ctx)
        new_dk.append(dk32[:ctx_rows].reshape(nb, seq, N_HEADS, HEAD_W))
        new_dv.append(dv32[:ctx_rows].reshape(nb, seq, N_HEADS, HEAD_W))
        new_gk.append(gk32[:ctx_rows].reshape(nb, seq, GQA_KV_HEADS, HEAD_W))
        new_gv.append(gv32[:ctx_rows].reshape(nb, seq, GQA_KV_HEADS, HEAD_W))

    y = _final_norm_call(h, final_norm[None, :])
    return (y[:ctx_rows].reshape(nb, seq, d), y[ctx_rows:].reshape(nl, lat_len, d),
            jnp.stack(new_s, axis=1), jnp.stack(new_dk, axis=1), jnp.stack(new_dv, axis=1),
            jnp.stack(new_gk, axis=1), jnp.stack(new_gv, axis=1))
```

```python
import functools
import math

import jax
import jax.numpy as jnp
from jax import lax
from jax.experimental import pallas as pl
from jax.experimental.pallas import tpu as pltpu

F32 = jnp.float32
BF16 = jnp.bfloat16

GRID_W = 64
ROPE_THETA = 10000.0
EPS = 1e-6
D_MODEL = 1024
BRANCH_W = 256
HEAD_W = 64
DIFF_DQK = 32
N_HEADS = 4
GQA_KV_HEADS = 2
GDN_CHUNK = 64
CONV_K = 5
SGU_CHUNK = 128
SGU_GROUPS = 4
N_GROUPS = 4
EXPERTS_PER_GROUP = 4
N_EXPERTS = 16
D_EXPERT = 256
LOG2E = 1.4426950408889634
DIFF_SCALE = DIFF_DQK ** -0.5 * LOG2E
GQA_SCALE = HEAD_W ** -0.5 * LOG2E
GDN_Q_SCALE = HEAD_W ** -0.5

LANES = 128
SUBLANES = 8
VMEM_LIMIT = 56 * 1024 * 1024

_IN_WIDTHS = (768, 256, 8, 8, 256, 256, 256, 256, 256, 256, 128, 128, 4096)
_IN_OFFS = tuple(sum(_IN_WIDTHS[:i]) for i in range(len(_IN_WIDTHS)))
(_O_QKV, _O_GATE, _O_A, _O_B, _O_DQ, _O_DK, _O_DV, _O_SU, _O_SV, _O_GQ, _O_GK, _O_GV, _O_MG) = _IN_OFFS
MIX_COLS = _O_MG

C_QKV, C_GATE, C_AB, C_DQ, C_DK, C_DV, C_SU, C_SV, C_GQ, C_GK, C_GV, C_END = (
    0, 768, 1024, 1152, 1408, 1664, 1920, 2176, 2432, 2688, 2816, 2944)

TM = 512
TM_MOE = 1024
PREP_R = 256
HALO = SUBLANES
SCAN_G = 4
PREP_PAR = 2
ATTN_TQ = 256


def _bf(x):
    return x.astype(BF16)


def _dot(a, b):
    return jnp.dot(a, b, preferred_element_type=F32)


def _dot_nt(a, b):
    return lax.dot_general(a, b, (((1,), (1,)), ((), ())), preferred_element_type=F32)


def _dot_tn(a, b):
    return lax.dot_general(a, b, (((0,), (0,)), ((), ())), preferred_element_type=F32)


def _split_dot(x, m, terms):
    acc = None
    r = x
    for t in range(terms):
        p = _bf(r)
        d = _dot(p, m)
        acc = d if acc is None else acc + d
        if t + 1 < terms:
            r = r - p.astype(F32)
    return acc


def _split_dot_left(m, x, terms):
    acc = None
    r = x
    for t in range(terms):
        p = _bf(r)
        d = _dot(m, p)
        acc = d if acc is None else acc + d
        if t + 1 < terms:
            r = r - p.astype(F32)
    return acc


def _sigmoid(x):
    return 1.0 / (1.0 + jnp.exp(-x))


def _silu(x):
    return x * _sigmoid(x)


def _norm_mod(h, gamma, scale, shift):
    ms = jnp.mean(h * h, axis=-1, keepdims=True)
    return (h * lax.rsqrt(ms + EPS) * gamma) * (1.0 + scale) + shift


def _group_sum(sq, b64):
    return _split_dot(sq, b64, 2)


def _rope(x, cos, sin_signed, nf):
    lane = lax.broadcasted_iota(jnp.int32, (1, LANES), 1)
    first = (lane % (2 * nf)) < nf
    partner = jnp.where(first, pltpu.roll(x, LANES - nf, 1), pltpu.roll(x, nf, 1))
    return x * cos + partner * sin_signed


def _layer_spec(shape, layer):
    return pl.BlockSpec((None,) + tuple(shape), lambda *_: (layer,) + (0,) * len(shape))


def _mod_spec(layer, mod_row):
    return pl.BlockSpec((None, None, 1, 6 * D_MODEL), lambda i, *_: (layer, mod_row(i), 0, 0))


def _ada_kernel(c_ref, w_ref, b_ref, o_ref):
    x = _silu(c_ref[...])
    o_ref[...] = jnp.dot(x, w_ref[...], precision=lax.Precision.HIGHEST,
                         preferred_element_type=F32) + b_ref[...]


def _ada_call(cond, ada_w, ada_b):
    n_layers, d, n_out = ada_w.shape
    tn = 1536
    return pl.pallas_call(
        _ada_kernel,
        grid=(n_layers, n_out // tn),
        in_specs=[
            pl.BlockSpec((SUBLANES, d), lambda l, j: (0, 0)),
            pl.BlockSpec((None, d, tn), lambda l, j: (l, 0, j)),
            pl.BlockSpec((None, 1, tn), lambda l, j: (l, 0, j)),
        ],
        out_specs=pl.BlockSpec((None, SUBLANES, tn), lambda l, j: (l, 0, j)),
        out_shape=jax.ShapeDtypeStruct((n_layers, SUBLANES, n_out), F32),
        compiler_params=pltpu.CompilerParams(vmem_limit_bytes=VMEM_LIMIT),
        name="ada_mod",
    )(cond, ada_w, ada_b.reshape(n_layers, 1, n_out))


def _inproj_kernel(h_ref, mod_ref, n1_ref, w_ref, rope_ref, b64_ref, sgn_ref, sgw_ref, sgb_ref, gqn_ref, gkn_ref,
                   gqkv_ref, ggate_ref, gab_ref, dq_ref, dk32_ref, dv32_ref, dk16_ref, dv16_ref, oc_ref,
                   gq_ref, gk32_ref, gv32_ref, gk16_ref, gv16_ref):
    tm = h_ref.shape[0]
    xm = _bf(_norm_mod(h_ref[...], n1_ref[...], mod_ref[:, D_MODEL:2 * D_MODEL], mod_ref[:, 0:D_MODEL]))

    def proj(a, b):
        return _dot(xm, w_ref[:, a:b])

    gqkv_ref[...] = proj(C_QKV, C_GATE)
    ggate_ref[...] = proj(C_GATE, C_AB)
    gab_ref[...] = proj(C_AB, C_DQ)

    cos_d, sin_d, cos_g, sin_g = rope_ref[0], rope_ref[1], rope_ref[2], rope_ref[3]
    b64 = b64_ref[...]

    for j in range(BRANCH_W // LANES):
        sl = slice(j * LANES, (j + 1) * LANES)
        q = proj(C_DQ + j * LANES, C_DQ + (j + 1) * LANES)
        dq_ref[:, sl] = _bf(_rope(q, cos_d, sin_d, DIFF_DQK // 4) * DIFF_SCALE)
        k = _rope(proj(C_DK + j * LANES, C_DK + (j + 1) * LANES), cos_d, sin_d, DIFF_DQK // 4)
        dk32_ref[:, sl] = k
        dk16_ref[:, sl] = _bf(k)
    v = proj(C_DV, C_SU)
    dv32_ref[...] = v
    dv16_ref[...] = _bf(v)

    u = proj(C_SU, C_SV)
    sv = proj(C_SV, C_GQ)
    vn = _bf(sv * lax.rsqrt(jnp.mean(sv * sv, axis=-1, keepdims=True) + EPS) * sgn_ref[...])
    group = lax.broadcasted_iota(jnp.int32, (1, BRANCH_W), 1) // HEAD_W
    for r in range(tm // SGU_CHUNK):
        rows = slice(r * SGU_CHUNK, (r + 1) * SGU_CHUNK)
        vc = vn[rows]
        z = sgb_ref[...]
        for g in range(SGU_GROUPS):
            z = z + jnp.where(group == g, _dot(sgw_ref[g], vc), 0.0)
        oc_ref[rows, :] = _bf(u[rows] * z)

    for j in range(BRANCH_W // LANES):
        sl = slice(j * LANES, (j + 1) * LANES)
        q = proj(C_GQ + j * LANES, C_GQ + (j + 1) * LANES)
        qn = q * lax.rsqrt(_group_sum(q * q, b64) * (1.0 / HEAD_W) + EPS) * gqn_ref[...]
        gq_ref[:, sl] = _bf(_rope(qn, cos_g, sin_g, HEAD_W // 4) * GQA_SCALE)
    k = proj(C_GK, C_GV)
    kn = k * lax.rsqrt(_group_sum(k * k, b64) * (1.0 / HEAD_W) + EPS) * gkn_ref[...]
    kr = _rope(kn, cos_g, sin_g, HEAD_W // 4)
    gk32_ref[...] = kr
    gk16_ref[...] = _bf(kr)
    v = proj(C_GV, C_END)
    gv32_ref[...] = v
    gv16_ref[...] = _bf(v)


def _inproj_call(h, mods, n1, w_mix, rope_tab, b64, sgn, sgw, sgb, gqn, gkn, *, layer, tiles_ctx, tiles_per_lat):
    t = h.shape[0]
    n_tiles = t // TM

    def mod_row(i):
        return jnp.where(i < tiles_ctx, 0, 1 + (i - tiles_ctx) // tiles_per_lat)

    def rope_blk(i):
        return jnp.where(i < tiles_ctx, 0, 1 + (i - tiles_ctx) % tiles_per_lat)

    def const(shape):
        return pl.BlockSpec(shape, lambda i: (0,) * len(shape))

    def rows(width):
        return pl.BlockSpec((TM, width), lambda i: (i, 0))

    out_widths_dtypes = [
        (768, F32), (256, F32), (128, F32),
        (256, BF16), (256, F32), (256, F32), (256, BF16), (256, BF16),
        (256, BF16),
        (256, BF16), (128, F32), (128, F32), (128, BF16), (128, BF16),
    ]
    return pl.pallas_call(
        _inproj_kernel,
        grid=(n_tiles,),
        in_specs=[
            rows(D_MODEL),
            _mod_spec(layer, mod_row),
            _layer_spec((1, D_MODEL), layer),
            _layer_spec((D_MODEL, C_END), layer),
            pl.BlockSpec((4, TM, LANES), lambda i: (0, rope_blk(i), 0)),
            const((LANES, LANES)),
            _layer_spec((1, BRANCH_W), layer),
            _layer_spec((SGU_GROUPS, SGU_CHUNK, SGU_CHUNK), layer),
            _layer_spec((SGU_CHUNK, BRANCH_W), layer),
            _layer_spec((1, LANES), layer),
            _layer_spec((1, LANES), layer),
        ],
        out_specs=[rows(w) for w, _ in out_widths_dtypes],
        out_shape=[jax.ShapeDtypeStruct((t, w), dt) for w, dt in out_widths_dtypes],
        compiler_params=pltpu.CompilerParams(dimension_semantics=("parallel",), vmem_limit_bytes=VMEM_LIMIT),
        name="inproj",
    )(h, mods, n1, w_mix, rope_tab, b64, sgn, sgw, sgb, gqn, gkn)


def _gdn_prep_kernel(x_ref, xp_ref, xn_ref, ab_ref, cw_ref, alog_ref, dtb_ref, b64_ref, tril_ref, triu_ref,
                     q_ref, k_ref, v_ref, gb_ref, *, ctx_blocks, ctx_bps, lat_bps):
    r = x_ref.shape[0]
    i = pl.program_id(0)
    is_ctx = i < ctx_blocks
    pos = jnp.where(is_ctx, i % ctx_bps, (i - ctx_blocks) % lat_bps)
    bps = jnp.where(is_ctx, ctx_bps, lat_bps)
    prev = jnp.where(pos > 0, xp_ref[...], 0.0)
    nxt = jnp.where(pos < bps - 1, xn_ref[...], 0.0)
    win = jnp.concatenate([prev, x_ref[...], nxt], axis=0)
    n = r + 2 * HALO
    y = None
    for j in range(CONV_K):
        s = (CONV_K // 2 - j) % n
        shifted = win if s == 0 else pltpu.roll(win, s, 0)
        term = cw_ref[j:j + 1, :] * shifted[HALO:HALO + r]
        y = term if y is None else y + term
    y = _silu(y)

    b64 = b64_ref[...]
    for part, ref, scale in ((0, q_ref, GDN_Q_SCALE), (1, k_ref, 1.0)):
        for j in range(BRANCH_W // LANES):
            x = y[:, part * BRANCH_W + j * LANES: part * BRANCH_W + (j + 1) * LANES]
            xn = x * (lax.rsqrt(_group_sum(x * x, b64) + EPS) * scale)
            for hh in range(LANES // HEAD_W):
                ref[j * (LANES // HEAD_W) + hh] = xn[:, hh * HEAD_W:(hh + 1) * HEAD_W]
    for h in range(N_HEADS):
        v_ref[h] = y[:, 2 * BRANCH_W + h * HEAD_W: 2 * BRANCH_W + (h + 1) * HEAD_W]

    ab = ab_ref[...]
    lane = lax.broadcasted_iota(jnp.int32, (1, LANES), 1)
    lm = lane % 4
    xa = ab + dtb_ref[...]
    softplus = jnp.maximum(xa, 0.0) + jnp.log1p(jnp.exp(-jnp.abs(xa)))
    la = jnp.where((lm < 2) & (lane < 4 * N_HEADS), -jnp.exp(alog_ref[...]) * softplus, 0.0)
    g_fwd = _split_dot_left(tril_ref[...], la, 3)
    g_bwd = _split_dot_left(triu_ref[...], la, 3)
    gb = jnp.where(lm == 0, g_fwd, jnp.where(lm == 1, g_bwd, _sigmoid(ab)))
    for h in range(N_HEADS):
        gb_ref[h] = gb if h == 0 else pltpu.roll(gb, LANES - 4 * h, 1)


def _gdn_prep_call(gqkv, gab, conv_w, alog_row, dtb_row, b64, tril, triu, *, layer, ctx_rows, ctx_len, lat_len):
    t = gqkv.shape[0]
    r = PREP_R
    n_blocks = t // r
    rh = r // HALO
    last_halo = t // HALO - 1
    kern = functools.partial(_gdn_prep_kernel, ctx_blocks=ctx_rows // r, ctx_bps=ctx_len // r, lat_bps=lat_len // r)

    def const(shape):
        return pl.BlockSpec(shape, lambda i: (0,) * len(shape))

    head_major = pl.BlockSpec((N_HEADS, r, HEAD_W), lambda i: (0, i, 0))
    return pl.pallas_call(
        kern,
        grid=(n_blocks,),
        in_specs=[
            pl.BlockSpec((r, 3 * BRANCH_W), lambda i: (i, 0)),
            pl.BlockSpec((HALO, 3 * BRANCH_W), lambda i: (jnp.maximum(i * rh - 1, 0), 0)),
            pl.BlockSpec((HALO, 3 * BRANCH_W), lambda i: (jnp.minimum((i + 1) * rh, last_halo), 0)),
            pl.BlockSpec((r, LANES), lambda i: (i, 0)),
            _layer_spec((SUBLANES, 3 * BRANCH_W), layer),
            _layer_spec((1, LANES), layer),
            _layer_spec((1, LANES), layer),
            const((LANES, LANES)),
            const((r, r)),
            const((r, r)),
        ],
        out_specs=[head_major, head_major, head_major,
                   pl.BlockSpec((N_HEADS, r, LANES), lambda i: (0, i, 0))],
        out_shape=[jax.ShapeDtypeStruct((N_HEADS, t, HEAD_W), F32)] * 3
        + [jax.ShapeDtypeStruct((N_HEADS, t, LANES), F32)],
        compiler_params=pltpu.CompilerParams(dimension_semantics=("parallel",), vmem_limit_bytes=VMEM_LIMIT),
        name="gdn_prep",
    )(gqkv, gqkv, gqkv, gab, conv_w, alog_row, dtb_row, b64, tril, triu)


def _gdn_scan_kernel(*refs, n_seq, seq_len, has_s0):
    if has_s0:
        q_ref, k_ref, v_ref, gb_ref, s0_ref, o_ref, s_ref, oi_scr, qg_scr, m_scr, b_scr = refs
    else:
        q_ref, k_ref, v_ref, gb_ref, o_ref, s_ref, oi_scr, qg_scr, m_scr, b_scr = refs
        s0_ref = None
    c = GDN_CHUNK
    gr = SCAN_G * c
    n_rows = n_seq * seq_len
    nc = seq_len // c
    row_i = lax.broadcasted_iota(jnp.int32, (gr, gr), 0)
    col_i = lax.broadcasted_iota(jnp.int32, (gr, gr), 1)
    eye = jnp.where(row_i == col_i, 1.0, 0.0)
    same = (row_i // c) == (col_i // c)
    incl = (same & (col_i <= row_i), same & (col_i >= row_i))
    strict = (same & (col_i < row_i), same & (col_i > row_i))
    blk16 = (row_i // 16) == (col_i // 16)
    blk32 = ((row_i // 32) == (col_i // 32)) & ~blk16
    blk64 = same & ((row_i // 32) != (col_i // 32))

    def prep_groups(gi, carry):
        probs = []
        for j in range(PREP_PAR):
            r0 = pl.multiple_of((gi * PREP_PAR + j) * gr, gr)
            rows = pl.ds(r0, gr)
            q = q_ref[rows, :]
            k = k_ref[rows, :]
            v = v_ref[rows, :]
            gbc = gb_ref[rows, :]
            gbt = gbc.T
            kb = _bf(k)
            kk = _dot_nt(kb, kb)
            qk = _dot_nt(_bf(q), kb)
            for d in range(2):
                gcol = gbc[:, d:d + 1]
                bcol = gbc[:, 2 + d:3 + d]
                dec = jnp.where(incl[d], jnp.exp(jnp.where(incl[d], gcol - gbt[d:d + 1, :], 0.0)), 0.0)
                probs.append(dict(d=d, r0=r0, rows=rows, q=q, k=k, v=v, gcol=gcol, bcol=bcol, dec=dec,
                                  qkm=_bf(qk * dec), nmat=jnp.where(strict[d], bcol * kk * dec, 0.0)))
        ps = [jnp.where(blk16, pr["nmat"], 0.0) for pr in probs]
        ts = [eye - p for p in ps]
        for _ in range(3):
            pbs = [_bf(p) for p in ps]
            ps = [_dot(pb, pb) for pb in pbs]
            ts = [t + _dot(_bf(t), _bf(p)) for t, p in zip(ts, ps)]
        for level in (blk32, blk64):
            tbs = [_bf(t) for t in ts]
            xs = [_dot(tb, _bf(jnp.where(level, pr["nmat"], 0.0))) for tb, pr in zip(tbs, probs)]
            ts = [t - _dot(_bf(x), tb) for t, x, tb in zip(ts, xs, tbs)]
        ainvs = [_bf(t) for t in ts]
        egs = [jnp.exp(pr["gcol"]) for pr in probs]
        ubs = [_bf(_dot(ai, _bf(pr["bcol"] * pr["v"]))) for ai, pr in zip(ainvs, probs)]
        wkbs = [_bf(_dot(ai, _bf((pr["bcol"] * eg) * pr["k"]))) for ai, pr, eg in zip(ainvs, probs, egs)]
        for pr, eg, ub, wkb in zip(probs, egs, ubs, wkbs):
            d, r0, rows, gcol = pr["d"], pr["r0"], pr["rows"], pr["gcol"]
            oi_scr[d, rows, :] = _dot(pr["qkm"], ub)
            qg_scr[d, rows, :] = _bf(pr["q"] * eg - _dot(pr["qkm"], wkb))
            glast = jnp.concatenate(
                [jnp.broadcast_to(gcol[j * c + (c - 1 if d == 0 else 0):j * c + (c if d == 0 else 1), :], (c, 1))
                 for j in range(SCAN_G)], axis=0)
            kdb = _bf(pr["k"] * jnp.exp(glast - gcol))
            for j in range(SCAN_G):
                cs = slice(j * c, (j + 1) * c)
                crow = pl.ds(r0 + j * c, c)
                m_scr[d, crow, :] = _bf(_dot_tn(kdb[cs], wkb[cs]))
                b_scr[d, crow, :] = _dot_tn(kdb[cs], ub[cs])
        return carry

    lax.fori_loop(0, n_rows // (gr * PREP_PAR), prep_groups, 0)

    def scan_step(s, d, ci, state):
        r0 = s * seq_len + ci * c
        if not isinstance(r0, int):
            r0 = pl.multiple_of(r0, c)
        rows = pl.ds(r0, c)
        sb = _bf(state)
        oi_scr[d, rows, :] = oi_scr[d, rows, :] + _dot(qg_scr[d, rows, :], sb)
        last = r0 + (c - 1) if d == 0 else r0
        egl = jnp.exp(gb_ref[pl.ds(last, 1), d:d + 1])
        return egl * state - _dot(m_scr[d, rows, :], sb) + b_scr[d, rows, :]

    def init_state(s, d):
        return s0_ref[s, d] if has_s0 else jnp.zeros((c, c), F32)

    chains = [(s, d) for s in range(n_seq) for d in range(2)]
    states = [init_state(s, d) for s, d in chains]
    if nc <= 8:
        for t in range(nc):
            states = [scan_step(s, d, t if d == 0 else nc - 1 - t, st) for (s, d), st in zip(chains, states)]
    else:
        def body(t, sts):
            return tuple(scan_step(s, d, t if d == 0 else nc - 1 - t, st) for (s, d), st in zip(chains, sts))
        states = lax.fori_loop(0, nc, body, tuple(states))
    for (s, d), st in zip(chains, states):
        s_ref[s, d] = st
    o_ref[...] = oi_scr[0] + oi_scr[1]


def _gdn_scan_call(q, k, v, gb, s0, *, row0, n_seq_total, seq_len, seqs_per_step):
    n_rows = seqs_per_step * seq_len
    blk0 = row0 // n_rows
    has_s0 = s0 is not None
    kern = functools.partial(_gdn_scan_kernel, n_seq=seqs_per_step, seq_len=seq_len, has_s0=has_s0)
    hm = pl.BlockSpec((None, n_rows, HEAD_W), lambda s, h: (h, blk0 + s, 0))
    st_spec = pl.BlockSpec((seqs_per_step, 2, None, HEAD_W, HEAD_W), lambda s, h: (s, 0, h, 0, 0))
    in_specs = [hm, hm, hm, pl.BlockSpec((None, n_rows, LANES), lambda s, h: (h, blk0 + s, 0))]
    args = [q, k, v, gb]
    if has_s0:
        in_specs.append(st_spec)
        args.append(s0)
    return pl.pallas_call(
        kern,
        grid=(n_seq_total // seqs_per_step, N_HEADS),
        in_specs=in_specs,
        out_specs=[pl.BlockSpec((None, n_rows, HEAD_W), lambda s, h: (h, s, 0)), st_spec],
        out_shape=[jax.ShapeDtypeStruct((N_HEADS, n_seq_total * seq_len, HEAD_W), F32),
                   jax.ShapeDtypeStruct((n_seq_total, 2, N_HEADS, HEAD_W, HEAD_W), F32)],
        scratch_shapes=[
            pltpu.VMEM((2, n_rows, HEAD_W), F32),
            pltpu.VMEM((2, n_rows, HEAD_W), BF16),
            pltpu.VMEM((2, n_rows, HEAD_W), BF16),
            pltpu.VMEM((2, n_rows, HEAD_W), F32),
        ],
        compiler_params=pltpu.CompilerParams(dimension_semantics=("parallel", "parallel"),
                                             vmem_limit_bytes=VMEM_LIMIT),
        name="gdn_scan",
    )(*args)


def _softmax_rows(scores):
    mx = None
    for s in scores:
        m = jnp.max(s, axis=-1, keepdims=True)
        mx = m if mx is None else jnp.maximum(mx, m)
    ps = [jnp.exp2(s - mx) for s in scores]
    tot = None
    for p in ps:
        l = jnp.sum(p, axis=-1, keepdims=True)
        tot = l if tot is None else tot + l
    return ps, tot


def _attn_kernel(*refs, has_cache, lam_init):
    if has_cache:
        (dq_ref, dk_ref, dv_ref, gq_ref, gk_ref, gv_ref, cdk_ref, cdv_ref, cgk_ref, cgv_ref,
         lp_ref, dn_ref, b64_ref, ob_ref, od_ref) = refs
        dks, dvs = [cdk_ref[...], dk_ref[...]], [cdv_ref[...], dv_ref[...]]
        gks, gvs = [cgk_ref[...], gk_ref[...]], [cgv_ref[...], gv_ref[...]]
    else:
        dq_ref, dk_ref, dv_ref, gq_ref, gk_ref, gv_ref, lp_ref, dn_ref, b64_ref, ob_ref, od_ref = refs
        dks, dvs = [dk_ref[...]], [dv_ref[...]]
        gks, gvs = [gk_ref[...]], [gv_ref[...]]

    lp = lp_ref[...]
    lam = (jnp.exp(jnp.sum(lp[0:1] * lp[1:2], axis=-1, keepdims=True))
           - jnp.exp(jnp.sum(lp[2:3] * lp[3:4], axis=-1, keepdims=True)) + lam_init)

    lane = lax.broadcasted_iota(jnp.int32, (1, BRANCH_W), 1)
    dq = dq_ref[...]
    o = None
    for h in range(N_HEADS):
        pn = []
        for m in range(2):
            lo = h * HEAD_W + m * DIFF_DQK
            qm = dq * jnp.where((lane >= lo) & (lane < lo + DIFF_DQK), 1.0, 0.0).astype(BF16)
            pn.append(_softmax_rows([_dot_nt(qm, ks) for ks in dks]))
        (p1s, l1), (p2s, l2) = pn
        inv1 = 1.0 / l1
        coef = lam * l1 / l2
        oh = None
        for p1, p2, vs in zip(p1s, p2s, dvs):
            t = _dot(_bf(p1 - coef * p2), vs)
            oh = t if oh is None else oh + t
        oh = oh * inv1
        oh = jnp.where(lane // HEAD_W == h, oh, 0.0)
        o = oh if o is None else o + oh
    b64 = b64_ref[...]
    for j in range(BRANCH_W // LANES):
        sl = slice(j * LANES, (j + 1) * LANES)
        x = o[:, sl]
        ms = _group_sum(x * x, b64) * (1.0 / HEAD_W)
        ob_ref[:, sl] = _bf(x * lax.rsqrt(ms + EPS) * dn_ref[:, sl] * (1.0 - lam_init))

    lane_h = lax.broadcasted_iota(jnp.int32, (1, LANES), 1)
    for half in range(2):
        qh = gq_ref[:, half * LANES:(half + 1) * LANES]
        acc = None
        for g in range(GQA_KV_HEADS):
            sel = lane_h // HEAD_W == g
            qm = qh * jnp.where(sel, 1.0, 0.0).astype(BF16)
            ps, tot = _softmax_rows([_dot_nt(qm, ks) for ks in gks])
            og = None
            for p, vs in zip(ps, gvs):
                t = _dot(_bf(p), vs)
                og = t if og is None else og + t
            og = og * (1.0 / tot)
            og = jnp.where(sel, og, 0.0)
            acc = og if acc is None else acc + og
        od_ref[:, half * LANES:(half + 1) * LANES] = _bf(acc)


def _attn_ctx_call(dq, dk, dv, gq, gk, gv, lp, dn, b64, *, layer, n_seq, seq_len, lam_init):
    kern = functools.partial(_attn_kernel, has_cache=False, lam_init=lam_init)

    def rows(width):
        return pl.BlockSpec((seq_len, width), lambda s: (s, 0))

    def const(shape):
        return pl.BlockSpec(shape, lambda s: (0,) * len(shape))

    return pl.pallas_call(
        kern,
        grid=(n_seq,),
        in_specs=[rows(256), rows(256), rows(256), rows(256), rows(128), rows(128),
                  _layer_spec((4, DIFF_DQK), layer), _layer_spec((1, BRANCH_W), layer), const((LANES, LANES))],
        out_specs=[rows(256), rows(256)],
        out_shape=[jax.ShapeDtypeStruct((n_seq * seq_len, BRANCH_W), BF16)] * 2,
        compiler_params=pltpu.CompilerParams(dimension_semantics=("parallel",), vmem_limit_bytes=VMEM_LIMIT),
        name="attn_ctx",
    )(dq, dk, dv, gq, gk, gv, lp, dn, b64)


def _attn_lat_call(dq, dk, dv, gq, gk, gv, cdk, cdv, cgk, cgv, lp, dn, b64, *, layer, row0, n_seq, seq_len,
                   lam_init):
    kern = functools.partial(_attn_kernel, has_cache=True, lam_init=lam_init)
    tq = ATTN_TQ
    qpb = seq_len // tq
    q0 = row0 // tq
    s0 = row0 // seq_len
    past = cdk.shape[1]

    def qrows(width):
        return pl.BlockSpec((tq, width), lambda b, i: (q0 + b * qpb + i, 0))

    def kvrows(width):
        return pl.BlockSpec((seq_len, width), lambda b, i: (s0 + b, 0))

    def cache(width):
        return pl.BlockSpec((None, past, width), lambda b, i: (b, 0, 0))

    def const(shape):
        return pl.BlockSpec(shape, lambda b, i: (0,) * len(shape))

    out = pl.BlockSpec((tq, BRANCH_W), lambda b, i: (b * qpb + i, 0))
    return pl.pallas_call(
        kern,
        grid=(n_seq, qpb),
        in_specs=[qrows(256), kvrows(256), kvrows(256), qrows(256), kvrows(128), kvrows(128),
                  cache(256), cache(256), cache(128), cache(128),
                  _layer_spec((4, DIFF_DQK), layer), _layer_spec((1, BRANCH_W), layer), const((LANES, LANES))],
        out_specs=[out, out],
        out_shape=[jax.ShapeDtypeStruct((n_seq * seq_len, BRANCH_W), BF16)] * 2,
        compiler_params=pltpu.CompilerParams(dimension_semantics=("parallel", "arbitrary"),
                                             vmem_limit_bytes=VMEM_LIMIT),
        name="attn_lat",
    )(dq, dk, dv, gq, gk, gv, cdk, cdv, cgk, cgv, lp, dn, b64)


def _merge_kernel(h_ref, mod_ref, n1_ref, oac_ref, oal_ref, gate_ref, gn_ref, obc_ref, obl_ref, oc_ref,
                  odc_ref, odl_ref, wmg_ref, wb_ref, wo_ref, b64_ref, out_ref, *, tiles_ctx):
    is_ctx = pl.program_id(0) < tiles_ctx
    h = h_ref[...]
    xm = _bf(_norm_mod(h, n1_ref[...], mod_ref[:, D_MODEL:2 * D_MODEL], mod_ref[:, 0:D_MODEL]))
    g1 = mod_ref[:, 2 * D_MODEL:3 * D_MODEL]
    b64 = b64_ref[...]
    oa_parts = []
    for j in range(BRANCH_W // LANES):
        sl = slice(j * LANES, (j + 1) * LANES)
        x = jnp.concatenate([jnp.where(is_ctx, oac_ref[2 * j + hh], oal_ref[2 * j + hh])
                             for hh in range(LANES // HEAD_W)], axis=-1)
        ms = _group_sum(x * x, b64) * (1.0 / HEAD_W)
        oa_parts.append(_bf(x * lax.rsqrt(ms + EPS) * gn_ref[:, sl] * _silu(gate_ref[:, sl])))
    branches = (None, jnp.where(is_ctx, obc_ref[...], obl_ref[...]), oc_ref[...],
                jnp.where(is_ctx, odc_ref[...], odl_ref[...]))
    merged = None
    for n in range(4):
        mg = _dot(xm, wmg_ref[:, n * D_MODEL:(n + 1) * D_MODEL])
        if n == 0:
            up = _dot(oa_parts[0], wb_ref[0, 0:LANES, :]) + _dot(oa_parts[1], wb_ref[0, LANES:2 * LANES, :])
        else:
            up = _dot(branches[n], wb_ref[n])
        term = _sigmoid(mg) * up
        merged = term if merged is None else merged + term
    out_ref[...] = h + g1 * _dot(_bf(merged), wo_ref[...])


def _merge_call(h, mods, n1, oa_ctx, oa_lat, gate, gn, ob_ctx, ob_lat, oc, od_ctx, od_lat, wmg, wb, wo, b64, *,
                layer, tiles_ctx, tiles_per_lat):
    t = h.shape[0]
    last_ctx = tiles_ctx - 1

    def mod_row(i):
        return jnp.where(i < tiles_ctx, 0, 1 + (i - tiles_ctx) // tiles_per_lat)

    def rows(width):
        return pl.BlockSpec((TM, width), lambda i: (i, 0))

    def ctx_rows(width):
        return pl.BlockSpec((TM, width), lambda i: (jnp.minimum(i, last_ctx), 0))

    def lat_rows(width):
        return pl.BlockSpec((TM, width), lambda i: (jnp.maximum(i - tiles_ctx, 0), 0))

    return pl.pallas_call(
        functools.partial(_merge_kernel, tiles_ctx=tiles_ctx),
        grid=(t // TM,),
        in_specs=[
            rows(D_MODEL),
            _mod_spec(layer, mod_row),
            _layer_spec((1, D_MODEL), layer),
            pl.BlockSpec((N_HEADS, TM, HEAD_W), lambda i: (0, jnp.minimum(i, last_ctx), 0)),
            pl.BlockSpec((N_HEADS, TM, HEAD_W), lambda i: (0, jnp.maximum(i - tiles_ctx, 0), 0)),
            rows(BRANCH_W),
            _layer_spec((1, BRANCH_W), layer),
            ctx_rows(BRANCH_W), lat_rows(BRANCH_W), rows(BRANCH_W), ctx_rows(BRANCH_W), lat_rows(BRANCH_W),
            _layer_spec((D_MODEL, 4 * D_MODEL), layer),
            _layer_spec((4, BRANCH_W, D_MODEL), layer),
            _layer_spec((D_MODEL, D_MODEL), layer),
            pl.BlockSpec((LANES, LANES), lambda i: (0, 0)),
        ],
        out_specs=rows(D_MODEL),
        out_shape=jax.ShapeDtypeStruct((t, D_MODEL), F32),
        compiler_params=pltpu.CompilerParams(dimension_semantics=("parallel",), vmem_limit_bytes=VMEM_LIMIT),
        name="merge",
    )(h, mods, n1, oa_ctx, oa_lat, gate, gn, ob_ctx, ob_lat, oc, od_ctx, od_lat, wmg, wb, wo, b64)


ROUTER_ROWS = 32
ROUTER_E0 = 8


def _moe_kernel(h_ref, mod_ref, n2_ref, wr_ref, br_ref, w1_ref, w3_ref, w2_ref, out_ref, xm_scr, gates_scr, acc_scr):
    e = pl.program_id(1)
    tm = h_ref.shape[0]

    @pl.when(e == 0)
    def _():
        xm = _norm_mod(h_ref[...], n2_ref[...], mod_ref[:, 4 * D_MODEL:5 * D_MODEL],
                       mod_ref[:, 3 * D_MODEL:4 * D_MODEL])
        xm_scr[...] = _bf(xm)
        lt = lax.dot_general(wr_ref[...], xm, (((1,), (1,)), ((), ())), precision=lax.Precision.HIGHEST,
                             preferred_element_type=F32) + br_ref[...]
        gl = lt[0:N_GROUPS]
        gidx = lax.broadcasted_iota(jnp.int32, (N_GROUPS, 1), 0)
        gmax = jnp.max(gl, axis=0, keepdims=True)
        gsel = jnp.min(jnp.where(gl == gmax, gidx, N_GROUPS), axis=0, keepdims=True)
        gprob = 1.0 / jnp.sum(jnp.exp(gl - gmax), axis=0, keepdims=True)
        el = lt[ROUTER_E0:ROUTER_E0 + N_EXPERTS]
        eidx = lax.broadcasted_iota(jnp.int32, (N_EXPERTS, 1), 0)
        neg = -jnp.inf
        cand = jnp.where(eidx // EXPERTS_PER_GROUP == gsel, el, neg)
        v1 = jnp.max(cand, axis=0, keepdims=True)
        i1 = jnp.min(jnp.where(cand == v1, eidx, N_EXPERTS), axis=0, keepdims=True)
        cand2 = jnp.where(eidx == i1, neg, cand)
        v2 = jnp.max(cand2, axis=0, keepdims=True)
        i2 = jnp.min(jnp.where(cand2 == v2, eidx, N_EXPERTS), axis=0, keepdims=True)
        ex = jnp.exp(v2 - v1)
        wa = 1.0 / (1.0 + ex)
        gates_t = (jnp.where(eidx == i1, wa, 0.0) + jnp.where(eidx == i2, ex * wa, 0.0)) * gprob
        full = jnp.concatenate([gates_t, jnp.zeros((LANES - N_EXPERTS, tm), F32)], axis=0)
        gates_scr[...] = full.T
        acc_scr[...] = jnp.zeros_like(acc_scr)

    xm = xm_scr[...]
    h1 = _dot(xm, w1_ref[...])
    h3 = _dot(xm, w3_ref[...])
    lane = lax.broadcasted_iota(jnp.int32, (1, LANES), 1)
    gate = jnp.sum(jnp.where(lane == e, gates_scr[...], 0.0), axis=-1, keepdims=True)
    hd = _silu(h1) * h3 * gate
    acc_scr[...] += _dot(_bf(hd), w2_ref[...])

    @pl.when(e == N_EXPERTS - 1)
    def _():
        out_ref[...] = h_ref[...] + mod_ref[:, 5 * D_MODEL:6 * D_MODEL] * acc_scr[...]


def _moe_call(h, mods, n2, wr, br, w1, w3, w2, *, layer, ctx_rows, lat_len):
    t = h.shape[0]
    tm = TM_MOE
    tiles_ctx = ctx_rows // tm
    tiles_per_lat = lat_len // tm

    def mod_row(i):
        return jnp.where(i < tiles_ctx, 0, 1 + (i - tiles_ctx) // tiles_per_lat)

    def expert(shape):
        return pl.BlockSpec((None, None) + shape, lambda i, e: (layer, e, 0, 0))

    return pl.pallas_call(
        _moe_kernel,
        grid=(t // tm, N_EXPERTS),
        in_specs=[
            pl.BlockSpec((tm, D_MODEL), lambda i, e: (i, 0)),
            _mod_spec(layer, mod_row),
            _layer_spec((1, D_MODEL), layer),
            _layer_spec((ROUTER_ROWS, D_MODEL), layer),
            _layer_spec((ROUTER_ROWS, 1), layer),
            expert((D_MODEL, D_EXPERT)),
            expert((D_MODEL, D_EXPERT)),
            expert((D_EXPERT, D_MODEL)),
        ],
        out_specs=pl.BlockSpec((tm, D_MODEL), lambda i, e: (i, 0)),
        out_shape=jax.ShapeDtypeStruct((t, D_MODEL), F32),
        scratch_shapes=[pltpu.VMEM((tm, D_MODEL), BF16), pltpu.VMEM((tm, LANES), F32),
                        pltpu.VMEM((tm, D_MODEL), F32)],
        compiler_params=pltpu.CompilerParams(dimension_semantics=("parallel", "arbitrary"),
                                             vmem_limit_bytes=VMEM_LIMIT),
        name="moe",
    )(h, mods, n2, wr, br, w1, w3, w2)


def _final_norm_kernel(h_ref, g_ref, o_ref):
    h = h_ref[...]
    o_ref[...] = h * lax.rsqrt(jnp.mean(h * h, axis=-1, keepdims=True) + EPS) * g_ref[...]


def _final_norm_call(h, g, *, row0, n_rows):
    blk0 = row0 // TM
    return pl.pallas_call(
        _final_norm_kernel,
        grid=(n_rows // TM,),
        in_specs=[pl.BlockSpec((TM, D_MODEL), lambda i: (blk0 + i, 0)), pl.BlockSpec((1, D_MODEL), lambda i: (0, 0))],
        out_specs=pl.BlockSpec((TM, D_MODEL), lambda i: (i, 0)),
        out_shape=jax.ShapeDtypeStruct((n_rows, D_MODEL), F32),
        compiler_params=pltpu.CompilerParams(dimension_semantics=("parallel",)),
        name="final_norm",
    )(h, g)


def _mixer_weights(w_in):
    depth, d, _ = w_in.shape
    ab_cols = []
    for h in range(N_HEADS):
        ab_cols += [_O_A + h, _O_A + N_HEADS + h, _O_B + h, _O_B + N_HEADS + h]
    ab = jnp.take(w_in[:, :, _O_A:_O_DQ], jnp.asarray([c - _O_A for c in ab_cols], dtype=jnp.int32), axis=2)
    parts = [w_in[:, :, _O_QKV:_O_A], ab, jnp.zeros((depth, d, LANES - len(ab_cols)), w_in.dtype),
             w_in[:, :, _O_DQ:_O_GQ]]
    for h in (0, 2, 1, 3):
        parts.append(w_in[:, :, _O_GQ + h * HEAD_W:_O_GQ + (h + 1) * HEAD_W])
    parts.append(w_in[:, :, _O_GK:_O_MG])
    w_mix = _bf(jnp.concatenate(parts, axis=2))
    assert w_mix.shape[2] == C_END
    return w_mix


def _rope_tables(lat_len):
    pos = jnp.arange(lat_len, dtype=jnp.int32)
    row = (pos // GRID_W).astype(F32)[:, None]
    col = (pos % GRID_W).astype(F32)[:, None]
    lane = jnp.arange(LANES, dtype=jnp.int32)[None, :]
    tabs = []
    for head_w in (DIFF_DQK, HEAD_W):
        half = head_w // 2
        nf = half // 2
        p = lane % head_w
        f = (p % half) % nf
        inv = ROPE_THETA ** (-f.astype(F32) / nf)
        ang = jnp.where(p < half, row, col) * inv
        sign = jnp.where((p % half) < nf, -1.0, 1.0)
        tabs += [jnp.cos(ang), jnp.sin(ang) * sign]
    tab = jnp.stack(tabs)
    ident = jnp.stack([jnp.ones((TM, LANES), F32), jnp.zeros((TM, LANES), F32)] * 2)
    return jnp.concatenate([ident, tab], axis=1)


def _chunk_tri(n, chunk, upper):
    i = jnp.arange(n)[:, None]
    j = jnp.arange(n)[None, :]
    same = (i // chunk) == (j // chunk)
    tri = (j >= i) if upper else (j <= i)
    return jnp.where(same & tri, 1.0, 0.0).astype(BF16)


def kernel(x_prompt, x_sample, state_gdn, cache_diff_k, cache_diff_v, cache_gqa_k, cache_gqa_v, c, c_ctx,
           ada_w, ada_b, norm1, norm2, w_in, gdn_conv, gdn_a_log, gdn_dt_bias, gdn_norm, diff_lambda, diff_norm,
           sgu_norm, sgu_ws, sgu_b, gqa_qnorm, gqa_knorm, w_branch, w_out, moe_wg, moe_bg, moe_we, moe_be,
           moe_w1, moe_w3, moe_w2, final_norm):
    nb, seq, d = x_prompt.shape
    nl, lat_len, _ = x_sample.shape
    depth = ada_w.shape[0]
    past = cache_diff_k.shape[2]
    ctx_rows = nb * seq
    lat_rows = nl * lat_len
    assert d == D_MODEL and seq % PREP_R == 0 and lat_len % TM == 0 and ctx_rows % lat_len == 0
    assert ctx_rows % TM == 0 and 1 + nl <= SUBLANES and lat_len % GRID_W == 0
    assert ctx_rows % TM_MOE == 0 and lat_len % TM_MOE == 0
    tiles_ctx = ctx_rows // TM
    tiles_per_lat = lat_len // TM
    tile_kw = dict(tiles_ctx=tiles_ctx, tiles_per_lat=tiles_per_lat)

    lane = jnp.arange(LANES)
    b64 = jnp.where((lane[:, None] // HEAD_W) == (lane[None, :] // HEAD_W), 1.0, 0.0).astype(BF16)
    tril = _chunk_tri(PREP_R, GDN_CHUNK, upper=False)
    triu = _chunk_tri(PREP_R, GDN_CHUNK, upper=True)
    rope_tab = _rope_tables(lat_len)

    w_mix = _mixer_weights(w_in)
    w_mg = _bf(w_in[:, :, _O_MG:])
    head_perm = jnp.asarray([r for h in (0, 2, 1, 3) for r in range(h * HEAD_W, (h + 1) * HEAD_W)], dtype=jnp.int32)
    wb = _bf(jnp.concatenate([w_branch[:, :3], jnp.take(w_branch[:, 3], head_perm, axis=1)[:, None]], axis=1))
    wo = _bf(w_out)
    w1, w3, w2 = _bf(moe_w1), _bf(moe_w3), _bf(moe_w2)
    wr = jnp.zeros((depth, ROUTER_ROWS, d), F32)
    wr = wr.at[:, 0:N_GROUPS].set(jnp.swapaxes(moe_wg, 1, 2))
    wr = wr.at[:, ROUTER_E0:ROUTER_E0 + N_EXPERTS].set(jnp.swapaxes(moe_we, 1, 2))
    br = jnp.zeros((depth, ROUTER_ROWS, 1), F32)
    br = br.at[:, 0:N_GROUPS, 0].set(moe_bg).at[:, ROUTER_E0:ROUTER_E0 + N_EXPERTS, 0].set(moe_be)
    conv_w = jnp.zeros((depth, SUBLANES, 3 * BRANCH_W), F32).at[:, :CONV_K].set(gdn_conv)
    ab_lane = jnp.zeros((depth, 1, LANES), F32)
    alog_row, dtb_row = ab_lane, ab_lane
    for h in range(N_HEADS):
        for dd in range(2):
            alog_row = alog_row.at[:, 0, 4 * h + dd].set(gdn_a_log[:, dd, h])
            dtb_row = dtb_row.at[:, 0, 4 * h + dd].set(gdn_dt_bias[:, dd, h])
    sgw = _bf(sgu_ws)
    sgb = jnp.repeat(jnp.swapaxes(sgu_b, 1, 2), HEAD_W, axis=2)
    gdn_gn = jnp.tile(gdn_norm, (1, N_HEADS))[:, None, :]
    diff_gn = jnp.tile(diff_norm, (1, N_HEADS))[:, None, :]
    gqn = jnp.tile(gqa_qnorm, (1, LANES // HEAD_W))[:, None, :]
    gkn = jnp.tile(gqa_knorm, (1, LANES // HEAD_W))[:, None, :]

    cond = jnp.zeros((SUBLANES, d), F32).at[0].set(c_ctx).at[1:1 + nl].set(c)
    mods = _ada_call(cond, ada_w, ada_b)[:, :, None, :]
    n1 = norm1[:, None, :]
    n2 = norm2[:, None, :]
    sgn = sgu_norm[:, None, :]

    h = jnp.concatenate([x_prompt.reshape(ctx_rows, d), x_sample.reshape(lat_rows, d)], axis=0)
    new_s, new_dk, new_dv, new_gk, new_gv = [], [], [], [], []
    for l in range(depth):
        lam_init = 0.8 - 0.6 * math.exp(-0.3 * l)
        (gqkv, ggate, gab, dq, dk32, dv32, dk16, dv16, oc, gq, gk32, gv32, gk16, gv16) = _inproj_call(
            h, mods, n1, w_mix, rope_tab, b64, sgn, sgw, sgb, gqn, gkn, layer=l, **tile_kw)

        qh, kh, vh, gb = _gdn_prep_call(gqkv, gab, conv_w, alog_row, dtb_row, b64, tril, triu, layer=l,
                                        ctx_rows=ctx_rows, ctx_len=seq, lat_len=lat_len)
        oa_ctx, s_ctx = _gdn_scan_call(qh, kh, vh, gb, None, row0=0, n_seq_total=nb, seq_len=seq,
                                       seqs_per_step=8 if nb % 8 == 0 else 1)
        oa_lat, _ = _gdn_scan_call(qh, kh, vh, gb, state_gdn[:, l], row0=ctx_rows, n_seq_total=nl,
                                   seq_len=lat_len, seqs_per_step=1)

        ob_ctx, od_ctx = _attn_ctx_call(dq, dk16, dv16, gq, gk16, gv16, diff_lambda, diff_gn, b64, layer=l,
                                        n_seq=nb, seq_len=seq, lam_init=lam_init)
        ob_lat, od_lat = _attn_lat_call(
            dq, dk16, dv16, gq, gk16, gv16,
            _bf(cache_diff_k[:, l].reshape(nl, past, BRANCH_W)), _bf(cache_diff_v[:, l].reshape(nl, past, BRANCH_W)),
            _bf(cache_gqa_k[:, l].reshape(nl, past, LANES)), _bf(cache_gqa_v[:, l].reshape(nl, past, LANES)),
            diff_lambda, diff_gn, b64, layer=l, row0=ctx_rows, n_seq=nl, seq_len=lat_len, lam_init=lam_init)

        h = _merge_call(h, mods, n1, oa_ctx, oa_lat, ggate, gdn_gn, ob_ctx, ob_lat, oc, od_ctx, od_lat,
                        w_mg, wb, wo, b64, layer=l, **tile_kw)
        h = _moe_call(h, mods, n2, wr, br, w1, w3, w2, layer=l, ctx_rows=ctx_rows, lat_len=lat_len)

        new_s.append(s_ctx)
        new_dk.append(dk32[:ctx_rows].reshape(nb, seq, N_HEADS, HEAD_W))
        new_dv.append(dv32[:ctx_rows].reshape(nb, seq, N_HEADS, HEAD_W))
        new_gk.append(gk32[:ctx_rows].reshape(nb, seq, GQA_KV_HEADS, HEAD_W))
        new_gv.append(gv32[:ctx_rows].reshape(nb, seq, GQA_KV_HEADS, HEAD_W))

    fg = final_norm[None, :]
    y_ctx = _final_norm_call(h, fg, row0=0, n_rows=ctx_rows)
    y_lat = _final_norm_call(h, fg, row0=ctx_rows, n_rows=lat_rows)
    return (y_ctx.reshape(nb, seq, d), y_lat.reshape(nl, lat_len, d),
            jnp.stack(new_s, axis=1), jnp.stack(new_dk, axis=1), jnp.stack(new_dv, axis=1),
            jnp.stack(new_gk, axis=1), jnp.stack(new_gv, axis=1))
```

```python
import functools
import math

import jax
import jax.numpy as jnp
from jax import lax
from jax.experimental import pallas as pl
from jax.experimental.pallas import tpu as pltpu

F32 = jnp.float32
BF16 = jnp.bfloat16

GRID_W = 64
ROPE_THETA = 10000.0
EPS = 1e-6
D_MODEL = 1024
BRANCH_W = 256
HEAD_W = 64
DIFF_DQK = 32
N_HEADS = 4
GQA_KV_HEADS = 2
GDN_CHUNK = 64
CONV_K = 5
SGU_CHUNK = 128
SGU_GROUPS = 4
N_GROUPS = 4
EXPERTS_PER_GROUP = 4
N_EXPERTS = 16
D_EXPERT = 256
LOG2E = 1.4426950408889634
DIFF_SCALE = DIFF_DQK ** -0.5 * LOG2E
GQA_SCALE = HEAD_W ** -0.5 * LOG2E
GDN_Q_SCALE = HEAD_W ** -0.5

LANES = 128
SUBLANES = 8
VMEM_LIMIT = 56 * 1024 * 1024

_IN_WIDTHS = (768, 256, 8, 8, 256, 256, 256, 256, 256, 256, 128, 128, 4096)
_IN_OFFS = tuple(sum(_IN_WIDTHS[:i]) for i in range(len(_IN_WIDTHS)))
(_O_QKV, _O_GATE, _O_A, _O_B, _O_DQ, _O_DK, _O_DV, _O_SU, _O_SV, _O_GQ, _O_GK, _O_GV, _O_MG) = _IN_OFFS
MIX_COLS = _O_MG

C_QKV, C_GATE, C_AB, C_DQ, C_DK, C_DV, C_SU, C_SV, C_GQ, C_GK, C_GV, C_END = (
    0, 768, 1024, 1152, 1408, 1664, 1920, 2176, 2432, 2688, 2816, 2944)

TM = 512
TM_MOE = 1024
PREP_R = 256
HALO = SUBLANES
SCAN_G = 4
PREP_PAR = 2
ATTN_TQ = 256


def _bf(x):
    return x.astype(BF16)


def _dot(a, b):
    return jnp.dot(a, b, preferred_element_type=F32)


def _dot_nt(a, b):
    return lax.dot_general(a, b, (((1,), (1,)), ((), ())), preferred_element_type=F32)


def _dot_tn(a, b):
    return lax.dot_general(a, b, (((0,), (0,)), ((), ())), preferred_element_type=F32)


def _split_dot(x, m, terms):
    acc = None
    r = x
    for t in range(terms):
        p = _bf(r)
        d = _dot(p, m)
        acc = d if acc is None else acc + d
        if t + 1 < terms:
            r = r - p.astype(F32)
    return acc


def _split_dot_left(m, x, terms):
    acc = None
    r = x
    for t in range(terms):
        p = _bf(r)
        d = _dot(m, p)
        acc = d if acc is None else acc + d
        if t + 1 < terms:
            r = r - p.astype(F32)
    return acc


def _sigmoid(x):
    return 1.0 / (1.0 + jnp.exp(-x))


def _silu(x):
    return x * _sigmoid(x)


def _norm_mod(h, gamma, scale, shift):
    ms = jnp.mean(h * h, axis=-1, keepdims=True)
    return (h * lax.rsqrt(ms + EPS) * gamma) * (1.0 + scale) + shift


def _group_sum(sq, b64):
    return _split_dot(sq, b64, 2)


def _rope(x, cos, sin_signed, nf):
    lane = lax.broadcasted_iota(jnp.int32, (1, LANES), 1)
    first = (lane % (2 * nf)) < nf
    partner = jnp.where(first, pltpu.roll(x, LANES - nf, 1), pltpu.roll(x, nf, 1))
    return x * cos + partner * sin_signed


def _layer_spec(shape, layer):
    return pl.BlockSpec((None,) + tuple(shape), lambda *_: (layer,) + (0,) * len(shape))


def _mod_spec(layer, mod_row):
    return pl.BlockSpec((None, None, 1, 6 * D_MODEL), lambda i, *_: (layer, mod_row(i), 0, 0))


def _ada_kernel(c_ref, w_ref, b_ref, o_ref):
    x = _silu(c_ref[...])
    o_ref[...] = jnp.dot(x, w_ref[...], precision=lax.Precision.HIGHEST,
                         preferred_element_type=F32) + b_ref[...]


def _ada_call(cond, ada_w, ada_b):
    n_layers, d, n_out = ada_w.shape
    tn = 1536
    return pl.pallas_call(
        _ada_kernel,
        grid=(n_layers, n_out // tn),
        in_specs=[
            pl.BlockSpec((SUBLANES, d), lambda l, j: (0, 0)),
            pl.BlockSpec((None, d, tn), lambda l, j: (l, 0, j)),
            pl.BlockSpec((None, 1, tn), lambda l, j: (l, 0, j)),
        ],
        out_specs=pl.BlockSpec((None, SUBLANES, tn), lambda l, j: (l, 0, j)),
        out_shape=jax.ShapeDtypeStruct((n_layers, SUBLANES, n_out), F32),
        compiler_params=pltpu.CompilerParams(vmem_limit_bytes=VMEM_LIMIT),
        name="ada_mod",
    )(cond, ada_w, ada_b.reshape(n_layers, 1, n_out))


def _inproj_kernel(h_ref, mod_ref, n1_ref, w_ref, rope_ref, b64_ref, sgn_ref, sgw_ref, sgb_ref, gqn_ref, gkn_ref,
                   gqkv_ref, ggate_ref, gab_ref, dq_ref, dk32_ref, dv32_ref, dk16_ref, dvt_ref, oc_ref,
                   gq_ref, gk32_ref, gv32_ref, gk16_ref, gvt_ref):
    tm = h_ref.shape[0]
    xm = _bf(_norm_mod(h_ref[...], n1_ref[...], mod_ref[:, D_MODEL:2 * D_MODEL], mod_ref[:, 0:D_MODEL]))

    def proj(a, b):
        return _dot(xm, w_ref[:, a:b])

    gqkv_ref[...] = proj(C_QKV, C_GATE)
    ggate_ref[...] = proj(C_GATE, C_AB)
    gab_ref[...] = proj(C_AB, C_DQ)

    cos_d, sin_d, cos_g, sin_g = rope_ref[0], rope_ref[1], rope_ref[2], rope_ref[3]
    b64 = b64_ref[...]

    for j in range(BRANCH_W // LANES):
        sl = slice(j * LANES, (j + 1) * LANES)
        q = proj(C_DQ + j * LANES, C_DQ + (j + 1) * LANES)
        dq_ref[:, sl] = _bf(_rope(q, cos_d, sin_d, DIFF_DQK // 4) * DIFF_SCALE)
        k = _rope(proj(C_DK + j * LANES, C_DK + (j + 1) * LANES), cos_d, sin_d, DIFF_DQK // 4)
        dk32_ref[:, sl] = k
        dk16_ref[:, sl] = _bf(k)
    v = proj(C_DV, C_SU)
    dv32_ref[...] = v
    dvt_ref[...] = _bf(v.T)

    u = proj(C_SU, C_SV)
    sv = proj(C_SV, C_GQ)
    vn = _bf(sv * lax.rsqrt(jnp.mean(sv * sv, axis=-1, keepdims=True) + EPS) * sgn_ref[...])
    group = lax.broadcasted_iota(jnp.int32, (1, BRANCH_W), 1) // HEAD_W
    for r in range(tm // SGU_CHUNK):
        rows = slice(r * SGU_CHUNK, (r + 1) * SGU_CHUNK)
        vc = vn[rows]
        z = sgb_ref[...]
        for g in range(SGU_GROUPS):
            z = z + jnp.where(group == g, _dot(sgw_ref[g], vc), 0.0)
        oc_ref[rows, :] = _bf(u[rows] * z)

    for j in range(BRANCH_W // LANES):
        sl = slice(j * LANES, (j + 1) * LANES)
        q = proj(C_GQ + j * LANES, C_GQ + (j + 1) * LANES)
        qn = q * lax.rsqrt(_group_sum(q * q, b64) * (1.0 / HEAD_W) + EPS) * gqn_ref[...]
        gq_ref[:, sl] = _bf(_rope(qn, cos_g, sin_g, HEAD_W // 4) * GQA_SCALE)
    k = proj(C_GK, C_GV)
    kn = k * lax.rsqrt(_group_sum(k * k, b64) * (1.0 / HEAD_W) + EPS) * gkn_ref[...]
    kr = _rope(kn, cos_g, sin_g, HEAD_W // 4)
    gk32_ref[...] = kr
    gk16_ref[...] = _bf(kr)
    v = proj(C_GV, C_END)
    gv32_ref[...] = v
    gvt_ref[...] = _bf(v.T)


def _inproj_call(h, mods, n1, w_mix, rope_tab, b64, sgn, sgw, sgb, gqn, gkn, *, layer, tiles_ctx, tiles_per_lat):
    t = h.shape[0]
    n_tiles = t // TM

    def mod_row(i):
        return jnp.where(i < tiles_ctx, 0, 1 + (i - tiles_ctx) // tiles_per_lat)

    def rope_blk(i):
        return jnp.where(i < tiles_ctx, 0, 1 + (i - tiles_ctx) % tiles_per_lat)

    def const(shape):
        return pl.BlockSpec(shape, lambda i: (0,) * len(shape))

    def rows(width):
        return pl.BlockSpec((TM, width), lambda i: (i, 0))

    def cols(height):
        return pl.BlockSpec((height, TM), lambda i: (0, i))

    row_major = lambda w, dt: (rows(w), jax.ShapeDtypeStruct((t, w), dt))
    col_major = lambda hgt, dt: (cols(hgt), jax.ShapeDtypeStruct((hgt, t), dt))
    outs = [
        row_major(768, F32), row_major(256, F32), row_major(128, F32),
        row_major(256, BF16), row_major(256, F32), row_major(256, F32),
        row_major(256, BF16), col_major(256, BF16),
        row_major(256, BF16),
        row_major(256, BF16), row_major(128, F32), row_major(128, F32),
        row_major(128, BF16), col_major(128, BF16),
    ]
    return pl.pallas_call(
        _inproj_kernel,
        grid=(n_tiles,),
        in_specs=[
            rows(D_MODEL),
            _mod_spec(layer, mod_row),
            _layer_spec((1, D_MODEL), layer),
            _layer_spec((D_MODEL, C_END), layer),
            pl.BlockSpec((4, TM, LANES), lambda i: (0, rope_blk(i), 0)),
            const((LANES, LANES)),
            _layer_spec((1, BRANCH_W), layer),
            _layer_spec((SGU_GROUPS, SGU_CHUNK, SGU_CHUNK), layer),
            _layer_spec((SGU_CHUNK, BRANCH_W), layer),
            _layer_spec((1, LANES), layer),
            _layer_spec((1, LANES), layer),
        ],
        out_specs=[spec for spec, _ in outs],
        out_shape=[shape for _, shape in outs],
        compiler_params=pltpu.CompilerParams(dimension_semantics=("parallel",), vmem_limit_bytes=VMEM_LIMIT),
        name="inproj",
    )(h, mods, n1, w_mix, rope_tab, b64, sgn, sgw, sgb, gqn, gkn)


def _gdn_prep_kernel(x_ref, xp_ref, xn_ref, ab_ref, cw_ref, alog_ref, dtb_ref, b64_ref, tril_ref, triu_ref,
                     q_ref, k_ref, v_ref, gb_ref, *, ctx_blocks, ctx_bps, lat_bps):
    r = x_ref.shape[0]
    i = pl.program_id(0)
    is_ctx = i < ctx_blocks
    pos = jnp.where(is_ctx, i % ctx_bps, (i - ctx_blocks) % lat_bps)
    bps = jnp.where(is_ctx, ctx_bps, lat_bps)
    prev = jnp.where(pos > 0, xp_ref[...], 0.0)
    nxt = jnp.where(pos < bps - 1, xn_ref[...], 0.0)
    win = jnp.concatenate([prev, x_ref[...], nxt], axis=0)
    n = r + 2 * HALO
    y = None
    for j in range(CONV_K):
        s = (CONV_K // 2 - j) % n
        shifted = win if s == 0 else pltpu.roll(win, s, 0)
        term = cw_ref[j:j + 1, :] * shifted[HALO:HALO + r]
        y = term if y is None else y + term
    y = _silu(y)

    b64 = b64_ref[...]
    for part, ref, scale in ((0, q_ref, GDN_Q_SCALE), (1, k_ref, 1.0)):
        for j in range(BRANCH_W // LANES):
            x = y[:, part * BRANCH_W + j * LANES: part * BRANCH_W + (j + 1) * LANES]
            xn = x * (lax.rsqrt(_group_sum(x * x, b64) + EPS) * scale)
            for hh in range(LANES // HEAD_W):
                ref[j * (LANES // HEAD_W) + hh] = xn[:, hh * HEAD_W:(hh + 1) * HEAD_W]
    for h in range(N_HEADS):
        v_ref[h] = y[:, 2 * BRANCH_W + h * HEAD_W: 2 * BRANCH_W + (h + 1) * HEAD_W]

    ab = ab_ref[...]
    lane = lax.broadcasted_iota(jnp.int32, (1, LANES), 1)
    lm = lane % 4
    xa = ab + dtb_ref[...]
    softplus = jnp.maximum(xa, 0.0) + jnp.log1p(jnp.exp(-jnp.abs(xa)))
    la = jnp.where((lm < 2) & (lane < 4 * N_HEADS), -jnp.exp(alog_ref[...]) * softplus, 0.0)
    g_fwd = _split_dot_left(tril_ref[...], la, 3)
    g_bwd = _split_dot_left(triu_ref[...], la, 3)
    gb = jnp.where(lm == 0, g_fwd, jnp.where(lm == 1, g_bwd, _sigmoid(ab)))
    for h in range(N_HEADS):
        gb_ref[h] = gb if h == 0 else pltpu.roll(gb, LANES - 4 * h, 1)


def _gdn_prep_call(gqkv, gab, conv_w, alog_row, dtb_row, b64, tril, triu, *, layer, ctx_rows, ctx_len, lat_len):
    t = gqkv.shape[0]
    r = PREP_R
    n_blocks = t // r
    rh = r // HALO
    last_halo = t // HALO - 1
    kern = functools.partial(_gdn_prep_kernel, ctx_blocks=ctx_rows // r, ctx_bps=ctx_len // r, lat_bps=lat_len // r)

    def const(shape):
        return pl.BlockSpec(shape, lambda i: (0,) * len(shape))

    head_major = pl.BlockSpec((N_HEADS, r, HEAD_W), lambda i: (0, i, 0))
    return pl.pallas_call(
        kern,
        grid=(n_blocks,),
        in_specs=[
            pl.BlockSpec((r, 3 * BRANCH_W), lambda i: (i, 0)),
            pl.BlockSpec((HALO, 3 * BRANCH_W), lambda i: (jnp.maximum(i * rh - 1, 0), 0)),
            pl.BlockSpec((HALO, 3 * BRANCH_W), lambda i: (jnp.minimum((i + 1) * rh, last_halo), 0)),
            pl.BlockSpec((r, LANES), lambda i: (i, 0)),
            _layer_spec((SUBLANES, 3 * BRANCH_W), layer),
            _layer_spec((1, LANES), layer),
            _layer_spec((1, LANES), layer),
            const((LANES, LANES)),
            const((r, r)),
            const((r, r)),
        ],
        out_specs=[head_major, head_major, head_major,
                   pl.BlockSpec((N_HEADS, r, LANES), lambda i: (0, i, 0))],
        out_shape=[jax.ShapeDtypeStruct((N_HEADS, t, HEAD_W), F32)] * 3
        + [jax.ShapeDtypeStruct((N_HEADS, t, LANES), F32)],
        compiler_params=pltpu.CompilerParams(dimension_semantics=("parallel",), vmem_limit_bytes=VMEM_LIMIT),
        name="gdn_prep",
    )(gqkv, gqkv, gqkv, gab, conv_w, alog_row, dtb_row, b64, tril, triu)


def _gdn_scan_kernel(*refs, n_seq, seq_len, has_s0):
    if has_s0:
        q_ref, k_ref, v_ref, gb_ref, s0_ref, o_ref, s_ref, oi_scr, qg_scr, m_scr, b_scr = refs
    else:
        q_ref, k_ref, v_ref, gb_ref, o_ref, s_ref, oi_scr, qg_scr, m_scr, b_scr = refs
        s0_ref = None
    c = GDN_CHUNK
    gr = SCAN_G * c
    n_rows = n_seq * seq_len
    nc = seq_len // c
    row_i = lax.broadcasted_iota(jnp.int32, (gr, gr), 0)
    col_i = lax.broadcasted_iota(jnp.int32, (gr, gr), 1)
    eye = jnp.where(row_i == col_i, 1.0, 0.0)
    same = (row_i // c) == (col_i // c)
    incl = (same & (col_i <= row_i), same & (col_i >= row_i))
    strict = (same & (col_i < row_i), same & (col_i > row_i))
    blk16 = (row_i // 16) == (col_i // 16)
    blk32 = ((row_i // 32) == (col_i // 32)) & ~blk16
    blk64 = same & ((row_i // 32) != (col_i // 32))

    def prep_groups(gi, carry):
        probs = []
        for j in range(PREP_PAR):
            r0 = pl.multiple_of((gi * PREP_PAR + j) * gr, gr)
            rows = pl.ds(r0, gr)
            q = q_ref[rows, :]
            k = k_ref[rows, :]
            v = v_ref[rows, :]
            gbc = gb_ref[rows, :]
            gbt = gbc.T
            kb = _bf(k)
            kk = _dot_nt(kb, kb)
            qk = _dot_nt(_bf(q), kb)
            for d in range(2):
                gcol = gbc[:, d:d + 1]
                bcol = gbc[:, 2 + d:3 + d]
                dec = jnp.where(incl[d], jnp.exp(jnp.where(incl[d], gcol - gbt[d:d + 1, :], 0.0)), 0.0)
                probs.append(dict(d=d, r0=r0, rows=rows, q=q, k=k, v=v, gcol=gcol, bcol=bcol, dec=dec,
                                  qkm=_bf(qk * dec), nmat=jnp.where(strict[d], bcol * kk * dec, 0.0)))
        ps = [jnp.where(blk16, pr["nmat"], 0.0) for pr in probs]
        ts = [eye - p for p in ps]
        for _ in range(3):
            pbs = [_bf(p) for p in ps]
            ps = [_dot(pb, pb) for pb in pbs]
            ts = [t + _dot(_bf(t), _bf(p)) for t, p in zip(ts, ps)]
        for level in (blk32, blk64):
            tbs = [_bf(t) for t in ts]
            xs = [_dot(tb, _bf(jnp.where(level, pr["nmat"], 0.0))) for tb, pr in zip(tbs, probs)]
            ts = [t - _dot(_bf(x), tb) for t, x, tb in zip(ts, xs, tbs)]
        ainvs = [_bf(t) for t in ts]
        egs = [jnp.exp(pr["gcol"]) for pr in probs]
        ubs = [_bf(_dot(ai, _bf(pr["bcol"] * pr["v"]))) for ai, pr in zip(ainvs, probs)]
        wkbs = [_bf(_dot(ai, _bf((pr["bcol"] * eg) * pr["k"]))) for ai, pr, eg in zip(ainvs, probs, egs)]
        for pr, eg, ub, wkb in zip(probs, egs, ubs, wkbs):
            d, r0, rows, gcol = pr["d"], pr["r0"], pr["rows"], pr["gcol"]
            oi_scr[d, rows, :] = _dot(pr["qkm"], ub)
            qg_scr[d, rows, :] = _bf(pr["q"] * eg - _dot(pr["qkm"], wkb))
            glast = jnp.concatenate(
                [jnp.broadcast_to(gcol[j * c + (c - 1 if d == 0 else 0):j * c + (c if d == 0 else 1), :], (c, 1))
                 for j in range(SCAN_G)], axis=0)
            kdb = _bf(pr["k"] * jnp.exp(glast - gcol))
            for j in range(SCAN_G):
                cs = slice(j * c, (j + 1) * c)
                crow = pl.ds(r0 + j * c, c)
                m_scr[d, crow, :] = _bf(_dot_tn(kdb[cs], wkb[cs]))
                b_scr[d, crow, :] = _dot_tn(kdb[cs], ub[cs])
        return carry

    lax.fori_loop(0, n_rows // (gr * PREP_PAR), prep_groups, 0)

    def scan_step(s, d, ci, state):
        r0 = s * seq_len + ci * c
        if not isinstance(r0, int):
            r0 = pl.multiple_of(r0, c)
        rows = pl.ds(r0, c)
        sb = _bf(state)
        oi_scr[d, rows, :] = oi_scr[d, rows, :] + _dot(qg_scr[d, rows, :], sb)
        last = r0 + (c - 1) if d == 0 else r0
        egl = jnp.exp(gb_ref[pl.ds(last, 1), d:d + 1])
        return egl * state - _dot(m_scr[d, rows, :], sb) + b_scr[d, rows, :]

    def init_state(s, d):
        return s0_ref[s, d] if has_s0 else jnp.zeros((c, c), F32)

    chains = [(s, d) for s in range(n_seq) for d in range(2)]
    states = [init_state(s, d) for s, d in chains]
    if nc <= 8:
        for t in range(nc):
            states = [scan_step(s, d, t if d == 0 else nc - 1 - t, st) for (s, d), st in zip(chains, states)]
    else:
        def body(t, sts):
            return tuple(scan_step(s, d, t if d == 0 else nc - 1 - t, st) for (s, d), st in zip(chains, sts))
        states = lax.fori_loop(0, nc, body, tuple(states))
    for (s, d), st in zip(chains, states):
        s_ref[s, d] = st
    o_ref[...] = oi_scr[0] + oi_scr[1]


def _gdn_scan_call(q, k, v, gb, s0, *, row0, n_seq_total, seq_len, seqs_per_step):
    n_rows = seqs_per_step * seq_len
    blk0 = row0 // n_rows
    has_s0 = s0 is not None
    kern = functools.partial(_gdn_scan_kernel, n_seq=seqs_per_step, seq_len=seq_len, has_s0=has_s0)
    hm = pl.BlockSpec((None, n_rows, HEAD_W), lambda s, h: (h, blk0 + s, 0))
    st_spec = pl.BlockSpec((seqs_per_step, 2, None, HEAD_W, HEAD_W), lambda s, h: (s, 0, h, 0, 0))
    in_specs = [hm, hm, hm, pl.BlockSpec((None, n_rows, LANES), lambda s, h: (h, blk0 + s, 0))]
    args = [q, k, v, gb]
    if has_s0:
        in_specs.append(st_spec)
        args.append(s0)
    return pl.pallas_call(
        kern,
        grid=(n_seq_total // seqs_per_step, N_HEADS),
        in_specs=in_specs,
        out_specs=[pl.BlockSpec((None, n_rows, HEAD_W), lambda s, h: (h, s, 0)), st_spec],
        out_shape=[jax.ShapeDtypeStruct((N_HEADS, n_seq_total * seq_len, HEAD_W), F32),
                   jax.ShapeDtypeStruct((n_seq_total, 2, N_HEADS, HEAD_W, HEAD_W), F32)],
        scratch_shapes=[
            pltpu.VMEM((2, n_rows, HEAD_W), F32),
            pltpu.VMEM((2, n_rows, HEAD_W), BF16),
            pltpu.VMEM((2, n_rows, HEAD_W), BF16),
            pltpu.VMEM((2, n_rows, HEAD_W), F32),
        ],
        compiler_params=pltpu.CompilerParams(dimension_semantics=("parallel", "parallel"),
                                             vmem_limit_bytes=VMEM_LIMIT),
        name="gdn_scan",
    )(*args)


ONES_ROWS = 16


def _attend_all(tasks):
    def scores(task):
        qm, ks, _ = task
        return [_dot_nt(k, qm) for k in ks]

    outs = []
    sts = scores(tasks[0])
    for i, (_, _, vts) in enumerate(tasks):
        nxt = scores(tasks[i + 1]) if i + 1 < len(tasks) else None
        mx = None
        for st in sts:
            m = jnp.max(st, axis=0, keepdims=True)
            mx = m if mx is None else jnp.maximum(mx, m)
        acc = None
        for st, vt in zip(sts, vts):
            t = _dot(vt, _bf(jnp.exp2(st - mx)))
            acc = t if acc is None else acc + t
        outs.append(acc[:HEAD_W] * (1.0 / acc[HEAD_W:HEAD_W + 1]))
        sts = nxt
    return outs


def _with_ones(vt_rows):
    return jnp.concatenate([vt_rows, jnp.ones((ONES_ROWS, vt_rows.shape[1]), vt_rows.dtype)], axis=0)


def _attn_kernel(*refs, has_cache, lam_init):
    if has_cache:
        (dq_ref, dk_ref, dvt_ref, gq_ref, gk_ref, gvt_ref, cdk_ref, cdvt_ref, cgk_ref, cgvt_ref,
         lp_ref, dn_ref, b64_ref, ob_ref, od_ref) = refs
        dks, dvts = [cdk_ref[...], dk_ref[...]], [cdvt_ref, dvt_ref]
        gks, gvts = [cgk_ref[...], gk_ref[...]], [cgvt_ref, gvt_ref]
    else:
        dq_ref, dk_ref, dvt_ref, gq_ref, gk_ref, gvt_ref, lp_ref, dn_ref, b64_ref, ob_ref, od_ref = refs
        dks, dvts = [dk_ref[...]], [dvt_ref]
        gks, gvts = [gk_ref[...]], [gvt_ref]

    lp = lp_ref[...]
    lam = (jnp.exp(jnp.sum(lp[0:1] * lp[1:2], axis=-1, keepdims=True))
           - jnp.exp(jnp.sum(lp[2:3] * lp[3:4], axis=-1, keepdims=True)) + lam_init)

    lane = lax.broadcasted_iota(jnp.int32, (1, BRANCH_W), 1)
    lane_h = lax.broadcasted_iota(jnp.int32, (1, LANES), 1)
    dq = dq_ref[...]
    tasks = []
    for h in range(N_HEADS):
        vts = [_with_ones(r[h * HEAD_W:(h + 1) * HEAD_W, :]) for r in dvts]
        for m in range(2):
            lo = h * HEAD_W + m * DIFF_DQK
            tasks.append((dq * jnp.where((lane >= lo) & (lane < lo + DIFF_DQK), 1.0, 0.0).astype(BF16), dks, vts))
    kv_vts = [[_with_ones(r[g * HEAD_W:(g + 1) * HEAD_W, :]) for r in gvts] for g in range(GQA_KV_HEADS)]
    for half in range(2):
        qh = gq_ref[:, half * LANES:(half + 1) * LANES]
        for g in range(GQA_KV_HEADS):
            tasks.append((qh * jnp.where(lane_h // HEAD_W == g, 1.0, 0.0).astype(BF16), gks, kv_vts[g]))
    outs = _attend_all(tasks)

    o = jnp.concatenate([outs[2 * h] - lam * outs[2 * h + 1] for h in range(N_HEADS)], axis=0).T
    b64 = b64_ref[...]
    for j in range(BRANCH_W // LANES):
        sl = slice(j * LANES, (j + 1) * LANES)
        x = o[:, sl]
        ms = _group_sum(x * x, b64) * (1.0 / HEAD_W)
        ob_ref[:, sl] = _bf(x * lax.rsqrt(ms + EPS) * dn_ref[:, sl] * (1.0 - lam_init))
    od_ref[...] = _bf(jnp.concatenate(outs[2 * N_HEADS:], axis=0).T)


def _attn_ctx_call(dq, dk, dvt, gq, gk, gvt, lp, dn, b64, *, layer, n_seq, seq_len, lam_init):
    kern = functools.partial(_attn_kernel, has_cache=False, lam_init=lam_init)

    def rows(width):
        return pl.BlockSpec((seq_len, width), lambda s: (s, 0))

    def cols(height):
        return pl.BlockSpec((height, seq_len), lambda s: (0, s))

    def const(shape):
        return pl.BlockSpec(shape, lambda s: (0,) * len(shape))

    return pl.pallas_call(
        kern,
        grid=(n_seq,),
        in_specs=[rows(256), rows(256), cols(256), rows(256), rows(128), cols(128),
                  _layer_spec((4, DIFF_DQK), layer), _layer_spec((1, BRANCH_W), layer), const((LANES, LANES))],
        out_specs=[rows(256), rows(256)],
        out_shape=[jax.ShapeDtypeStruct((n_seq * seq_len, BRANCH_W), BF16)] * 2,
        compiler_params=pltpu.CompilerParams(dimension_semantics=("parallel",), vmem_limit_bytes=VMEM_LIMIT),
        name="attn_ctx",
    )(dq, dk, dvt, gq, gk, gvt, lp, dn, b64)


def _attn_lat_call(dq, dk, dvt, gq, gk, gvt, cdk, cdvt, cgk, cgvt, lp, dn, b64, *, layer, row0, n_seq, seq_len,
                   lam_init):
    kern = functools.partial(_attn_kernel, has_cache=True, lam_init=lam_init)
    tq = ATTN_TQ
    qpb = seq_len // tq
    q0 = row0 // tq
    s0 = row0 // seq_len
    past = cdk.shape[1]

    def qrows(width):
        return pl.BlockSpec((tq, width), lambda b, i: (q0 + b * qpb + i, 0))

    def kvrows(width):
        return pl.BlockSpec((seq_len, width), lambda b, i: (s0 + b, 0))

    def kvcols(height):
        return pl.BlockSpec((height, seq_len), lambda b, i: (0, s0 + b))

    def cache(width):
        return pl.BlockSpec((None, past, width), lambda b, i: (b, 0, 0))

    def cache_t(height):
        return pl.BlockSpec((None, height, past), lambda b, i: (b, 0, 0))

    def const(shape):
        return pl.BlockSpec(shape, lambda b, i: (0,) * len(shape))

    out = pl.BlockSpec((tq, BRANCH_W), lambda b, i: (b * qpb + i, 0))
    return pl.pallas_call(
        kern,
        grid=(n_seq, qpb),
        in_specs=[qrows(256), kvrows(256), kvcols(256), qrows(256), kvrows(128), kvcols(128),
                  cache(256), cache_t(256), cache(128), cache_t(128),
                  _layer_spec((4, DIFF_DQK), layer), _layer_spec((1, BRANCH_W), layer), const((LANES, LANES))],
        out_specs=[out, out],
        out_shape=[jax.ShapeDtypeStruct((n_seq * seq_len, BRANCH_W), BF16)] * 2,
        compiler_params=pltpu.CompilerParams(dimension_semantics=("parallel", "arbitrary"),
                                             vmem_limit_bytes=VMEM_LIMIT),
        name="attn_lat",
    )(dq, dk, dvt, gq, gk, gvt, cdk, cdvt, cgk, cgvt, lp, dn, b64)


def _merge_kernel(h_ref, mod_ref, n1_ref, oac_ref, oal_ref, gate_ref, gn_ref, obc_ref, obl_ref, oc_ref,
                  odc_ref, odl_ref, wmg_ref, wb_ref, wo_ref, b64_ref, out_ref, *, tiles_ctx):
    is_ctx = pl.program_id(0) < tiles_ctx
    h = h_ref[...]
    xm = _bf(_norm_mod(h, n1_ref[...], mod_ref[:, D_MODEL:2 * D_MODEL], mod_ref[:, 0:D_MODEL]))
    g1 = mod_ref[:, 2 * D_MODEL:3 * D_MODEL]
    b64 = b64_ref[...]
    oa_parts = []
    for j in range(BRANCH_W // LANES):
        sl = slice(j * LANES, (j + 1) * LANES)
        x = jnp.concatenate([jnp.where(is_ctx, oac_ref[2 * j + hh], oal_ref[2 * j + hh])
                             for hh in range(LANES // HEAD_W)], axis=-1)
        ms = _group_sum(x * x, b64) * (1.0 / HEAD_W)
        oa_parts.append(_bf(x * lax.rsqrt(ms + EPS) * gn_ref[:, sl] * _silu(gate_ref[:, sl])))
    branches = (None, jnp.where(is_ctx, obc_ref[...], obl_ref[...]), oc_ref[...],
                jnp.where(is_ctx, odc_ref[...], odl_ref[...]))
    merged = None
    for n in range(4):
        mg = _dot(xm, wmg_ref[:, n * D_MODEL:(n + 1) * D_MODEL])
        if n == 0:
            up = _dot(oa_parts[0], wb_ref[0, 0:LANES, :]) + _dot(oa_parts[1], wb_ref[0, LANES:2 * LANES, :])
        else:
            up = _dot(branches[n], wb_ref[n])
        term = _sigmoid(mg) * up
        merged = term if merged is None else merged + term
    out_ref[...] = h + g1 * _dot(_bf(merged), wo_ref[...])


def _merge_call(h, mods, n1, oa_ctx, oa_lat, gate, gn, ob_ctx, ob_lat, oc, od_ctx, od_lat, wmg, wb, wo, b64, *,
                layer, tiles_ctx, tiles_per_lat):
    t = h.shape[0]
    last_ctx = tiles_ctx - 1

    def mod_row(i):
        return jnp.where(i < tiles_ctx, 0, 1 + (i - tiles_ctx) // tiles_per_lat)

    def rows(width):
        return pl.BlockSpec((TM, width), lambda i: (i, 0))

    def ctx_rows(width):
        return pl.BlockSpec((TM, width), lambda i: (jnp.minimum(i, last_ctx), 0))

    def lat_rows(width):
        return pl.BlockSpec((TM, width), lambda i: (jnp.maximum(i - tiles_ctx, 0), 0))

    return pl.pallas_call(
        functools.partial(_merge_kernel, tiles_ctx=tiles_ctx),
        grid=(t // TM,),
        in_specs=[
            rows(D_MODEL),
            _mod_spec(layer, mod_row),
            _layer_spec((1, D_MODEL), layer),
            pl.BlockSpec((N_HEADS, TM, HEAD_W), lambda i: (0, jnp.minimum(i, last_ctx), 0)),
            pl.BlockSpec((N_HEADS, TM, HEAD_W), lambda i: (0, jnp.maximum(i - tiles_ctx, 0), 0)),
            rows(BRANCH_W),
            _layer_spec((1, BRANCH_W), layer),
            ctx_rows(BRANCH_W), lat_rows(BRANCH_W), rows(BRANCH_W), ctx_rows(BRANCH_W), lat_rows(BRANCH_W),
            _layer_spec((D_MODEL, 4 * D_MODEL), layer),
            _layer_spec((4, BRANCH_W, D_MODEL), layer),
            _layer_spec((D_MODEL, D_MODEL), layer),
            pl.BlockSpec((LANES, LANES), lambda i: (0, 0)),
        ],
        out_specs=rows(D_MODEL),
        out_shape=jax.ShapeDtypeStruct((t, D_MODEL), F32),
        compiler_params=pltpu.CompilerParams(dimension_semantics=("parallel",), vmem_limit_bytes=VMEM_LIMIT),
        name="merge",
    )(h, mods, n1, oa_ctx, oa_lat, gate, gn, ob_ctx, ob_lat, oc, od_ctx, od_lat, wmg, wb, wo, b64)


ROUTER_ROWS = 32
ROUTER_E0 = 8


def _moe_kernel(h_ref, mod_ref, n2_ref, wr_ref, br_ref, w1_ref, w3_ref, w2_ref, out_ref, xm_scr, gates_scr, acc_scr):
    e = pl.program_id(1)
    tm = h_ref.shape[0]

    @pl.when(e == 0)
    def _():
        xm = _norm_mod(h_ref[...], n2_ref[...], mod_ref[:, 4 * D_MODEL:5 * D_MODEL],
                       mod_ref[:, 3 * D_MODEL:4 * D_MODEL])
        xm_scr[...] = _bf(xm)
        lt = lax.dot_general(wr_ref[...], xm, (((1,), (1,)), ((), ())), precision=lax.Precision.HIGHEST,
                             preferred_element_type=F32) + br_ref[...]
        gl = lt[0:N_GROUPS]
        gidx = lax.broadcasted_iota(jnp.int32, (N_GROUPS, 1), 0)
        gmax = jnp.max(gl, axis=0, keepdims=True)
        gsel = jnp.min(jnp.where(gl == gmax, gidx, N_GROUPS), axis=0, keepdims=True)
        gprob = 1.0 / jnp.sum(jnp.exp(gl - gmax), axis=0, keepdims=True)
        el = lt[ROUTER_E0:ROUTER_E0 + N_EXPERTS]
        eidx = lax.broadcasted_iota(jnp.int32, (N_EXPERTS, 1), 0)
        neg = -jnp.inf
        cand = jnp.where(eidx // EXPERTS_PER_GROUP == gsel, el, neg)
        v1 = jnp.max(cand, axis=0, keepdims=True)
        i1 = jnp.min(jnp.where(cand == v1, eidx, N_EXPERTS), axis=0, keepdims=True)
        cand2 = jnp.where(eidx == i1, neg, cand)
        v2 = jnp.max(cand2, axis=0, keepdims=True)
        i2 = jnp.min(jnp.where(cand2 == v2, eidx, N_EXPERTS), axis=0, keepdims=True)
        ex = jnp.exp(v2 - v1)
        wa = 1.0 / (1.0 + ex)
        gates_t = (jnp.where(eidx == i1, wa, 0.0) + jnp.where(eidx == i2, ex * wa, 0.0)) * gprob
        full = jnp.concatenate([gates_t, jnp.zeros((LANES - N_EXPERTS, tm), F32)], axis=0)
        gates_scr[...] = full.T
        acc_scr[...] = jnp.zeros_like(acc_scr)

    xm = xm_scr[...]
    h1 = _dot(xm, w1_ref[...])
    h3 = _dot(xm, w3_ref[...])
    lane = lax.broadcasted_iota(jnp.int32, (1, LANES), 1)
    gate = jnp.sum(jnp.where(lane == e, gates_scr[...], 0.0), axis=-1, keepdims=True)
    hd = _silu(h1) * h3 * gate
    acc_scr[...] += _dot(_bf(hd), w2_ref[...])

    @pl.when(e == N_EXPERTS - 1)
    def _():
        out_ref[...] = h_ref[...] + mod_ref[:, 5 * D_MODEL:6 * D_MODEL] * acc_scr[...]


def _moe_call(h, mods, n2, wr, br, w1, w3, w2, *, layer, ctx_rows, lat_len):
    t = h.shape[0]
    tm = TM_MOE
    tiles_ctx = ctx_rows // tm
    tiles_per_lat = lat_len // tm

    def mod_row(i):
        return jnp.where(i < tiles_ctx, 0, 1 + (i - tiles_ctx) // tiles_per_lat)

    def expert(shape):
        return pl.BlockSpec((None, None) + shape, lambda i, e: (layer, e, 0, 0))

    return pl.pallas_call(
        _moe_kernel,
        grid=(t // tm, N_EXPERTS),
        in_specs=[
            pl.BlockSpec((tm, D_MODEL), lambda i, e: (i, 0)),
            _mod_spec(layer, mod_row),
            _layer_spec((1, D_MODEL), layer),
            _layer_spec((ROUTER_ROWS, D_MODEL), layer),
            _layer_spec((ROUTER_ROWS, 1), layer),
            expert((D_MODEL, D_EXPERT)),
            expert((D_MODEL, D_EXPERT)),
            expert((D_EXPERT, D_MODEL)),
        ],
        out_specs=pl.BlockSpec((tm, D_MODEL), lambda i, e: (i, 0)),
        out_shape=jax.ShapeDtypeStruct((t, D_MODEL), F32),
        scratch_shapes=[pltpu.VMEM((tm, D_MODEL), BF16), pltpu.VMEM((tm, LANES), F32),
                        pltpu.VMEM((tm, D_MODEL), F32)],
        compiler_params=pltpu.CompilerParams(dimension_semantics=("parallel", "arbitrary"),
                                             vmem_limit_bytes=VMEM_LIMIT),
        name="moe",
    )(h, mods, n2, wr, br, w1, w3, w2)


def _final_norm_kernel(h_ref, g_ref, o_ref):
    h = h_ref[...]
    o_ref[...] = h * lax.rsqrt(jnp.mean(h * h, axis=-1, keepdims=True) + EPS) * g_ref[...]


def _final_norm_call(h, g, *, row0, n_rows):
    blk0 = row0 // TM
    return pl.pallas_call(
        _final_norm_kernel,
        grid=(n_rows // TM,),
        in_specs=[pl.BlockSpec((TM, D_MODEL), lambda i: (blk0 + i, 0)), pl.BlockSpec((1, D_MODEL), lambda i: (0, 0))],
        out_specs=pl.BlockSpec((TM, D_MODEL), lambda i: (i, 0)),
        out_shape=jax.ShapeDtypeStruct((n_rows, D_MODEL), F32),
        compiler_params=pltpu.CompilerParams(dimension_semantics=("parallel",)),
        name="final_norm",
    )(h, g)


def _mixer_weights(w_in):
    depth, d, _ = w_in.shape
    ab_cols = []
    for h in range(N_HEADS):
        ab_cols += [_O_A + h, _O_A + N_HEADS + h, _O_B + h, _O_B + N_HEADS + h]
    ab = jnp.take(w_in[:, :, _O_A:_O_DQ], jnp.asarray([c - _O_A for c in ab_cols], dtype=jnp.int32), axis=2)
    parts = [w_in[:, :, _O_QKV:_O_A], ab, jnp.zeros((depth, d, LANES - len(ab_cols)), w_in.dtype),
             w_in[:, :, _O_DQ:_O_GQ]]
    for h in (0, 2, 1, 3):
        parts.append(w_in[:, :, _O_GQ + h * HEAD_W:_O_GQ + (h + 1) * HEAD_W])
    parts.append(w_in[:, :, _O_GK:_O_MG])
    w_mix = _bf(jnp.concatenate(parts, axis=2))
    assert w_mix.shape[2] == C_END
    return w_mix


def _rope_tables(lat_len):
    pos = jnp.arange(lat_len, dtype=jnp.int32)
    row = (pos // GRID_W).astype(F32)[:, None]
    col = (pos % GRID_W).astype(F32)[:, None]
    lane = jnp.arange(LANES, dtype=jnp.int32)[None, :]
    tabs = []
    for head_w in (DIFF_DQK, HEAD_W):
        half = head_w // 2
        nf = half // 2
        p = lane % head_w
        f = (p % half) % nf
        inv = ROPE_THETA ** (-f.astype(F32) / nf)
        ang = jnp.where(p < half, row, col) * inv
        sign = jnp.where((p % half) < nf, -1.0, 1.0)
        tabs += [jnp.cos(ang), jnp.sin(ang) * sign]
    tab = jnp.stack(tabs)
    ident = jnp.stack([jnp.ones((TM, LANES), F32), jnp.zeros((TM, LANES), F32)] * 2)
    return jnp.concatenate([ident, tab], axis=1)


def _chunk_tri(n, chunk, upper):
    i = jnp.arange(n)[:, None]
    j = jnp.arange(n)[None, :]
    same = (i // chunk) == (j // chunk)
    tri = (j >= i) if upper else (j <= i)
    return jnp.where(same & tri, 1.0, 0.0).astype(BF16)


def kernel(x_prompt, x_sample, state_gdn, cache_diff_k, cache_diff_v, cache_gqa_k, cache_gqa_v, c, c_ctx,
           ada_w, ada_b, norm1, norm2, w_in, gdn_conv, gdn_a_log, gdn_dt_bias, gdn_norm, diff_lambda, diff_norm,
           sgu_norm, sgu_ws, sgu_b, gqa_qnorm, gqa_knorm, w_branch, w_out, moe_wg, moe_bg, moe_we, moe_be,
           moe_w1, moe_w3, moe_w2, final_norm):
    nb, seq, d = x_prompt.shape
    nl, lat_len, _ = x_sample.shape
    depth = ada_w.shape[0]
    past = cache_diff_k.shape[2]
    ctx_rows = nb * seq
    lat_rows = nl * lat_len
    assert d == D_MODEL and seq % PREP_R == 0 and lat_len % TM == 0 and ctx_rows % lat_len == 0
    assert ctx_rows % TM == 0 and 1 + nl <= SUBLANES and lat_len % GRID_W == 0
    assert ctx_rows % TM_MOE == 0 and lat_len % TM_MOE == 0
    tiles_ctx = ctx_rows // TM
    tiles_per_lat = lat_len // TM
    tile_kw = dict(tiles_ctx=tiles_ctx, tiles_per_lat=tiles_per_lat)

    lane = jnp.arange(LANES)
    b64 = jnp.where((lane[:, None] // HEAD_W) == (lane[None, :] // HEAD_W), 1.0, 0.0).astype(BF16)
    tril = _chunk_tri(PREP_R, GDN_CHUNK, upper=False)
    triu = _chunk_tri(PREP_R, GDN_CHUNK, upper=True)
    rope_tab = _rope_tables(lat_len)

    w_mix = _mixer_weights(w_in)
    w_mg = _bf(w_in[:, :, _O_MG:])
    head_perm = jnp.asarray([r for h in (0, 2, 1, 3) for r in range(h * HEAD_W, (h + 1) * HEAD_W)], dtype=jnp.int32)
    wb = _bf(jnp.concatenate([w_branch[:, :3], jnp.take(w_branch[:, 3], head_perm, axis=1)[:, None]], axis=1))
    wo = _bf(w_out)
    w1, w3, w2 = _bf(moe_w1), _bf(moe_w3), _bf(moe_w2)
    wr = jnp.zeros((depth, ROUTER_ROWS, d), F32)
    wr = wr.at[:, 0:N_GROUPS].set(jnp.swapaxes(moe_wg, 1, 2))
    wr = wr.at[:, ROUTER_E0:ROUTER_E0 + N_EXPERTS].set(jnp.swapaxes(moe_we, 1, 2))
    br = jnp.zeros((depth, ROUTER_ROWS, 1), F32)
    br = br.at[:, 0:N_GROUPS, 0].set(moe_bg).at[:, ROUTER_E0:ROUTER_E0 + N_EXPERTS, 0].set(moe_be)
    conv_w = jnp.zeros((depth, SUBLANES, 3 * BRANCH_W), F32).at[:, :CONV_K].set(gdn_conv)
    ab_lane = jnp.zeros((depth, 1, LANES), F32)
    alog_row, dtb_row = ab_lane, ab_lane
    for h in range(N_HEADS):
        for dd in range(2):
            alog_row = alog_row.at[:, 0, 4 * h + dd].set(gdn_a_log[:, dd, h])
            dtb_row = dtb_row.at[:, 0, 4 * h + dd].set(gdn_dt_bias[:, dd, h])
    sgw = _bf(sgu_ws)
    sgb = jnp.repeat(jnp.swapaxes(sgu_b, 1, 2), HEAD_W, axis=2)
    gdn_gn = jnp.tile(gdn_norm, (1, N_HEADS))[:, None, :]
    diff_gn = jnp.tile(diff_norm, (1, N_HEADS))[:, None, :]
    gqn = jnp.tile(gqa_qnorm, (1, LANES // HEAD_W))[:, None, :]
    gkn = jnp.tile(gqa_knorm, (1, LANES // HEAD_W))[:, None, :]

    cond = jnp.zeros((SUBLANES, d), F32).at[0].set(c_ctx).at[1:1 + nl].set(c)
    mods = _ada_call(cond, ada_w, ada_b)[:, :, None, :]
    n1 = norm1[:, None, :]
    n2 = norm2[:, None, :]
    sgn = sgu_norm[:, None, :]

    h = jnp.concatenate([x_prompt.reshape(ctx_rows, d), x_sample.reshape(lat_rows, d)], axis=0)
    new_s, new_dk, new_dv, new_gk, new_gv = [], [], [], [], []
    for l in range(depth):
        lam_init = 0.8 - 0.6 * math.exp(-0.3 * l)
        (gqkv, ggate, gab, dq, dk32, dv32, dk16, dvt, oc, gq, gk32, gv32, gk16, gvt) = _inproj_call(
            h, mods, n1, w_mix, rope_tab, b64, sgn, sgw, sgb, gqn, gkn, layer=l, **tile_kw)

        qh, kh, vh, gb = _gdn_prep_call(gqkv, gab, conv_w, alog_row, dtb_row, b64, tril, triu, layer=l,
                                        ctx_rows=ctx_rows, ctx_len=seq, lat_len=lat_len)
        oa_ctx, s_ctx = _gdn_scan_call(qh, kh, vh, gb, None, row0=0, n_seq_total=nb, seq_len=seq,
                                       seqs_per_step=8 if nb % 8 == 0 else 1)
        oa_lat, _ = _gdn_scan_call(qh, kh, vh, gb, state_gdn[:, l], row0=ctx_rows, n_seq_total=nl,
                                   seq_len=lat_len, seqs_per_step=1)

        ob_ctx, od_ctx = _attn_ctx_call(dq, dk16, dvt, gq, gk16, gvt, diff_lambda, diff_gn, b64, layer=l,
                                        n_seq=nb, seq_len=seq, lam_init=lam_init)
        ob_lat, od_lat = _attn_lat_call(
            dq, dk16, dvt, gq, gk16, gvt,
            _bf(cache_diff_k[:, l].reshape(nl, past, BRANCH_W)),
            _bf(jnp.swapaxes(cache_diff_v[:, l].reshape(nl, past, BRANCH_W), 1, 2)),
            _bf(cache_gqa_k[:, l].reshape(nl, past, LANES)),
            _bf(jnp.swapaxes(cache_gqa_v[:, l].reshape(nl, past, LANES), 1, 2)),
            diff_lambda, diff_gn, b64, layer=l, row0=ctx_rows, n_seq=nl, seq_len=lat_len, lam_init=lam_init)

        h = _merge_call(h, mods, n1, oa_ctx, oa_lat, ggate, gdn_gn, ob_ctx, ob_lat, oc, od_ctx, od_lat,
                        w_mg, wb, wo, b64, layer=l, **tile_kw)
        h = _moe_call(h, mods, n2, wr, br, w1, w3, w2, layer=l, ctx_rows=ctx_rows, lat_len=lat_len)

        new_s.append(s_ctx)
        new_dk.append(dk32[:ctx_rows].reshape(nb, seq, N_HEADS, HEAD_W))
        new_dv.append(dv32[:ctx_rows].reshape(nb, seq, N_HEADS, HEAD_W))
        new_gk.append(gk32[:ctx_rows].reshape(nb, seq, GQA_KV_HEADS, HEAD_W))
        new_gv.append(gv32[:ctx_rows].reshape(nb, seq, GQA_KV_HEADS, HEAD_W))

    fg = final_norm[None, :]
    y_ctx = _final_norm_call(h, fg, row0=0, n_rows=ctx_rows)
    y_lat = _final_norm_call(h, fg, row0=ctx_rows, n_rows=lat_rows)
    return (y_ctx.reshape(nb, seq, d), y_lat.reshape(nl, lat_len, d),
            jnp.stack(new_s, axis=1), jnp.stack(new_dk, axis=1), jnp.stack(new_dv, axis=1),
            jnp.stack(new_gk, axis=1), jnp.stack(new_gv, axis=1))
```

```python
import functools
import math

import jax
import jax.numpy as jnp
from jax import lax
from jax.experimental import pallas as pl
from jax.experimental.pallas import tpu as pltpu

F32 = jnp.float32
BF16 = jnp.bfloat16

GRID_W = 64
ROPE_THETA = 10000.0
EPS = 1e-6
D_MODEL = 1024
BRANCH_W = 256
HEAD_W = 64
DIFF_DQK = 32
N_HEADS = 4
GQA_KV_HEADS = 2
GDN_CHUNK = 64
CONV_K = 5
SGU_CHUNK = 128
SGU_GROUPS = 4
N_GROUPS = 4
EXPERTS_PER_GROUP = 4
N_EXPERTS = 16
D_EXPERT = 256
LOG2E = 1.4426950408889634
DIFF_SCALE = DIFF_DQK ** -0.5 * LOG2E
GQA_SCALE = HEAD_W ** -0.5 * LOG2E
GDN_Q_SCALE = HEAD_W ** -0.5

LANES = 128
SUBLANES = 8
VMEM_LIMIT = 56 * 1024 * 1024

_IN_WIDTHS = (768, 256, 8, 8, 256, 256, 256, 256, 256, 256, 128, 128, 4096)
_IN_OFFS = tuple(sum(_IN_WIDTHS[:i]) for i in range(len(_IN_WIDTHS)))
(_O_QKV, _O_GATE, _O_A, _O_B, _O_DQ, _O_DK, _O_DV, _O_SU, _O_SV, _O_GQ, _O_GK, _O_GV, _O_MG) = _IN_OFFS
MIX_COLS = _O_MG

C_QKV, C_GATE, C_AB, C_DQ, C_DK, C_DV, C_SU, C_SV, C_GQ, C_GK, C_GV, C_END = (
    0, 768, 1024, 1152, 1408, 1664, 1920, 2176, 2432, 2688, 2816, 2944)

TM = 512
TM_MOE = 1024
PREP_R = 256
HALO = SUBLANES
SCAN_G = 4
PREP_PAR = 4
ATTN_TQ = 256


def _bf(x):
    return x.astype(BF16)


def _dot(a, b):
    return jnp.dot(a, b, preferred_element_type=F32)


def _dot_nt(a, b):
    return lax.dot_general(a, b, (((1,), (1,)), ((), ())), preferred_element_type=F32)


def _dot_tn(a, b):
    return lax.dot_general(a, b, (((0,), (0,)), ((), ())), preferred_element_type=F32)


def _split_dot(x, m, terms):
    acc = None
    r = x
    for t in range(terms):
        p = _bf(r)
        d = _dot(p, m)
        acc = d if acc is None else acc + d
        if t + 1 < terms:
            r = r - p.astype(F32)
    return acc


def _split_dot_left(m, x, terms):
    acc = None
    r = x
    for t in range(terms):
        p = _bf(r)
        d = _dot(m, p)
        acc = d if acc is None else acc + d
        if t + 1 < terms:
            r = r - p.astype(F32)
    return acc


def _sigmoid(x):
    return 1.0 / (1.0 + jnp.exp(-x))


def _silu(x):
    return x * _sigmoid(x)


def _norm_mod(h, gamma, scale, shift):
    ms = jnp.mean(h * h, axis=-1, keepdims=True)
    return (h * lax.rsqrt(ms + EPS) * gamma) * (1.0 + scale) + shift


def _group_sum(sq, b64):
    return _split_dot(sq, b64, 2)


def _rope(x, cos, sin_signed, nf):
    lane = lax.broadcasted_iota(jnp.int32, (1, LANES), 1)
    first = (lane % (2 * nf)) < nf
    partner = jnp.where(first, pltpu.roll(x, LANES - nf, 1), pltpu.roll(x, nf, 1))
    return x * cos + partner * sin_signed


def _layer_spec(shape, layer):
    return pl.BlockSpec((None,) + tuple(shape), lambda *_: (layer,) + (0,) * len(shape))


def _mod_spec(layer, mod_row):
    return pl.BlockSpec((None, None, 1, 6 * D_MODEL), lambda i, *_: (layer, mod_row(i), 0, 0))


def _ada_kernel(c_ref, w_ref, b_ref, o_ref):
    x = _silu(c_ref[...])
    o_ref[...] = jnp.dot(x, w_ref[...], precision=lax.Precision.HIGHEST,
                         preferred_element_type=F32) + b_ref[...]


def _ada_call(cond, ada_w, ada_b):
    n_layers, d, n_out = ada_w.shape
    tn = 1536
    return pl.pallas_call(
        _ada_kernel,
        grid=(n_layers, n_out // tn),
        in_specs=[
            pl.BlockSpec((SUBLANES, d), lambda l, j: (0, 0)),
            pl.BlockSpec((None, d, tn), lambda l, j: (l, 0, j)),
            pl.BlockSpec((None, 1, tn), lambda l, j: (l, 0, j)),
        ],
        out_specs=pl.BlockSpec((None, SUBLANES, tn), lambda l, j: (l, 0, j)),
        out_shape=jax.ShapeDtypeStruct((n_layers, SUBLANES, n_out), F32),
        compiler_params=pltpu.CompilerParams(vmem_limit_bytes=VMEM_LIMIT),
        name="ada_mod",
    )(cond, ada_w, ada_b.reshape(n_layers, 1, n_out))


def _inproj_kernel(h_ref, mod_ref, n1_ref, w_ref, rope_ref, b64_ref, sgn_ref, sgw_ref, sgb_ref, gqn_ref, gkn_ref,
                   gqkv_ref, ggate_ref, gab_ref, dqt_ref, dk32_ref, dv32_ref, dk16_ref, dvt_ref, oc_ref,
                   gqt_ref, gk32_ref, gv32_ref, gk16_ref, gvt_ref):
    tm = h_ref.shape[0]
    xm = _bf(_norm_mod(h_ref[...], n1_ref[...], mod_ref[:, D_MODEL:2 * D_MODEL], mod_ref[:, 0:D_MODEL]))

    def proj(a, b):
        return _dot(xm, w_ref[:, a:b])

    gqkv_ref[...] = proj(C_QKV, C_GATE)
    ggate_ref[...] = proj(C_GATE, C_AB)
    gab_ref[...] = proj(C_AB, C_DQ)

    cos_d, sin_d, cos_g, sin_g = rope_ref[0], rope_ref[1], rope_ref[2], rope_ref[3]
    b64 = b64_ref[...]

    for j in range(BRANCH_W // LANES):
        sl = slice(j * LANES, (j + 1) * LANES)
        q = proj(C_DQ + j * LANES, C_DQ + (j + 1) * LANES)
        dqt_ref[sl, :] = _bf((_rope(q, cos_d, sin_d, DIFF_DQK // 4) * DIFF_SCALE).T)
        k = _rope(proj(C_DK + j * LANES, C_DK + (j + 1) * LANES), cos_d, sin_d, DIFF_DQK // 4)
        dk32_ref[:, sl] = k
        dk16_ref[:, sl] = _bf(k)
    v = proj(C_DV, C_SU)
    dv32_ref[...] = v
    dvt_ref[...] = _bf(v.T)

    u = proj(C_SU, C_SV)
    sv = proj(C_SV, C_GQ)
    vn = _bf(sv * lax.rsqrt(jnp.mean(sv * sv, axis=-1, keepdims=True) + EPS) * sgn_ref[...])
    group = lax.broadcasted_iota(jnp.int32, (1, BRANCH_W), 1) // HEAD_W
    for r in range(tm // SGU_CHUNK):
        rows = slice(r * SGU_CHUNK, (r + 1) * SGU_CHUNK)
        vc = vn[rows]
        z = sgb_ref[...]
        for g in range(SGU_GROUPS):
            z = z + jnp.where(group == g, _dot(sgw_ref[g], vc), 0.0)
        oc_ref[rows, :] = _bf(u[rows] * z)

    for j in range(BRANCH_W // LANES):
        sl = slice(j * LANES, (j + 1) * LANES)
        q = proj(C_GQ + j * LANES, C_GQ + (j + 1) * LANES)
        qn = q * lax.rsqrt(_group_sum(q * q, b64) * (1.0 / HEAD_W) + EPS) * gqn_ref[...]
        gqt_ref[sl, :] = _bf((_rope(qn, cos_g, sin_g, HEAD_W // 4) * GQA_SCALE).T)
    k = proj(C_GK, C_GV)
    kn = k * lax.rsqrt(_group_sum(k * k, b64) * (1.0 / HEAD_W) + EPS) * gkn_ref[...]
    kr = _rope(kn, cos_g, sin_g, HEAD_W // 4)
    gk32_ref[...] = kr
    gk16_ref[...] = _bf(kr)
    v = proj(C_GV, C_END)
    gv32_ref[...] = v
    gvt_ref[...] = _bf(v.T)


def _inproj_call(h, mods, n1, w_mix, rope_tab, b64, sgn, sgw, sgb, gqn, gkn, *, layer, tiles_ctx, tiles_per_lat):
    t = h.shape[0]
    n_tiles = t // TM

    def mod_row(i):
        return jnp.where(i < tiles_ctx, 0, 1 + (i - tiles_ctx) // tiles_per_lat)

    def rope_blk(i):
        return jnp.where(i < tiles_ctx, 0, 1 + (i - tiles_ctx) % tiles_per_lat)

    def const(shape):
        return pl.BlockSpec(shape, lambda i: (0,) * len(shape))

    def rows(width):
        return pl.BlockSpec((TM, width), lambda i: (i, 0))

    def cols(height):
        return pl.BlockSpec((height, TM), lambda i: (0, i))

    row_major = lambda w, dt: (rows(w), jax.ShapeDtypeStruct((t, w), dt))
    col_major = lambda hgt, dt: (cols(hgt), jax.ShapeDtypeStruct((hgt, t), dt))
    outs = [
        row_major(768, F32), row_major(256, F32), row_major(128, F32),
        col_major(256, BF16), row_major(256, F32), row_major(256, F32),
        row_major(256, BF16), col_major(256, BF16),
        row_major(256, BF16),
        col_major(256, BF16), row_major(128, F32), row_major(128, F32),
        row_major(128, BF16), col_major(128, BF16),
    ]
    return pl.pallas_call(
        _inproj_kernel,
        grid=(n_tiles,),
        in_specs=[
            rows(D_MODEL),
            _mod_spec(layer, mod_row),
            _layer_spec((1, D_MODEL), layer),
            _layer_spec((D_MODEL, C_END), layer),
            pl.BlockSpec((4, TM, LANES), lambda i: (0, rope_blk(i), 0)),
            const((LANES, LANES)),
            _layer_spec((1, BRANCH_W), layer),
            _layer_spec((SGU_GROUPS, SGU_CHUNK, SGU_CHUNK), layer),
            _layer_spec((SGU_CHUNK, BRANCH_W), layer),
            _layer_spec((1, LANES), layer),
            _layer_spec((1, LANES), layer),
        ],
        out_specs=[spec for spec, _ in outs],
        out_shape=[shape for _, shape in outs],
        compiler_params=pltpu.CompilerParams(dimension_semantics=("parallel",), vmem_limit_bytes=VMEM_LIMIT),
        name="inproj",
    )(h, mods, n1, w_mix, rope_tab, b64, sgn, sgw, sgb, gqn, gkn)


def _gdn_prep_kernel(x_ref, xp_ref, xn_ref, ab_ref, cw_ref, alog_ref, dtb_ref, b64_ref, tril_ref, triu_ref,
                     q_ref, k_ref, v_ref, gb_ref, *, ctx_blocks, ctx_bps, lat_bps):
    r = x_ref.shape[0]
    i = pl.program_id(0)
    is_ctx = i < ctx_blocks
    pos = jnp.where(is_ctx, i % ctx_bps, (i - ctx_blocks) % lat_bps)
    bps = jnp.where(is_ctx, ctx_bps, lat_bps)
    prev = jnp.where(pos > 0, xp_ref[...], 0.0)
    nxt = jnp.where(pos < bps - 1, xn_ref[...], 0.0)
    win = jnp.concatenate([prev, x_ref[...], nxt], axis=0)
    n = r + 2 * HALO
    y = None
    for j in range(CONV_K):
        s = (CONV_K // 2 - j) % n
        shifted = win if s == 0 else pltpu.roll(win, s, 0)
        term = cw_ref[j:j + 1, :] * shifted[HALO:HALO + r]
        y = term if y is None else y + term
    y = _silu(y)

    b64 = b64_ref[...]
    for part, ref, scale in ((0, q_ref, GDN_Q_SCALE), (1, k_ref, 1.0)):
        for j in range(BRANCH_W // LANES):
            x = y[:, part * BRANCH_W + j * LANES: part * BRANCH_W + (j + 1) * LANES]
            xn = x * (lax.rsqrt(_group_sum(x * x, b64) + EPS) * scale)
            for hh in range(LANES // HEAD_W):
                ref[j * (LANES // HEAD_W) + hh] = xn[:, hh * HEAD_W:(hh + 1) * HEAD_W]
    for h in range(N_HEADS):
        v_ref[h] = y[:, 2 * BRANCH_W + h * HEAD_W: 2 * BRANCH_W + (h + 1) * HEAD_W]

    ab = ab_ref[...]
    lane = lax.broadcasted_iota(jnp.int32, (1, LANES), 1)
    lm = lane % 4
    xa = ab + dtb_ref[...]
    softplus = jnp.maximum(xa, 0.0) + jnp.log1p(jnp.exp(-jnp.abs(xa)))
    la = jnp.where((lm < 2) & (lane < 4 * N_HEADS), -jnp.exp(alog_ref[...]) * softplus, 0.0)
    g_fwd = _split_dot_left(tril_ref[...], la, 3)
    g_bwd = _split_dot_left(triu_ref[...], la, 3)
    gb = jnp.where(lm == 0, g_fwd, jnp.where(lm == 1, g_bwd, _sigmoid(ab)))
    for h in range(N_HEADS):
        gb_ref[h] = gb if h == 0 else pltpu.roll(gb, LANES - 4 * h, 1)


def _gdn_prep_call(gqkv, gab, conv_w, alog_row, dtb_row, b64, tril, triu, *, layer, ctx_rows, ctx_len, lat_len):
    t = gqkv.shape[0]
    r = PREP_R
    n_blocks = t // r
    rh = r // HALO
    last_halo = t // HALO - 1
    kern = functools.partial(_gdn_prep_kernel, ctx_blocks=ctx_rows // r, ctx_bps=ctx_len // r, lat_bps=lat_len // r)

    def const(shape):
        return pl.BlockSpec(shape, lambda i: (0,) * len(shape))

    head_major = pl.BlockSpec((N_HEADS, r, HEAD_W), lambda i: (0, i, 0))
    return pl.pallas_call(
        kern,
        grid=(n_blocks,),
        in_specs=[
            pl.BlockSpec((r, 3 * BRANCH_W), lambda i: (i, 0)),
            pl.BlockSpec((HALO, 3 * BRANCH_W), lambda i: (jnp.maximum(i * rh - 1, 0), 0)),
            pl.BlockSpec((HALO, 3 * BRANCH_W), lambda i: (jnp.minimum((i + 1) * rh, last_halo), 0)),
            pl.BlockSpec((r, LANES), lambda i: (i, 0)),
            _layer_spec((SUBLANES, 3 * BRANCH_W), layer),
            _layer_spec((1, LANES), layer),
            _layer_spec((1, LANES), layer),
            const((LANES, LANES)),
            const((r, r)),
            const((r, r)),
        ],
        out_specs=[head_major, head_major, head_major,
                   pl.BlockSpec((N_HEADS, r, LANES), lambda i: (0, i, 0))],
        out_shape=[jax.ShapeDtypeStruct((N_HEADS, t, HEAD_W), F32)] * 3
        + [jax.ShapeDtypeStruct((N_HEADS, t, LANES), F32)],
        compiler_params=pltpu.CompilerParams(dimension_semantics=("parallel",), vmem_limit_bytes=VMEM_LIMIT),
        name="gdn_prep",
    )(gqkv, gqkv, gqkv, gab, conv_w, alog_row, dtb_row, b64, tril, triu)


def _gdn_scan_kernel(*refs, n_seq, seq_len, has_s0):
    if has_s0:
        q_ref, k_ref, v_ref, gb_ref, s0_ref, o_ref, s_ref, oi_scr, qg_scr, m_scr, b_scr = refs
    else:
        q_ref, k_ref, v_ref, gb_ref, o_ref, s_ref, oi_scr, qg_scr, m_scr, b_scr = refs
        s0_ref = None
    c = GDN_CHUNK
    gr = SCAN_G * c
    n_rows = n_seq * seq_len
    nc = seq_len // c
    row_i = lax.broadcasted_iota(jnp.int32, (gr, gr), 0)
    col_i = lax.broadcasted_iota(jnp.int32, (gr, gr), 1)
    eye = jnp.where(row_i == col_i, 1.0, 0.0)
    same = (row_i // c) == (col_i // c)
    incl = (same & (col_i <= row_i), same & (col_i >= row_i))
    strict = (same & (col_i < row_i), same & (col_i > row_i))
    blk16 = (row_i // 16) == (col_i // 16)
    blk32 = ((row_i // 32) == (col_i // 32)) & ~blk16
    blk64 = same & ((row_i // 32) != (col_i // 32))

    def prep_groups(gi, carry):
        probs = []
        for j in range(PREP_PAR):
            r0 = pl.multiple_of((gi * PREP_PAR + j) * gr, gr)
            rows = pl.ds(r0, gr)
            q = q_ref[rows, :]
            k = k_ref[rows, :]
            v = v_ref[rows, :]
            gbc = gb_ref[rows, :]
            gbt = gbc.T
            kb = _bf(k)
            kk = _dot_nt(kb, kb)
            qk = _dot_nt(_bf(q), kb)
            for d in range(2):
                gcol = gbc[:, d:d + 1]
                bcol = gbc[:, 2 + d:3 + d]
                dec = jnp.where(incl[d], jnp.exp(jnp.where(incl[d], gcol - gbt[d:d + 1, :], 0.0)), 0.0)
                probs.append(dict(d=d, r0=r0, rows=rows, q=q, k=k, v=v, gcol=gcol, bcol=bcol, dec=dec,
                                  qkm=_bf(qk * dec), nmat=jnp.where(strict[d], bcol * kk * dec, 0.0)))
        ps = [jnp.where(blk16, pr["nmat"], 0.0) for pr in probs]
        ts = [eye - p for p in ps]
        for _ in range(3):
            pbs = [_bf(p) for p in ps]
            ps = [_dot(pb, pb) for pb in pbs]
            ts = [t + _dot(_bf(t), _bf(p)) for t, p in zip(ts, ps)]
        for level in (blk32, blk64):
            tbs = [_bf(t) for t in ts]
            xs = [_dot(tb, _bf(jnp.where(level, pr["nmat"], 0.0))) for tb, pr in zip(tbs, probs)]
            ts = [t - _dot(_bf(x), tb) for t, x, tb in zip(ts, xs, tbs)]
        ainvs = [_bf(t) for t in ts]
        egs = [jnp.exp(pr["gcol"]) for pr in probs]
        rhs = [_bf(jnp.concatenate([pr["bcol"] * pr["v"], (pr["bcol"] * eg) * pr["k"]], axis=-1))
               for pr, eg in zip(probs, egs)]
        uwks = [_bf(_dot(ai, r)) for ai, r in zip(ainvs, rhs)]
        intra = [_dot(pr["qkm"], uwk) for pr, uwk in zip(probs, uwks)]
        for pr, eg, uwk, it in zip(probs, egs, uwks, intra):
            d, r0, rows, gcol = pr["d"], pr["r0"], pr["rows"], pr["gcol"]
            oi_scr[d, rows, :] = it[:, :HEAD_W]
            qg_scr[d, rows, :] = _bf(pr["q"] * eg - it[:, HEAD_W:])
            glast = jnp.concatenate(
                [jnp.broadcast_to(gcol[j * c + (c - 1 if d == 0 else 0):j * c + (c if d == 0 else 1), :], (c, 1))
                 for j in range(SCAN_G)], axis=0)
            kdb = _bf(pr["k"] * jnp.exp(glast - gcol))
            for j in range(SCAN_G):
                cs = slice(j * c, (j + 1) * c)
                crow = pl.ds(r0 + j * c, c)
                kd_uwk = _dot_tn(kdb[cs], uwk[cs])
                b_scr[d, crow, :] = kd_uwk[:, :HEAD_W]
                m_scr[d, crow, :] = _bf(kd_uwk[:, HEAD_W:])
        return carry

    lax.fori_loop(0, n_rows // (gr * PREP_PAR), prep_groups, 0)

    def scan_step(s, d, ci, state):
        r0 = s * seq_len + ci * c
        if not isinstance(r0, int):
            r0 = pl.multiple_of(r0, c)
        rows = pl.ds(r0, c)
        sb = _bf(state)
        oi_scr[d, rows, :] = oi_scr[d, rows, :] + _dot(qg_scr[d, rows, :], sb)
        last = r0 + (c - 1) if d == 0 else r0
        egl = jnp.exp(gb_ref[pl.ds(last, 1), d:d + 1])
        return egl * state - _dot(m_scr[d, rows, :], sb) + b_scr[d, rows, :]

    def init_state(s, d):
        return s0_ref[s, d] if has_s0 else jnp.zeros((c, c), F32)

    chains = [(s, d) for s in range(n_seq) for d in range(2)]
    states = [init_state(s, d) for s, d in chains]
    if nc <= 8:
        for t in range(nc):
            states = [scan_step(s, d, t if d == 0 else nc - 1 - t, st) for (s, d), st in zip(chains, states)]
    else:
        def body(t, sts):
            return tuple(scan_step(s, d, t if d == 0 else nc - 1 - t, st) for (s, d), st in zip(chains, sts))
        states = lax.fori_loop(0, nc, body, tuple(states))
    for (s, d), st in zip(chains, states):
        s_ref[s, d] = st
    o_ref[...] = oi_scr[0] + oi_scr[1]


def _gdn_scan_call(q, k, v, gb, s0, *, row0, n_seq_total, seq_len, seqs_per_step):
    n_rows = seqs_per_step * seq_len
    assert n_rows % (SCAN_G * GDN_CHUNK * PREP_PAR) == 0 and row0 % n_rows == 0
    blk0 = row0 // n_rows
    has_s0 = s0 is not None
    kern = functools.partial(_gdn_scan_kernel, n_seq=seqs_per_step, seq_len=seq_len, has_s0=has_s0)
    hm = pl.BlockSpec((None, n_rows, HEAD_W), lambda s, h: (h, blk0 + s, 0))
    st_spec = pl.BlockSpec((seqs_per_step, 2, None, HEAD_W, HEAD_W), lambda s, h: (s, 0, h, 0, 0))
    in_specs = [hm, hm, hm, pl.BlockSpec((None, n_rows, LANES), lambda s, h: (h, blk0 + s, 0))]
    args = [q, k, v, gb]
    if has_s0:
        in_specs.append(st_spec)
        args.append(s0)
    return pl.pallas_call(
        kern,
        grid=(n_seq_total // seqs_per_step, N_HEADS),
        in_specs=in_specs,
        out_specs=[pl.BlockSpec((None, n_rows, HEAD_W), lambda s, h: (h, s, 0)), st_spec],
        out_shape=[jax.ShapeDtypeStruct((N_HEADS, n_seq_total * seq_len, HEAD_W), F32),
                   jax.ShapeDtypeStruct((n_seq_total, 2, N_HEADS, HEAD_W, HEAD_W), F32)],
        scratch_shapes=[
            pltpu.VMEM((2, n_rows, HEAD_W), F32),
            pltpu.VMEM((2, n_rows, HEAD_W), BF16),
            pltpu.VMEM((2, n_rows, HEAD_W), BF16),
            pltpu.VMEM((2, n_rows, HEAD_W), F32),
        ],
        compiler_params=pltpu.CompilerParams(dimension_semantics=("parallel", "parallel"),
                                             vmem_limit_bytes=VMEM_LIMIT),
        name="gdn_scan",
    )(*args)


ONES_ROWS = 16


KEY_CHUNK = 1024


def _attend_all(tasks):
    outs = []
    sts = [_dot(k, tasks[0][0]) for k in tasks[0][1]]
    for i, (_, _, vts) in enumerate(tasks):
        mx = None
        for st in sts:
            m = jnp.max(st, axis=0, keepdims=True)
            mx = m if mx is None else jnp.maximum(mx, m)
        nxt = []
        acc = None
        for c, (st, vt) in enumerate(zip(sts, vts)):
            if i + 1 < len(tasks):
                nxt.append(_dot(tasks[i + 1][1][c], tasks[i + 1][0]))
            t = _dot(vt, _bf(jnp.exp2(st - mx)))
            acc = t if acc is None else acc + t
        outs.append(acc[:HEAD_W] * (1.0 / acc[HEAD_W:HEAD_W + 1]))
        sts = nxt
    return outs


def _key_pieces(k_ref, cache_ref):
    n = k_ref.shape[0]
    kc = min(KEY_CHUNK, n)
    own = [k_ref[c * kc:(c + 1) * kc, :] for c in range(n // kc)]
    return own if cache_ref is None else [cache_ref[...]] + own


def _value_pieces(vt_ref, cache_ref, rows):
    def with_ones(x):
        return jnp.concatenate([x, jnp.ones((ONES_ROWS, x.shape[1]), x.dtype)], axis=0)

    n = vt_ref.shape[1]
    kc = min(KEY_CHUNK, n)
    own = [with_ones(vt_ref[rows, c * kc:(c + 1) * kc]) for c in range(n // kc)]
    return own if cache_ref is None else [with_ones(cache_ref[rows, :])] + own


def _attn_kernel(*refs, has_cache, lam_init):
    if has_cache:
        (dqt_ref, dk_ref, dvt_ref, gqt_ref, gk_ref, gvt_ref, cdk_ref, cdvt_ref, cgk_ref, cgvt_ref,
         lp_ref, dn_ref, b64_ref, ob_ref, od_ref) = refs
    else:
        dqt_ref, dk_ref, dvt_ref, gqt_ref, gk_ref, gvt_ref, lp_ref, dn_ref, b64_ref, ob_ref, od_ref = refs
        cdk_ref = cdvt_ref = cgk_ref = cgvt_ref = None

    lp = lp_ref[...]
    lam = (jnp.exp(jnp.sum(lp[0:1] * lp[1:2], axis=-1, keepdims=True))
           - jnp.exp(jnp.sum(lp[2:3] * lp[3:4], axis=-1, keepdims=True)) + lam_init)

    chan = lax.broadcasted_iota(jnp.int32, (BRANCH_W, 1), 0)
    chan_h = lax.broadcasted_iota(jnp.int32, (LANES, 1), 0)
    dqt = dqt_ref[...]
    dks = _key_pieces(dk_ref, cdk_ref)
    gks = _key_pieces(gk_ref, cgk_ref)
    tasks = []
    for h in range(N_HEADS):
        vts = _value_pieces(dvt_ref, cdvt_ref, slice(h * HEAD_W, (h + 1) * HEAD_W))
        for m in range(2):
            lo = h * HEAD_W + m * DIFF_DQK
            tasks.append((dqt * jnp.where((chan >= lo) & (chan < lo + DIFF_DQK), 1.0, 0.0).astype(BF16), dks, vts))
    kv_vts = [_value_pieces(gvt_ref, cgvt_ref, slice(g * HEAD_W, (g + 1) * HEAD_W)) for g in range(GQA_KV_HEADS)]
    for half in range(2):
        qh = gqt_ref[half * LANES:(half + 1) * LANES, :]
        for g in range(GQA_KV_HEADS):
            tasks.append((qh * jnp.where(chan_h // HEAD_W == g, 1.0, 0.0).astype(BF16), gks, kv_vts[g]))
    outs = _attend_all(tasks)

    o = jnp.concatenate([outs[2 * h] - lam * outs[2 * h + 1] for h in range(N_HEADS)], axis=0).T
    b64 = b64_ref[...]
    for j in range(BRANCH_W // LANES):
        sl = slice(j * LANES, (j + 1) * LANES)
        x = o[:, sl]
        ms = _group_sum(x * x, b64) * (1.0 / HEAD_W)
        ob_ref[:, sl] = _bf(x * lax.rsqrt(ms + EPS) * dn_ref[:, sl] * (1.0 - lam_init))
    od_ref[...] = _bf(jnp.concatenate(outs[2 * N_HEADS:], axis=0).T)


def _attn_ctx_call(dqt, dk, dvt, gqt, gk, gvt, lp, dn, b64, *, layer, n_seq, seq_len, lam_init):
    kern = functools.partial(_attn_kernel, has_cache=False, lam_init=lam_init)

    def rows(width):
        return pl.BlockSpec((seq_len, width), lambda s: (s, 0))

    def cols(height):
        return pl.BlockSpec((height, seq_len), lambda s: (0, s))

    def const(shape):
        return pl.BlockSpec(shape, lambda s: (0,) * len(shape))

    return pl.pallas_call(
        kern,
        grid=(n_seq,),
        in_specs=[cols(256), rows(256), cols(256), cols(256), rows(128), cols(128),
                  _layer_spec((4, DIFF_DQK), layer), _layer_spec((1, BRANCH_W), layer), const((LANES, LANES))],
        out_specs=[rows(256), rows(256)],
        out_shape=[jax.ShapeDtypeStruct((n_seq * seq_len, BRANCH_W), BF16)] * 2,
        compiler_params=pltpu.CompilerParams(dimension_semantics=("parallel",), vmem_limit_bytes=VMEM_LIMIT),
        name="attn_ctx",
    )(dqt, dk, dvt, gqt, gk, gvt, lp, dn, b64)


def _attn_lat_call(dqt, dk, dvt, gqt, gk, gvt, cdk, cdvt, cgk, cgvt, lp, dn, b64, *, layer, row0, n_seq, seq_len,
                   lam_init):
    kern = functools.partial(_attn_kernel, has_cache=True, lam_init=lam_init)
    tq = ATTN_TQ
    qpb = seq_len // tq
    q0 = row0 // tq
    s0 = row0 // seq_len
    past = cdk.shape[1]

    def qcols(height):
        return pl.BlockSpec((height, tq), lambda b, i: (0, q0 + b * qpb + i))

    def kvrows(width):
        return pl.BlockSpec((seq_len, width), lambda b, i: (s0 + b, 0))

    def kvcols(height):
        return pl.BlockSpec((height, seq_len), lambda b, i: (0, s0 + b))

    def cache(width):
        return pl.BlockSpec((None, past, width), lambda b, i: (b, 0, 0))

    def cache_t(height):
        return pl.BlockSpec((None, height, past), lambda b, i: (b, 0, 0))

    def const(shape):
        return pl.BlockSpec(shape, lambda b, i: (0,) * len(shape))

    out = pl.BlockSpec((tq, BRANCH_W), lambda b, i: (b * qpb + i, 0))
    return pl.pallas_call(
        kern,
        grid=(n_seq, qpb),
        in_specs=[qcols(256), kvrows(256), kvcols(256), qcols(256), kvrows(128), kvcols(128),
                  cache(256), cache_t(256), cache(128), cache_t(128),
                  _layer_spec((4, DIFF_DQK), layer), _layer_spec((1, BRANCH_W), layer), const((LANES, LANES))],
        out_specs=[out, out],
        out_shape=[jax.ShapeDtypeStruct((n_seq * seq_len, BRANCH_W), BF16)] * 2,
        compiler_params=pltpu.CompilerParams(dimension_semantics=("parallel", "arbitrary"),
                                             vmem_limit_bytes=VMEM_LIMIT),
        name="attn_lat",
    )(dqt, dk, dvt, gqt, gk, gvt, cdk, cdvt, cgk, cgvt, lp, dn, b64)


def _merge_kernel(h_ref, mod_ref, n1_ref, oac_ref, oal_ref, gate_ref, gn_ref, obc_ref, obl_ref, oc_ref,
                  odc_ref, odl_ref, wmg_ref, wb_ref, wo_ref, b64_ref, out_ref, *, tiles_ctx):
    is_ctx = pl.program_id(0) < tiles_ctx
    h = h_ref[...]
    xm = _bf(_norm_mod(h, n1_ref[...], mod_ref[:, D_MODEL:2 * D_MODEL], mod_ref[:, 0:D_MODEL]))
    g1 = mod_ref[:, 2 * D_MODEL:3 * D_MODEL]
    b64 = b64_ref[...]
    oa_parts = []
    for j in range(BRANCH_W // LANES):
        sl = slice(j * LANES, (j + 1) * LANES)
        x = jnp.concatenate([jnp.where(is_ctx, oac_ref[2 * j + hh], oal_ref[2 * j + hh])
                             for hh in range(LANES // HEAD_W)], axis=-1)
        ms = _group_sum(x * x, b64) * (1.0 / HEAD_W)
        oa_parts.append(_bf(x * lax.rsqrt(ms + EPS) * gn_ref[:, sl] * _silu(gate_ref[:, sl])))
    branches = (None, jnp.where(is_ctx, obc_ref[...], obl_ref[...]), oc_ref[...],
                jnp.where(is_ctx, odc_ref[...], odl_ref[...]))
    merged = None
    for n in range(4):
        mg = _dot(xm, wmg_ref[:, n * D_MODEL:(n + 1) * D_MODEL])
        if n == 0:
            up = _dot(oa_parts[0], wb_ref[0, 0:LANES, :]) + _dot(oa_parts[1], wb_ref[0, LANES:2 * LANES, :])
        else:
            up = _dot(branches[n], wb_ref[n])
        term = _sigmoid(mg) * up
        merged = term if merged is None else merged + term
    out_ref[...] = h + g1 * _dot(_bf(merged), wo_ref[...])


def _merge_call(h, mods, n1, oa_ctx, oa_lat, gate, gn, ob_ctx, ob_lat, oc, od_ctx, od_lat, wmg, wb, wo, b64, *,
                layer, tiles_ctx, tiles_per_lat):
    t = h.shape[0]
    last_ctx = tiles_ctx - 1

    def mod_row(i):
        return jnp.where(i < tiles_ctx, 0, 1 + (i - tiles_ctx) // tiles_per_lat)

    def rows(width):
        return pl.BlockSpec((TM, width), lambda i: (i, 0))

    def ctx_rows(width):
        return pl.BlockSpec((TM, width), lambda i: (jnp.minimum(i, last_ctx), 0))

    def lat_rows(width):
        return pl.BlockSpec((TM, width), lambda i: (jnp.maximum(i - tiles_ctx, 0), 0))

    return pl.pallas_call(
        functools.partial(_merge_kernel, tiles_ctx=tiles_ctx),
        grid=(t // TM,),
        in_specs=[
            rows(D_MODEL),
            _mod_spec(layer, mod_row),
            _layer_spec((1, D_MODEL), layer),
            pl.BlockSpec((N_HEADS, TM, HEAD_W), lambda i: (0, jnp.minimum(i, last_ctx), 0)),
            pl.BlockSpec((N_HEADS, TM, HEAD_W), lambda i: (0, jnp.maximum(i - tiles_ctx, 0), 0)),
            rows(BRANCH_W),
            _layer_spec((1, BRANCH_W), layer),
            ctx_rows(BRANCH_W), lat_rows(BRANCH_W), rows(BRANCH_W), ctx_rows(BRANCH_W), lat_rows(BRANCH_W),
            _layer_spec((D_MODEL, 4 * D_MODEL), layer),
            _layer_spec((4, BRANCH_W, D_MODEL), layer),
            _layer_spec((D_MODEL, D_MODEL), layer),
            pl.BlockSpec((LANES, LANES), lambda i: (0, 0)),
        ],
        out_specs=rows(D_MODEL),
        out_shape=jax.ShapeDtypeStruct((t, D_MODEL), F32),
        compiler_params=pltpu.CompilerParams(dimension_semantics=("parallel",), vmem_limit_bytes=VMEM_LIMIT),
        name="merge",
    )(h, mods, n1, oa_ctx, oa_lat, gate, gn, ob_ctx, ob_lat, oc, od_ctx, od_lat, wmg, wb, wo, b64)


ROUTER_ROWS = 32
ROUTER_E0 = 8


def _moe_kernel(h_ref, mod_ref, n2_ref, wr_ref, br_ref, w1_ref, w3_ref, w2_ref, out_ref, xm_scr, gates_scr, acc_scr):
    e = pl.program_id(1)
    tm = h_ref.shape[0]

    @pl.when(e == 0)
    def _():
        xm = _norm_mod(h_ref[...], n2_ref[...], mod_ref[:, 4 * D_MODEL:5 * D_MODEL],
                       mod_ref[:, 3 * D_MODEL:4 * D_MODEL])
        xm_scr[...] = _bf(xm)
        lt = lax.dot_general(wr_ref[...], xm, (((1,), (1,)), ((), ())), precision=lax.Precision.HIGHEST,
                             preferred_element_type=F32) + br_ref[...]
        gl = lt[0:N_GROUPS]
        gidx = lax.broadcasted_iota(jnp.int32, (N_GROUPS, 1), 0)
        gmax = jnp.max(gl, axis=0, keepdims=True)
        gsel = jnp.min(jnp.where(gl == gmax, gidx, N_GROUPS), axis=0, keepdims=True)
        gprob = 1.0 / jnp.sum(jnp.exp(gl - gmax), axis=0, keepdims=True)
        el = lt[ROUTER_E0:ROUTER_E0 + N_EXPERTS]
        eidx = lax.broadcasted_iota(jnp.int32, (N_EXPERTS, 1), 0)
        neg = -jnp.inf
        cand = jnp.where(eidx // EXPERTS_PER_GROUP == gsel, el, neg)
        v1 = jnp.max(cand, axis=0, keepdims=True)
        i1 = jnp.min(jnp.where(cand == v1, eidx, N_EXPERTS), axis=0, keepdims=True)
        cand2 = jnp.where(eidx == i1, neg, cand)
        v2 = jnp.max(cand2, axis=0, keepdims=True)
        i2 = jnp.min(jnp.where(cand2 == v2, eidx, N_EXPERTS), axis=0, keepdims=True)
        ex = jnp.exp(v2 - v1)
        wa = 1.0 / (1.0 + ex)
        gates_t = (jnp.where(eidx == i1, wa, 0.0) + jnp.where(eidx == i2, ex * wa, 0.0)) * gprob
        full = jnp.concatenate([gates_t, jnp.zeros((LANES - N_EXPERTS, tm), F32)], axis=0)
        gates_scr[...] = full.T
        acc_scr[...] = jnp.zeros_like(acc_scr)

    xm = xm_scr[...]
    h1 = _dot(xm, w1_ref[...])
    h3 = _dot(xm, w3_ref[...])
    lane = lax.broadcasted_iota(jnp.int32, (1, LANES), 1)
    gate = jnp.sum(jnp.where(lane == e, gates_scr[...], 0.0), axis=-1, keepdims=True)
    hd = _silu(h1) * h3 * gate
    acc_scr[...] += _dot(_bf(hd), w2_ref[...])

    @pl.when(e == N_EXPERTS - 1)
    def _():
        out_ref[...] = h_ref[...] + mod_ref[:, 5 * D_MODEL:6 * D_MODEL] * acc_scr[...]


def _moe_call(h, mods, n2, wr, br, w1, w3, w2, *, layer, ctx_rows, lat_len):
    t = h.shape[0]
    tm = TM_MOE
    tiles_ctx = ctx_rows // tm
    tiles_per_lat = lat_len // tm

    def mod_row(i):
        return jnp.where(i < tiles_ctx, 0, 1 + (i - tiles_ctx) // tiles_per_lat)

    def expert(shape):
        return pl.BlockSpec((None, None) + shape, lambda i, e: (layer, e, 0, 0))

    return pl.pallas_call(
        _moe_kernel,
        grid=(t // tm, N_EXPERTS),
        in_specs=[
            pl.BlockSpec((tm, D_MODEL), lambda i, e: (i, 0)),
            _mod_spec(layer, mod_row),
            _layer_spec((1, D_MODEL), layer),
            _layer_spec((ROUTER_ROWS, D_MODEL), layer),
            _layer_spec((ROUTER_ROWS, 1), layer),
            expert((D_MODEL, D_EXPERT)),
            expert((D_MODEL, D_EXPERT)),
            expert((D_EXPERT, D_MODEL)),
        ],
        out_specs=pl.BlockSpec((tm, D_MODEL), lambda i, e: (i, 0)),
        out_shape=jax.ShapeDtypeStruct((t, D_MODEL), F32),
        scratch_shapes=[pltpu.VMEM((tm, D_MODEL), BF16), pltpu.VMEM((tm, LANES), F32),
                        pltpu.VMEM((tm, D_MODEL), F32)],
        compiler_params=pltpu.CompilerParams(dimension_semantics=("parallel", "arbitrary"),
                                             vmem_limit_bytes=VMEM_LIMIT),
        name="moe",
    )(h, mods, n2, wr, br, w1, w3, w2)


def _final_norm_kernel(h_ref, g_ref, o_ref):
    h = h_ref[...]
    o_ref[...] = h * lax.rsqrt(jnp.mean(h * h, axis=-1, keepdims=True) + EPS) * g_ref[...]


def _final_norm_call(h, g, *, row0, n_rows):
    blk0 = row0 // TM
    return pl.pallas_call(
        _final_norm_kernel,
        grid=(n_rows // TM,),
        in_specs=[pl.BlockSpec((TM, D_MODEL), lambda i: (blk0 + i, 0)), pl.BlockSpec((1, D_MODEL), lambda i: (0, 0))],
        out_specs=pl.BlockSpec((TM, D_MODEL), lambda i: (i, 0)),
        out_shape=jax.ShapeDtypeStruct((n_rows, D_MODEL), F32),
        compiler_params=pltpu.CompilerParams(dimension_semantics=("parallel",)),
        name="final_norm",
    )(h, g)


def _mixer_weights(w_in):
    depth, d, _ = w_in.shape
    ab_cols = []
    for h in range(N_HEADS):
        ab_cols += [_O_A + h, _O_A + N_HEADS + h, _O_B + h, _O_B + N_HEADS + h]
    ab = jnp.take(w_in[:, :, _O_A:_O_DQ], jnp.asarray([c - _O_A for c in ab_cols], dtype=jnp.int32), axis=2)
    parts = [w_in[:, :, _O_QKV:_O_A], ab, jnp.zeros((depth, d, LANES - len(ab_cols)), w_in.dtype),
             w_in[:, :, _O_DQ:_O_GQ]]
    for h in (0, 2, 1, 3):
        parts.append(w_in[:, :, _O_GQ + h * HEAD_W:_O_GQ + (h + 1) * HEAD_W])
    parts.append(w_in[:, :, _O_GK:_O_MG])
    w_mix = _bf(jnp.concatenate(parts, axis=2))
    assert w_mix.shape[2] == C_END
    return w_mix


def _rope_tables(lat_len):
    pos = jnp.arange(lat_len, dtype=jnp.int32)
    row = (pos // GRID_W).astype(F32)[:, None]
    col = (pos % GRID_W).astype(F32)[:, None]
    lane = jnp.arange(LANES, dtype=jnp.int32)[None, :]
    tabs = []
    for head_w in (DIFF_DQK, HEAD_W):
        half = head_w // 2
        nf = half // 2
        p = lane % head_w
        f = (p % half) % nf
        inv = ROPE_THETA ** (-f.astype(F32) / nf)
        ang = jnp.where(p < half, row, col) * inv
        sign = jnp.where((p % half) < nf, -1.0, 1.0)
        tabs += [jnp.cos(ang), jnp.sin(ang) * sign]
    tab = jnp.stack(tabs)
    ident = jnp.stack([jnp.ones((TM, LANES), F32), jnp.zeros((TM, LANES), F32)] * 2)
    return jnp.concatenate([ident, tab], axis=1)


def _chunk_tri(n, chunk, upper):
    i = jnp.arange(n)[:, None]
    j = jnp.arange(n)[None, :]
    same = (i // chunk) == (j // chunk)
    tri = (j >= i) if upper else (j <= i)
    return jnp.where(same & tri, 1.0, 0.0).astype(BF16)


def kernel(x_prompt, x_sample, state_gdn, cache_diff_k, cache_diff_v, cache_gqa_k, cache_gqa_v, c, c_ctx,
           ada_w, ada_b, norm1, norm2, w_in, gdn_conv, gdn_a_log, gdn_dt_bias, gdn_norm, diff_lambda, diff_norm,
           sgu_norm, sgu_ws, sgu_b, gqa_qnorm, gqa_knorm, w_branch, w_out, moe_wg, moe_bg, moe_we, moe_be,
           moe_w1, moe_w3, moe_w2, final_norm):
    nb, seq, d = x_prompt.shape
    nl, lat_len, _ = x_sample.shape
    depth = ada_w.shape[0]
    past = cache_diff_k.shape[2]
    ctx_rows = nb * seq
    lat_rows = nl * lat_len
    assert d == D_MODEL and seq % PREP_R == 0 and lat_len % TM == 0 and ctx_rows % lat_len == 0
    assert ctx_rows % TM == 0 and 1 + nl <= SUBLANES and lat_len % GRID_W == 0
    assert ctx_rows % TM_MOE == 0 and lat_len % TM_MOE == 0
    tiles_ctx = ctx_rows // TM
    tiles_per_lat = lat_len // TM
    tile_kw = dict(tiles_ctx=tiles_ctx, tiles_per_lat=tiles_per_lat)

    lane = jnp.arange(LANES)
    b64 = jnp.where((lane[:, None] // HEAD_W) == (lane[None, :] // HEAD_W), 1.0, 0.0).astype(BF16)
    tril = _chunk_tri(PREP_R, GDN_CHUNK, upper=False)
    triu = _chunk_tri(PREP_R, GDN_CHUNK, upper=True)
    rope_tab = _rope_tables(lat_len)

    w_mix = _mixer_weights(w_in)
    w_mg = _bf(w_in[:, :, _O_MG:])
    head_perm = jnp.asarray([r for h in (0, 2, 1, 3) for r in range(h * HEAD_W, (h + 1) * HEAD_W)], dtype=jnp.int32)
    wb = _bf(jnp.concatenate([w_branch[:, :3], jnp.take(w_branch[:, 3], head_perm, axis=1)[:, None]], axis=1))
    wo = _bf(w_out)
    w1, w3, w2 = _bf(moe_w1), _bf(moe_w3), _bf(moe_w2)
    wr = jnp.zeros((depth, ROUTER_ROWS, d), F32)
    wr = wr.at[:, 0:N_GROUPS].set(jnp.swapaxes(moe_wg, 1, 2))
    wr = wr.at[:, ROUTER_E0:ROUTER_E0 + N_EXPERTS].set(jnp.swapaxes(moe_we, 1, 2))
    br = jnp.zeros((depth, ROUTER_ROWS, 1), F32)
    br = br.at[:, 0:N_GROUPS, 0].set(moe_bg).at[:, ROUTER_E0:ROUTER_E0 + N_EXPERTS, 0].set(moe_be)
    conv_w = jnp.zeros((depth, SUBLANES, 3 * BRANCH_W), F32).at[:, :CONV_K].set(gdn_conv)
    ab_lane = jnp.zeros((depth, 1, LANES), F32)
    alog_row, dtb_row = ab_lane, ab_lane
    for h in range(N_HEADS):
        for dd in range(2):
            alog_row = alog_row.at[:, 0, 4 * h + dd].set(gdn_a_log[:, dd, h])
            dtb_row = dtb_row.at[:, 0, 4 * h + dd].set(gdn_dt_bias[:, dd, h])
    sgw = _bf(sgu_ws)
    sgb = jnp.repeat(jnp.swapaxes(sgu_b, 1, 2), HEAD_W, axis=2)
    gdn_gn = jnp.tile(gdn_norm, (1, N_HEADS))[:, None, :]
    diff_gn = jnp.tile(diff_norm, (1, N_HEADS))[:, None, :]
    gqn = jnp.tile(gqa_qnorm, (1, LANES // HEAD_W))[:, None, :]
    gkn = jnp.tile(gqa_knorm, (1, LANES // HEAD_W))[:, None, :]

    cond = jnp.zeros((SUBLANES, d), F32).at[0].set(c_ctx).at[1:1 + nl].set(c)
    mods = _ada_call(cond, ada_w, ada_b)[:, :, None, :]
    n1 = norm1[:, None, :]
    n2 = norm2[:, None, :]
    sgn = sgu_norm[:, None, :]

    h = jnp.concatenate([x_prompt.reshape(ctx_rows, d), x_sample.reshape(lat_rows, d)], axis=0)
    new_s, new_dk, new_dv, new_gk, new_gv = [], [], [], [], []
    for l in range(depth):
        lam_init = 0.8 - 0.6 * math.exp(-0.3 * l)
        (gqkv, ggate, gab, dqt, dk32, dv32, dk16, dvt, oc, gqt, gk32, gv32, gk16, gvt) = _inproj_call(
            h, mods, n1, w_mix, rope_tab, b64, sgn, sgw, sgb, gqn, gkn, layer=l, **tile_kw)

        qh, kh, vh, gb = _gdn_prep_call(gqkv, gab, conv_w, alog_row, dtb_row, b64, tril, triu, layer=l,
                                        ctx_rows=ctx_rows, ctx_len=seq, lat_len=lat_len)
        oa_ctx, s_ctx = _gdn_scan_call(qh, kh, vh, gb, None, row0=0, n_seq_total=nb, seq_len=seq,
                                       seqs_per_step=8 if nb % 8 == 0 else 1)
        oa_lat, _ = _gdn_scan_call(qh, kh, vh, gb, state_gdn[:, l], row0=ctx_rows, n_seq_total=nl,
                                   seq_len=lat_len, seqs_per_step=1)

        ob_ctx, od_ctx = _attn_ctx_call(dqt, dk16, dvt, gqt, gk16, gvt, diff_lambda, diff_gn, b64, layer=l,
                                        n_seq=nb, seq_len=seq, lam_init=lam_init)
        ob_lat, od_lat = _attn_lat_call(
            dqt, dk16, dvt, gqt, gk16, gvt,
            _bf(cache_diff_k[:, l].reshape(nl, past, BRANCH_W)),
            _bf(jnp.swapaxes(cache_diff_v[:, l].reshape(nl, past, BRANCH_W), 1, 2)),
            _bf(cache_gqa_k[:, l].reshape(nl, past, LANES)),
            _bf(jnp.swapaxes(cache_gqa_v[:, l].reshape(nl, past, LANES), 1, 2)),
            diff_lambda, diff_gn, b64, layer=l, row0=ctx_rows, n_seq=nl, seq_len=lat_len, lam_init=lam_init)

        h = _merge_call(h, mods, n1, oa_ctx, oa_lat, ggate, gdn_gn, ob_ctx, ob_lat, oc, od_ctx, od_lat,
                        w_mg, wb, wo, b64, layer=l, **tile_kw)
        h = _moe_call(h, mods, n2, wr, br, w1, w3, w2, layer=l, ctx_rows=ctx_rows, lat_len=lat_len)

        new_s.append(s_ctx)
        new_dk.append(dk32[:ctx_rows].reshape(nb, seq, N_HEADS, HEAD_W))
        new_dv.append(dv32[:ctx_rows].reshape(nb, seq, N_HEADS, HEAD_W))
        new_gk.append(gk32[:ctx_rows].reshape(nb, seq, GQA_KV_HEADS, HEAD_W))
        new_gv.append(gv32[:ctx_rows].reshape(nb, seq, GQA_KV_HEADS, HEAD_W))

    fg = final_norm[None, :]
    y_ctx = _final_norm_call(h, fg, row0=0, n_rows=ctx_rows)
    y_lat = _final_norm_call(h, fg, row0=ctx_rows, n_rows=lat_rows)
    return (y_ctx.reshape(nb, seq, d), y_lat.reshape(nl, lat_len, d),
            jnp.stack(new_s, axis=1), jnp.stack(new_dk, axis=1), jnp.stack(new_dv, axis=1),
            jnp.stack(new_gk, axis=1), jnp.stack(new_gv, axis=1))
```

```python
import functools
import math

import jax
import jax.numpy as jnp
from jax import lax
from jax.experimental import pallas as pl
from jax.experimental.pallas import tpu as pltpu

F32 = jnp.float32
BF16 = jnp.bfloat16

GRID_W = 64
ROPE_THETA = 10000.0
EPS = 1e-6
D_MODEL = 1024
BRANCH_W = 256
HEAD_W = 64
DIFF_DQK = 32
N_HEADS = 4
GQA_KV_HEADS = 2
GDN_CHUNK = 64
CONV_K = 5
SGU_CHUNK = 128
SGU_GROUPS = 4
N_GROUPS = 4
EXPERTS_PER_GROUP = 4
N_EXPERTS = 16
D_EXPERT = 256
LOG2E = 1.4426950408889634
DIFF_SCALE = DIFF_DQK ** -0.5 * LOG2E
GQA_SCALE = HEAD_W ** -0.5 * LOG2E
GDN_Q_SCALE = HEAD_W ** -0.5

LANES = 128
SUBLANES = 8
VMEM_LIMIT = 56 * 1024 * 1024

_IN_WIDTHS = (768, 256, 8, 8, 256, 256, 256, 256, 256, 256, 128, 128, 4096)
_IN_OFFS = tuple(sum(_IN_WIDTHS[:i]) for i in range(len(_IN_WIDTHS)))
(_O_QKV, _O_GATE, _O_A, _O_B, _O_DQ, _O_DK, _O_DV, _O_SU, _O_SV, _O_GQ, _O_GK, _O_GV, _O_MG) = _IN_OFFS
MIX_COLS = _O_MG

C_QKV, C_GATE, C_AB, C_DQ, C_DK, C_DV, C_SU, C_SV, C_GQ, C_GK, C_GV, C_END = (
    0, 768, 1024, 1152, 1408, 1664, 1920, 2176, 2432, 2688, 2816, 2944)

TM = 512
TM_MOE = 1024
PREP_R = 256
HALO = SUBLANES
SCAN_G = 4
PREP_PAR = 4
ATTN_TQ = 256


def _bf(x):
    return x.astype(BF16)


def _dot(a, b):
    return jnp.dot(a, b, preferred_element_type=F32)


def _dot_nt(a, b):
    return lax.dot_general(a, b, (((1,), (1,)), ((), ())), preferred_element_type=F32)


def _dot_tn(a, b):
    return lax.dot_general(a, b, (((0,), (0,)), ((), ())), preferred_element_type=F32)


def _split_dot(x, m, terms):
    acc = None
    r = x
    for t in range(terms):
        p = _bf(r)
        d = _dot(p, m)
        acc = d if acc is None else acc + d
        if t + 1 < terms:
            r = r - p.astype(F32)
    return acc


def _split_dot_left(m, x, terms):
    acc = None
    r = x
    for t in range(terms):
        p = _bf(r)
        d = _dot(m, p)
        acc = d if acc is None else acc + d
        if t + 1 < terms:
            r = r - p.astype(F32)
    return acc


def _sigmoid(x):
    return 1.0 / (1.0 + jnp.exp(-x))


def _silu(x):
    return x * _sigmoid(x)


def _norm_mod(h, gamma, scale, shift):
    ms = jnp.mean(h * h, axis=-1, keepdims=True)
    return (h * lax.rsqrt(ms + EPS) * gamma) * (1.0 + scale) + shift


def _group_sum(sq, b64):
    return _split_dot(sq, b64, 2)


def _rope(x, cos, sin_signed, nf):
    lane = lax.broadcasted_iota(jnp.int32, (1, LANES), 1)
    first = (lane % (2 * nf)) < nf
    partner = jnp.where(first, pltpu.roll(x, LANES - nf, 1), pltpu.roll(x, nf, 1))
    return x * cos + partner * sin_signed


def _layer_spec(shape, layer):
    return pl.BlockSpec((None,) + tuple(shape), lambda *_: (layer,) + (0,) * len(shape))


def _mod_spec(layer, mod_row):
    return pl.BlockSpec((None, None, 1, 6 * D_MODEL), lambda i, *_: (layer, mod_row(i), 0, 0))


def _ada_kernel(c_ref, w_ref, b_ref, o_ref):
    x = _silu(c_ref[...])
    o_ref[...] = jnp.dot(x, w_ref[...], precision=lax.Precision.HIGHEST,
                         preferred_element_type=F32) + b_ref[...]


def _ada_call(cond, ada_w, ada_b):
    n_layers, d, n_out = ada_w.shape
    tn = 1536
    return pl.pallas_call(
        _ada_kernel,
        grid=(n_layers, n_out // tn),
        in_specs=[
            pl.BlockSpec((SUBLANES, d), lambda l, j: (0, 0)),
            pl.BlockSpec((None, d, tn), lambda l, j: (l, 0, j)),
            pl.BlockSpec((None, 1, tn), lambda l, j: (l, 0, j)),
        ],
        out_specs=pl.BlockSpec((None, SUBLANES, tn), lambda l, j: (l, 0, j)),
        out_shape=jax.ShapeDtypeStruct((n_layers, SUBLANES, n_out), F32),
        compiler_params=pltpu.CompilerParams(vmem_limit_bytes=VMEM_LIMIT),
        name="ada_mod",
    )(cond, ada_w, ada_b.reshape(n_layers, 1, n_out))


def _inproj_kernel(h_ref, mod_ref, n1_ref, w_ref, rope_ref, b64_ref, sgn_ref, sgw_ref, sgb_ref, gqn_ref, gkn_ref,
                   gqkv_ref, ggate_ref, gab_ref, dqt_ref, dk32_ref, dv32_ref, dk16_ref, dvt_ref, oc_ref,
                   gqt_ref, gk32_ref, gv32_ref, gk16_ref, gvt_ref):
    tm = h_ref.shape[0]
    xm = _bf(_norm_mod(h_ref[...], n1_ref[...], mod_ref[:, D_MODEL:2 * D_MODEL], mod_ref[:, 0:D_MODEL]))

    def proj(a, b):
        return _dot(xm, w_ref[:, a:b])

    gqkv_ref[...] = proj(C_QKV, C_GATE)
    ggate_ref[...] = proj(C_GATE, C_AB)
    gab_ref[...] = proj(C_AB, C_DQ)

    cos_d, sin_d, cos_g, sin_g = rope_ref[0], rope_ref[1], rope_ref[2], rope_ref[3]
    b64 = b64_ref[...]

    for j in range(BRANCH_W // LANES):
        sl = slice(j * LANES, (j + 1) * LANES)
        q = proj(C_DQ + j * LANES, C_DQ + (j + 1) * LANES)
        dqt_ref[sl, :] = _bf((_rope(q, cos_d, sin_d, DIFF_DQK // 4) * DIFF_SCALE).T)
        k = _rope(proj(C_DK + j * LANES, C_DK + (j + 1) * LANES), cos_d, sin_d, DIFF_DQK // 4)
        dk32_ref[:, sl] = k
        dk16_ref[:, sl] = _bf(k)
    v = proj(C_DV, C_SU)
    dv32_ref[...] = v
    dvt_ref[...] = _bf(v.T)

    u = proj(C_SU, C_SV)
    sv = proj(C_SV, C_GQ)
    vn = _bf(sv * lax.rsqrt(jnp.mean(sv * sv, axis=-1, keepdims=True) + EPS) * sgn_ref[...])
    group = lax.broadcasted_iota(jnp.int32, (1, BRANCH_W), 1) // HEAD_W
    for r in range(tm // SGU_CHUNK):
        rows = slice(r * SGU_CHUNK, (r + 1) * SGU_CHUNK)
        vc = vn[rows]
        z = sgb_ref[...]
        for g in range(SGU_GROUPS):
            z = z + jnp.where(group == g, _dot(sgw_ref[g], vc), 0.0)
        oc_ref[rows, :] = _bf(u[rows] * z)

    for j in range(BRANCH_W // LANES):
        sl = slice(j * LANES, (j + 1) * LANES)
        q = proj(C_GQ + j * LANES, C_GQ + (j + 1) * LANES)
        qn = q * lax.rsqrt(_group_sum(q * q, b64) * (1.0 / HEAD_W) + EPS) * gqn_ref[...]
        gqt_ref[sl, :] = _bf((_rope(qn, cos_g, sin_g, HEAD_W // 4) * GQA_SCALE).T)
    k = proj(C_GK, C_GV)
    kn = k * lax.rsqrt(_group_sum(k * k, b64) * (1.0 / HEAD_W) + EPS) * gkn_ref[...]
    kr = _rope(kn, cos_g, sin_g, HEAD_W // 4)
    gk32_ref[...] = kr
    gk16_ref[...] = _bf(kr)
    v = proj(C_GV, C_END)
    gv32_ref[...] = v
    gvt_ref[...] = _bf(v.T)


def _inproj_call(h, mods, n1, w_mix, rope_tab, b64, sgn, sgw, sgb, gqn, gkn, *, layer, tiles_ctx, tiles_per_lat):
    t = h.shape[0]
    n_tiles = t // TM

    def mod_row(i):
        return jnp.where(i < tiles_ctx, 0, 1 + (i - tiles_ctx) // tiles_per_lat)

    def rope_blk(i):
        return jnp.where(i < tiles_ctx, 0, 1 + (i - tiles_ctx) % tiles_per_lat)

    def const(shape):
        return pl.BlockSpec(shape, lambda i: (0,) * len(shape))

    def rows(width):
        return pl.BlockSpec((TM, width), lambda i: (i, 0))

    def cols(height):
        return pl.BlockSpec((height, TM), lambda i: (0, i))

    row_major = lambda w, dt: (rows(w), jax.ShapeDtypeStruct((t, w), dt))
    col_major = lambda hgt, dt: (cols(hgt), jax.ShapeDtypeStruct((hgt, t), dt))
    outs = [
        row_major(768, F32), row_major(256, F32), row_major(128, F32),
        col_major(256, BF16), row_major(256, F32), row_major(256, F32),
        row_major(256, BF16), col_major(256, BF16),
        row_major(256, BF16),
        col_major(256, BF16), row_major(128, F32), row_major(128, F32),
        row_major(128, BF16), col_major(128, BF16),
    ]
    return pl.pallas_call(
        _inproj_kernel,
        grid=(n_tiles,),
        in_specs=[
            rows(D_MODEL),
            _mod_spec(layer, mod_row),
            _layer_spec((1, D_MODEL), layer),
            _layer_spec((D_MODEL, C_END), layer),
            pl.BlockSpec((4, TM, LANES), lambda i: (0, rope_blk(i), 0)),
            const((LANES, LANES)),
            _layer_spec((1, BRANCH_W), layer),
            _layer_spec((SGU_GROUPS, SGU_CHUNK, SGU_CHUNK), layer),
            _layer_spec((SGU_CHUNK, BRANCH_W), layer),
            _layer_spec((1, LANES), layer),
            _layer_spec((1, LANES), layer),
        ],
        out_specs=[spec for spec, _ in outs],
        out_shape=[shape for _, shape in outs],
        compiler_params=pltpu.CompilerParams(dimension_semantics=("parallel",), vmem_limit_bytes=VMEM_LIMIT),
        name="inproj",
    )(h, mods, n1, w_mix, rope_tab, b64, sgn, sgw, sgb, gqn, gkn)


def _gdn_prep_kernel(x_ref, xp_ref, xn_ref, ab_ref, cw_ref, alog_ref, dtb_ref, b64_ref, tril_ref, triu_ref,
                     q_ref, k_ref, v_ref, gb_ref, *, ctx_blocks, ctx_bps, lat_bps):
    r = x_ref.shape[0]
    i = pl.program_id(0)
    is_ctx = i < ctx_blocks
    pos = jnp.where(is_ctx, i % ctx_bps, (i - ctx_blocks) % lat_bps)
    bps = jnp.where(is_ctx, ctx_bps, lat_bps)
    prev = jnp.where(pos > 0, xp_ref[...], 0.0)
    nxt = jnp.where(pos < bps - 1, xn_ref[...], 0.0)
    win = jnp.concatenate([prev, x_ref[...], nxt], axis=0)
    n = r + 2 * HALO
    y = None
    for j in range(CONV_K):
        s = (CONV_K // 2 - j) % n
        shifted = win if s == 0 else pltpu.roll(win, s, 0)
        term = cw_ref[j:j + 1, :] * shifted[HALO:HALO + r]
        y = term if y is None else y + term
    y = _silu(y)

    b64 = b64_ref[...]
    for part, ref, scale in ((0, q_ref, GDN_Q_SCALE), (1, k_ref, 1.0)):
        for j in range(BRANCH_W // LANES):
            x = y[:, part * BRANCH_W + j * LANES: part * BRANCH_W + (j + 1) * LANES]
            xn = x * (lax.rsqrt(_group_sum(x * x, b64) + EPS) * scale)
            for hh in range(LANES // HEAD_W):
                ref[j * (LANES // HEAD_W) + hh] = xn[:, hh * HEAD_W:(hh + 1) * HEAD_W]
    for h in range(N_HEADS):
        v_ref[h] = y[:, 2 * BRANCH_W + h * HEAD_W: 2 * BRANCH_W + (h + 1) * HEAD_W]

    ab = ab_ref[...]
    lane = lax.broadcasted_iota(jnp.int32, (1, LANES), 1)
    lm = lane % 4
    xa = ab + dtb_ref[...]
    softplus = jnp.maximum(xa, 0.0) + jnp.log1p(jnp.exp(-jnp.abs(xa)))
    la = jnp.where((lm < 2) & (lane < 4 * N_HEADS), -jnp.exp(alog_ref[...]) * softplus, 0.0)
    g_fwd = _split_dot_left(tril_ref[...], la, 3)
    g_bwd = _split_dot_left(triu_ref[...], la, 3)
    gb = jnp.where(lm == 0, g_fwd, jnp.where(lm == 1, g_bwd, _sigmoid(ab)))
    for h in range(N_HEADS):
        gb_ref[h] = gb if h == 0 else pltpu.roll(gb, LANES - 4 * h, 1)


def _gdn_prep_call(gqkv, gab, conv_w, alog_row, dtb_row, b64, tril, triu, *, layer, ctx_rows, ctx_len, lat_len):
    t = gqkv.shape[0]
    r = PREP_R
    n_blocks = t // r
    rh = r // HALO
    last_halo = t // HALO - 1
    kern = functools.partial(_gdn_prep_kernel, ctx_blocks=ctx_rows // r, ctx_bps=ctx_len // r, lat_bps=lat_len // r)

    def const(shape):
        return pl.BlockSpec(shape, lambda i: (0,) * len(shape))

    head_major = pl.BlockSpec((N_HEADS, r, HEAD_W), lambda i: (0, i, 0))
    return pl.pallas_call(
        kern,
        grid=(n_blocks,),
        in_specs=[
            pl.BlockSpec((r, 3 * BRANCH_W), lambda i: (i, 0)),
            pl.BlockSpec((HALO, 3 * BRANCH_W), lambda i: (jnp.maximum(i * rh - 1, 0), 0)),
            pl.BlockSpec((HALO, 3 * BRANCH_W), lambda i: (jnp.minimum((i + 1) * rh, last_halo), 0)),
            pl.BlockSpec((r, LANES), lambda i: (i, 0)),
            _layer_spec((SUBLANES, 3 * BRANCH_W), layer),
            _layer_spec((1, LANES), layer),
            _layer_spec((1, LANES), layer),
            const((LANES, LANES)),
            const((r, r)),
            const((r, r)),
        ],
        out_specs=[head_major, head_major, head_major,
                   pl.BlockSpec((N_HEADS, r, LANES), lambda i: (0, i, 0))],
        out_shape=[jax.ShapeDtypeStruct((N_HEADS, t, HEAD_W), F32)] * 3
        + [jax.ShapeDtypeStruct((N_HEADS, t, LANES), F32)],
        compiler_params=pltpu.CompilerParams(dimension_semantics=("parallel",), vmem_limit_bytes=VMEM_LIMIT),
        name="gdn_prep",
    )(gqkv, gqkv, gqkv, gab, conv_w, alog_row, dtb_row, b64, tril, triu)


def _gdn_scan_kernel(*refs, n_seq, seq_len, has_s0):
    if has_s0:
        q_ref, k_ref, v_ref, gb_ref, s0_ref, o_ref, s_ref, oi_scr, qg_scr, m_scr, b_scr = refs
    else:
        q_ref, k_ref, v_ref, gb_ref, o_ref, s_ref, oi_scr, qg_scr, m_scr, b_scr = refs
        s0_ref = None
    c = GDN_CHUNK
    gr = SCAN_G * c
    n_rows = n_seq * seq_len
    nc = seq_len // c
    row_i = lax.broadcasted_iota(jnp.int32, (gr, gr), 0)
    col_i = lax.broadcasted_iota(jnp.int32, (gr, gr), 1)
    eye = jnp.where(row_i == col_i, 1.0, 0.0)
    same = (row_i // c) == (col_i // c)
    incl = (same & (col_i <= row_i), same & (col_i >= row_i))
    strict = (same & (col_i < row_i), same & (col_i > row_i))
    blk16 = (row_i // 16) == (col_i // 16)
    blk32 = ((row_i // 32) == (col_i // 32)) & ~blk16
    blk64 = same & ((row_i // 32) != (col_i // 32))

    def prep_groups(gi, carry):
        probs = []
        for j in range(PREP_PAR):
            r0 = pl.multiple_of((gi * PREP_PAR + j) * gr, gr)
            rows = pl.ds(r0, gr)
            q = q_ref[rows, :]
            k = k_ref[rows, :]
            v = v_ref[rows, :]
            gbc = gb_ref[rows, :]
            gbt = gbc.T
            kb = _bf(k)
            kk = _dot_nt(kb, kb)
            qk = _dot_nt(_bf(q), kb)
            for d in range(2):
                gcol = gbc[:, d:d + 1]
                bcol = gbc[:, 2 + d:3 + d]
                dec = jnp.where(incl[d], jnp.exp(jnp.where(incl[d], gcol - gbt[d:d + 1, :], 0.0)), 0.0)
                probs.append(dict(d=d, r0=r0, rows=rows, q=q, k=k, v=v, gcol=gcol, bcol=bcol, dec=dec,
                                  qkm=_bf(qk * dec), nmat=jnp.where(strict[d], bcol * kk * dec, 0.0)))
        ps = [jnp.where(blk16, pr["nmat"], 0.0) for pr in probs]
        ts = [eye - p for p in ps]
        for _ in range(3):
            pbs = [_bf(p) for p in ps]
            ps = [_dot(pb, pb) for pb in pbs]
            ts = [t + _dot(_bf(t), _bf(p)) for t, p in zip(ts, ps)]
        for level in (blk32, blk64):
            tbs = [_bf(t) for t in ts]
            xs = [_dot(tb, _bf(jnp.where(level, pr["nmat"], 0.0))) for tb, pr in zip(tbs, probs)]
            ts = [t - _dot(_bf(x), tb) for t, x, tb in zip(ts, xs, tbs)]
        ainvs = [_bf(t) for t in ts]
        egs = [jnp.exp(pr["gcol"]) for pr in probs]
        rhs = [_bf(jnp.concatenate([pr["bcol"] * pr["v"], (pr["bcol"] * eg) * pr["k"]], axis=-1))
               for pr, eg in zip(probs, egs)]
        uwks = [_bf(_dot(ai, r)) for ai, r in zip(ainvs, rhs)]
        intra = [_dot(pr["qkm"], uwk) for pr, uwk in zip(probs, uwks)]
        for pr, eg, uwk, it in zip(probs, egs, uwks, intra):
            d, r0, rows, gcol = pr["d"], pr["r0"], pr["rows"], pr["gcol"]
            oi_scr[d, rows, :] = it[:, :HEAD_W]
            qg_scr[d, rows, :] = _bf(pr["q"] * eg - it[:, HEAD_W:])
            glast = jnp.concatenate(
                [jnp.broadcast_to(gcol[j * c + (c - 1 if d == 0 else 0):j * c + (c if d == 0 else 1), :], (c, 1))
                 for j in range(SCAN_G)], axis=0)
            kdb = _bf(pr["k"] * jnp.exp(glast - gcol))
            for j in range(SCAN_G):
                cs = slice(j * c, (j + 1) * c)
                crow = pl.ds(r0 + j * c, c)
                kd_uwk = _dot_tn(kdb[cs], uwk[cs])
                b_scr[d, crow, :] = kd_uwk[:, :HEAD_W]
                m_scr[d, crow, :] = _bf(kd_uwk[:, HEAD_W:])
        return carry

    lax.fori_loop(0, n_rows // (gr * PREP_PAR), prep_groups, 0)

    def scan_step(s, d, ci, state):
        r0 = s * seq_len + ci * c
        if not isinstance(r0, int):
            r0 = pl.multiple_of(r0, c)
        rows = pl.ds(r0, c)
        sb = _bf(state)
        oi_scr[d, rows, :] = oi_scr[d, rows, :] + _dot(qg_scr[d, rows, :], sb)
        last = r0 + (c - 1) if d == 0 else r0
        egl = jnp.exp(gb_ref[pl.ds(last, 1), d:d + 1])
        return egl * state - _dot(m_scr[d, rows, :], sb) + b_scr[d, rows, :]

    def init_state(s, d):
        return s0_ref[s, d] if has_s0 else jnp.zeros((c, c), F32)

    chains = [(s, d) for s in range(n_seq) for d in range(2)]
    states = [init_state(s, d) for s, d in chains]
    if nc <= 8:
        for t in range(nc):
            states = [scan_step(s, d, t if d == 0 else nc - 1 - t, st) for (s, d), st in zip(chains, states)]
    else:
        def body(t, sts):
            return tuple(scan_step(s, d, t if d == 0 else nc - 1 - t, st) for (s, d), st in zip(chains, sts))
        states = lax.fori_loop(0, nc, body, tuple(states))
    for (s, d), st in zip(chains, states):
        s_ref[s, d] = st
    o_ref[...] = oi_scr[0] + oi_scr[1]


def _gdn_scan_call(q, k, v, gb, s0, *, row0, n_seq_total, seq_len, seqs_per_step):
    n_rows = seqs_per_step * seq_len
    assert n_rows % (SCAN_G * GDN_CHUNK * PREP_PAR) == 0 and row0 % n_rows == 0
    blk0 = row0 // n_rows
    has_s0 = s0 is not None
    kern = functools.partial(_gdn_scan_kernel, n_seq=seqs_per_step, seq_len=seq_len, has_s0=has_s0)
    hm = pl.BlockSpec((None, n_rows, HEAD_W), lambda s, h: (h, blk0 + s, 0))
    st_spec = pl.BlockSpec((seqs_per_step, 2, None, HEAD_W, HEAD_W), lambda s, h: (s, 0, h, 0, 0))
    in_specs = [hm, hm, hm, pl.BlockSpec((None, n_rows, LANES), lambda s, h: (h, blk0 + s, 0))]
    args = [q, k, v, gb]
    if has_s0:
        in_specs.append(st_spec)
        args.append(s0)
    return pl.pallas_call(
        kern,
        grid=(n_seq_total // seqs_per_step, N_HEADS),
        in_specs=in_specs,
        out_specs=[pl.BlockSpec((None, n_rows, HEAD_W), lambda s, h: (h, s, 0)), st_spec],
        out_shape=[jax.ShapeDtypeStruct((N_HEADS, n_seq_total * seq_len, HEAD_W), F32),
                   jax.ShapeDtypeStruct((n_seq_total, 2, N_HEADS, HEAD_W, HEAD_W), F32)],
        scratch_shapes=[
            pltpu.VMEM((2, n_rows, HEAD_W), F32),
            pltpu.VMEM((2, n_rows, HEAD_W), BF16),
            pltpu.VMEM((2, n_rows, HEAD_W), BF16),
            pltpu.VMEM((2, n_rows, HEAD_W), F32),
        ],
        compiler_params=pltpu.CompilerParams(dimension_semantics=("parallel", "parallel"),
                                             vmem_limit_bytes=VMEM_LIMIT),
        name="gdn_scan",
    )(*args)


ONES_ROWS = 16


KEY_CHUNK = 1024


def _attend_all(tasks):
    outs = []
    sts = [_dot(k, tasks[0][0]) for k in tasks[0][1]]
    for i, (_, _, vts) in enumerate(tasks):
        mx = None
        for st in sts:
            m = jnp.max(st, axis=0, keepdims=True)
            mx = m if mx is None else jnp.maximum(mx, m)
        nxt = []
        acc = None
        for c, (st, vt) in enumerate(zip(sts, vts)):
            if i + 1 < len(tasks):
                nxt.append(_dot(tasks[i + 1][1][c], tasks[i + 1][0]))
            t = _dot(vt, _bf(jnp.exp2(st - mx)))
            acc = t if acc is None else acc + t
        outs.append(acc[:HEAD_W] * (1.0 / acc[HEAD_W:HEAD_W + 1]))
        sts = nxt
    return outs


def _key_pieces(k_ref, cache_ref):
    n = k_ref.shape[0]
    kc = min(KEY_CHUNK, n)
    own = [k_ref[c * kc:(c + 1) * kc, :] for c in range(n // kc)]
    return own if cache_ref is None else [cache_ref[...]] + own


def _value_pieces(vt_ref, cache_ref, rows):
    def with_ones(x):
        return jnp.concatenate([x, jnp.ones((ONES_ROWS, x.shape[1]), x.dtype)], axis=0)

    n = vt_ref.shape[1]
    kc = min(KEY_CHUNK, n)
    own = [with_ones(vt_ref[rows, c * kc:(c + 1) * kc]) for c in range(n // kc)]
    return own if cache_ref is None else [with_ones(cache_ref[rows, :])] + own


def _attn_kernel(*refs, has_cache, lam_init):
    if has_cache:
        (dqt_ref, dk_ref, dvt_ref, gqt_ref, gk_ref, gvt_ref, cdk_ref, cdvt_ref, cgk_ref, cgvt_ref,
         lp_ref, dn_ref, b64_ref, ob_ref, od_ref) = refs
    else:
        dqt_ref, dk_ref, dvt_ref, gqt_ref, gk_ref, gvt_ref, lp_ref, dn_ref, b64_ref, ob_ref, od_ref = refs
        cdk_ref = cdvt_ref = cgk_ref = cgvt_ref = None

    lp = lp_ref[...]
    lam = (jnp.exp(jnp.sum(lp[0:1] * lp[1:2], axis=-1, keepdims=True))
           - jnp.exp(jnp.sum(lp[2:3] * lp[3:4], axis=-1, keepdims=True)) + lam_init)

    chan = lax.broadcasted_iota(jnp.int32, (BRANCH_W, 1), 0)
    chan_h = lax.broadcasted_iota(jnp.int32, (LANES, 1), 0)
    dqt = dqt_ref[...]
    dks = _key_pieces(dk_ref, cdk_ref)
    gks = _key_pieces(gk_ref, cgk_ref)
    tasks = []
    for h in range(N_HEADS):
        vts = _value_pieces(dvt_ref, cdvt_ref, slice(h * HEAD_W, (h + 1) * HEAD_W))
        for m in range(2):
            lo = h * HEAD_W + m * DIFF_DQK
            tasks.append((dqt * jnp.where((chan >= lo) & (chan < lo + DIFF_DQK), 1.0, 0.0).astype(BF16), dks, vts))
    kv_vts = [_value_pieces(gvt_ref, cgvt_ref, slice(g * HEAD_W, (g + 1) * HEAD_W)) for g in range(GQA_KV_HEADS)]
    for half in range(2):
        qh = gqt_ref[half * LANES:(half + 1) * LANES, :]
        for g in range(GQA_KV_HEADS):
            tasks.append((qh * jnp.where(chan_h // HEAD_W == g, 1.0, 0.0).astype(BF16), gks, kv_vts[g]))
    outs = _attend_all(tasks)

    o = jnp.concatenate([outs[2 * h] - lam * outs[2 * h + 1] for h in range(N_HEADS)], axis=0).T
    b64 = b64_ref[...]
    for j in range(BRANCH_W // LANES):
        sl = slice(j * LANES, (j + 1) * LANES)
        x = o[:, sl]
        ms = _group_sum(x * x, b64) * (1.0 / HEAD_W)
        ob_ref[:, sl] = _bf(x * lax.rsqrt(ms + EPS) * dn_ref[:, sl] * (1.0 - lam_init))
    od_ref[...] = _bf(jnp.concatenate(outs[2 * N_HEADS:], axis=0).T)


def _attn_ctx_call(dqt, dk, dvt, gqt, gk, gvt, lp, dn, b64, *, layer, n_seq, seq_len, lam_init):
    kern = functools.partial(_attn_kernel, has_cache=False, lam_init=lam_init)

    def rows(width):
        return pl.BlockSpec((seq_len, width), lambda s: (s, 0))

    def cols(height):
        return pl.BlockSpec((height, seq_len), lambda s: (0, s))

    def const(shape):
        return pl.BlockSpec(shape, lambda s: (0,) * len(shape))

    return pl.pallas_call(
        kern,
        grid=(n_seq,),
        in_specs=[cols(256), rows(256), cols(256), cols(256), rows(128), cols(128),
                  _layer_spec((4, DIFF_DQK), layer), _layer_spec((1, BRANCH_W), layer), const((LANES, LANES))],
        out_specs=[rows(256), rows(256)],
        out_shape=[jax.ShapeDtypeStruct((n_seq * seq_len, BRANCH_W), BF16)] * 2,
        compiler_params=pltpu.CompilerParams(dimension_semantics=("parallel",), vmem_limit_bytes=VMEM_LIMIT),
        name="attn_ctx",
    )(dqt, dk, dvt, gqt, gk, gvt, lp, dn, b64)


def _attn_lat_call(dqt, dk, dvt, gqt, gk, gvt, cdk, cdvt, cgk, cgvt, lp, dn, b64, *, layer, row0, n_seq, seq_len,
                   lam_init):
    kern = functools.partial(_attn_kernel, has_cache=True, lam_init=lam_init)
    tq = ATTN_TQ
    qpb = seq_len // tq
    q0 = row0 // tq
    s0 = row0 // seq_len
    past = cdk.shape[1]

    def qcols(height):
        return pl.BlockSpec((height, tq), lambda b, i: (0, q0 + b * qpb + i))

    def kvrows(width):
        return pl.BlockSpec((seq_len, width), lambda b, i: (s0 + b, 0))

    def kvcols(height):
        return pl.BlockSpec((height, seq_len), lambda b, i: (0, s0 + b))

    def cache(width):
        return pl.BlockSpec((None, past, width), lambda b, i: (b, 0, 0))

    def cache_t(height):
        return pl.BlockSpec((None, height, past), lambda b, i: (b, 0, 0))

    def const(shape):
        return pl.BlockSpec(shape, lambda b, i: (0,) * len(shape))

    out = pl.BlockSpec((tq, BRANCH_W), lambda b, i: (b * qpb + i, 0))
    return pl.pallas_call(
        kern,
        grid=(n_seq, qpb),
        in_specs=[qcols(256), kvrows(256), kvcols(256), qcols(256), kvrows(128), kvcols(128),
                  cache(256), cache_t(256), cache(128), cache_t(128),
                  _layer_spec((4, DIFF_DQK), layer), _layer_spec((1, BRANCH_W), layer), const((LANES, LANES))],
        out_specs=[out, out],
        out_shape=[jax.ShapeDtypeStruct((n_seq * seq_len, BRANCH_W), BF16)] * 2,
        compiler_params=pltpu.CompilerParams(dimension_semantics=("parallel", "arbitrary"),
                                             vmem_limit_bytes=VMEM_LIMIT),
        name="attn_lat",
    )(dqt, dk, dvt, gqt, gk, gvt, cdk, cdvt, cgk, cgvt, lp, dn, b64)


def _merge_kernel(h_ref, mod_ref, n1_ref, oac_ref, oal_ref, gate_ref, gn_ref, obc_ref, obl_ref, oc_ref,
                  odc_ref, odl_ref, wmg_ref, wb_ref, wo_ref, b64_ref, out_ref, *, tiles_ctx):
    is_ctx = pl.program_id(0) < tiles_ctx
    h = h_ref[...]
    xm = _bf(_norm_mod(h, n1_ref[...], mod_ref[:, D_MODEL:2 * D_MODEL], mod_ref[:, 0:D_MODEL]))
    g1 = mod_ref[:, 2 * D_MODEL:3 * D_MODEL]
    b64 = b64_ref[...]
    oa_parts = []
    for j in range(BRANCH_W // LANES):
        sl = slice(j * LANES, (j + 1) * LANES)
        x = jnp.concatenate([jnp.where(is_ctx, oac_ref[2 * j + hh], oal_ref[2 * j + hh])
                             for hh in range(LANES // HEAD_W)], axis=-1)
        ms = _group_sum(x * x, b64) * (1.0 / HEAD_W)
        oa_parts.append(_bf(x * lax.rsqrt(ms + EPS) * gn_ref[:, sl] * _silu(gate_ref[:, sl])))
    branches = (None, jnp.where(is_ctx, obc_ref[...], obl_ref[...]), oc_ref[...],
                jnp.where(is_ctx, odc_ref[...], odl_ref[...]))
    merged = None
    for n in range(4):
        mg = _dot(xm, wmg_ref[:, n * D_MODEL:(n + 1) * D_MODEL])
        if n == 0:
            up = _dot(oa_parts[0], wb_ref[0, 0:LANES, :]) + _dot(oa_parts[1], wb_ref[0, LANES:2 * LANES, :])
        else:
            up = _dot(branches[n], wb_ref[n])
        term = _sigmoid(mg) * up
        merged = term if merged is None else merged + term
    out_ref[...] = h + g1 * _dot(_bf(merged), wo_ref[...])


def _merge_call(h, mods, n1, oa_ctx, oa_lat, gate, gn, ob_ctx, ob_lat, oc, od_ctx, od_lat, wmg, wb, wo, b64, *,
                layer, tiles_ctx, tiles_per_lat):
    t = h.shape[0]
    last_ctx = tiles_ctx - 1

    def mod_row(i):
        return jnp.where(i < tiles_ctx, 0, 1 + (i - tiles_ctx) // tiles_per_lat)

    def rows(width):
        return pl.BlockSpec((TM, width), lambda i: (i, 0))

    def ctx_rows(width):
        return pl.BlockSpec((TM, width), lambda i: (jnp.minimum(i, last_ctx), 0))

    def lat_rows(width):
        return pl.BlockSpec((TM, width), lambda i: (jnp.maximum(i - tiles_ctx, 0), 0))

    return pl.pallas_call(
        functools.partial(_merge_kernel, tiles_ctx=tiles_ctx),
        grid=(t // TM,),
        in_specs=[
            rows(D_MODEL),
            _mod_spec(layer, mod_row),
            _layer_spec((1, D_MODEL), layer),
            pl.BlockSpec((N_HEADS, TM, HEAD_W), lambda i: (0, jnp.minimum(i, last_ctx), 0)),
            pl.BlockSpec((N_HEADS, TM, HEAD_W), lambda i: (0, jnp.maximum(i - tiles_ctx, 0), 0)),
            rows(BRANCH_W),
            _layer_spec((1, BRANCH_W), layer),
            ctx_rows(BRANCH_W), lat_rows(BRANCH_W), rows(BRANCH_W), ctx_rows(BRANCH_W), lat_rows(BRANCH_W),
            _layer_spec((D_MODEL, 4 * D_MODEL), layer),
            _layer_spec((4, BRANCH_W, D_MODEL), layer),
            _layer_spec((D_MODEL, D_MODEL), layer),
            pl.BlockSpec((LANES, LANES), lambda i: (0, 0)),
        ],
        out_specs=rows(D_MODEL),
        out_shape=jax.ShapeDtypeStruct((t, D_MODEL), F32),
        compiler_params=pltpu.CompilerParams(dimension_semantics=("parallel",), vmem_limit_bytes=VMEM_LIMIT),
        name="merge",
    )(h, mods, n1, oa_ctx, oa_lat, gate, gn, ob_ctx, ob_lat, oc, od_ctx, od_lat, wmg, wb, wo, b64)


ROUTER_ROWS = 32
ROUTER_E0 = 8


def _moe_kernel(h_ref, mod_ref, n2_ref, wr_ref, br_ref, w1_ref, w3_ref, w2_ref, out_ref, xm_scr, gates_scr, acc_scr):
    e = pl.program_id(1)
    tm = h_ref.shape[0]

    @pl.when(e == 0)
    def _():
        xm = _norm_mod(h_ref[...], n2_ref[...], mod_ref[:, 4 * D_MODEL:5 * D_MODEL],
                       mod_ref[:, 3 * D_MODEL:4 * D_MODEL])
        x_hi = _bf(xm)
        xm_scr[...] = x_hi
        x_lo = _bf(xm - x_hi.astype(F32))
        hi = _dot_nt(wr_ref[...], x_hi)
        lt = (hi[0:ROUTER_ROWS] + hi[ROUTER_ROWS:2 * ROUTER_ROWS]
              + _dot_nt(wr_ref[0:ROUTER_ROWS, :], x_lo) + br_ref[...])
        gl = lt[0:N_GROUPS]
        gidx = lax.broadcasted_iota(jnp.int32, (N_GROUPS, 1), 0)
        gmax = jnp.max(gl, axis=0, keepdims=True)
        gsel = jnp.min(jnp.where(gl == gmax, gidx, N_GROUPS), axis=0, keepdims=True)
        gprob = 1.0 / jnp.sum(jnp.exp(gl - gmax), axis=0, keepdims=True)
        el = lt[ROUTER_E0:ROUTER_E0 + N_EXPERTS]
        eidx = lax.broadcasted_iota(jnp.int32, (N_EXPERTS, 1), 0)
        neg = -jnp.inf
        cand = jnp.where(eidx // EXPERTS_PER_GROUP == gsel, el, neg)
        v1 = jnp.max(cand, axis=0, keepdims=True)
        i1 = jnp.min(jnp.where(cand == v1, eidx, N_EXPERTS), axis=0, keepdims=True)
        cand2 = jnp.where(eidx == i1, neg, cand)
        v2 = jnp.max(cand2, axis=0, keepdims=True)
        i2 = jnp.min(jnp.where(cand2 == v2, eidx, N_EXPERTS), axis=0, keepdims=True)
        ex = jnp.exp(v2 - v1)
        wa = 1.0 / (1.0 + ex)
        gates_t = (jnp.where(eidx == i1, wa, 0.0) + jnp.where(eidx == i2, ex * wa, 0.0)) * gprob
        full = jnp.concatenate([gates_t, jnp.zeros((LANES - N_EXPERTS, tm), F32)], axis=0)
        gates_scr[...] = full.T
        acc_scr[...] = jnp.zeros_like(acc_scr)

    xm = xm_scr[...]
    h1 = _dot(xm, w1_ref[...])
    h3 = _dot(xm, w3_ref[...])
    lane = lax.broadcasted_iota(jnp.int32, (1, LANES), 1)
    gate = jnp.sum(jnp.where(lane == e, gates_scr[...], 0.0), axis=-1, keepdims=True)
    hd = _silu(h1) * h3 * gate
    acc_scr[...] += _dot(_bf(hd), w2_ref[...])

    @pl.when(e == N_EXPERTS - 1)
    def _():
        out_ref[...] = h_ref[...] + mod_ref[:, 5 * D_MODEL:6 * D_MODEL] * acc_scr[...]


def _moe_call(h, mods, n2, wr, br, w1, w3, w2, *, layer, ctx_rows, lat_len):
    t = h.shape[0]
    tm = TM_MOE
    tiles_ctx = ctx_rows // tm
    tiles_per_lat = lat_len // tm

    def mod_row(i):
        return jnp.where(i < tiles_ctx, 0, 1 + (i - tiles_ctx) // tiles_per_lat)

    def expert(shape):
        return pl.BlockSpec((None, None) + shape, lambda i, e: (layer, e, 0, 0))

    return pl.pallas_call(
        _moe_kernel,
        grid=(t // tm, N_EXPERTS),
        in_specs=[
            pl.BlockSpec((tm, D_MODEL), lambda i, e: (i, 0)),
            _mod_spec(layer, mod_row),
            _layer_spec((1, D_MODEL), layer),
            _layer_spec((2 * ROUTER_ROWS, D_MODEL), layer),
            _layer_spec((ROUTER_ROWS, 1), layer),
            expert((D_MODEL, D_EXPERT)),
            expert((D_MODEL, D_EXPERT)),
            expert((D_EXPERT, D_MODEL)),
        ],
        out_specs=pl.BlockSpec((tm, D_MODEL), lambda i, e: (i, 0)),
        out_shape=jax.ShapeDtypeStruct((t, D_MODEL), F32),
        scratch_shapes=[pltpu.VMEM((tm, D_MODEL), BF16), pltpu.VMEM((tm, LANES), F32),
                        pltpu.VMEM((tm, D_MODEL), F32)],
        compiler_params=pltpu.CompilerParams(dimension_semantics=("parallel", "arbitrary"),
                                             vmem_limit_bytes=VMEM_LIMIT),
        name="moe",
    )(h, mods, n2, wr, br, w1, w3, w2)


def _final_norm_kernel(h_ref, g_ref, o_ref):
    h = h_ref[...]
    o_ref[...] = h * lax.rsqrt(jnp.mean(h * h, axis=-1, keepdims=True) + EPS) * g_ref[...]


def _final_norm_call(h, g, *, row0, n_rows):
    blk0 = row0 // TM
    return pl.pallas_call(
        _final_norm_kernel,
        grid=(n_rows // TM,),
        in_specs=[pl.BlockSpec((TM, D_MODEL), lambda i: (blk0 + i, 0)), pl.BlockSpec((1, D_MODEL), lambda i: (0, 0))],
        out_specs=pl.BlockSpec((TM, D_MODEL), lambda i: (i, 0)),
        out_shape=jax.ShapeDtypeStruct((n_rows, D_MODEL), F32),
        compiler_params=pltpu.CompilerParams(dimension_semantics=("parallel",)),
        name="final_norm",
    )(h, g)


def _mixer_weights(w_in):
    depth, d, _ = w_in.shape
    ab_cols = []
    for h in range(N_HEADS):
        ab_cols += [_O_A + h, _O_A + N_HEADS + h, _O_B + h, _O_B + N_HEADS + h]
    ab = jnp.take(w_in[:, :, _O_A:_O_DQ], jnp.asarray([c - _O_A for c in ab_cols], dtype=jnp.int32), axis=2)
    parts = [w_in[:, :, _O_QKV:_O_A], ab, jnp.zeros((depth, d, LANES - len(ab_cols)), w_in.dtype),
             w_in[:, :, _O_DQ:_O_GQ]]
    for h in (0, 2, 1, 3):
        parts.append(w_in[:, :, _O_GQ + h * HEAD_W:_O_GQ + (h + 1) * HEAD_W])
    parts.append(w_in[:, :, _O_GK:_O_MG])
    w_mix = _bf(jnp.concatenate(parts, axis=2))
    assert w_mix.shape[2] == C_END
    return w_mix


def _rope_tables(lat_len):
    pos = jnp.arange(lat_len, dtype=jnp.int32)
    row = (pos // GRID_W).astype(F32)[:, None]
    col = (pos % GRID_W).astype(F32)[:, None]
    lane = jnp.arange(LANES, dtype=jnp.int32)[None, :]
    tabs = []
    for head_w in (DIFF_DQK, HEAD_W):
        half = head_w // 2
        nf = half // 2
        p = lane % head_w
        f = (p % half) % nf
        inv = ROPE_THETA ** (-f.astype(F32) / nf)
        ang = jnp.where(p < half, row, col) * inv
        sign = jnp.where((p % half) < nf, -1.0, 1.0)
        tabs += [jnp.cos(ang), jnp.sin(ang) * sign]
    tab = jnp.stack(tabs)
    ident = jnp.stack([jnp.ones((TM, LANES), F32), jnp.zeros((TM, LANES), F32)] * 2)
    return jnp.concatenate([ident, tab], axis=1)


def _chunk_tri(n, chunk, upper):
    i = jnp.arange(n)[:, None]
    j = jnp.arange(n)[None, :]
    same = (i // chunk) == (j // chunk)
    tri = (j >= i) if upper else (j <= i)
    return jnp.where(same & tri, 1.0, 0.0).astype(BF16)


def kernel(x_prompt, x_sample, state_gdn, cache_diff_k, cache_diff_v, cache_gqa_k, cache_gqa_v, c, c_ctx,
           ada_w, ada_b, norm1, norm2, w_in, gdn_conv, gdn_a_log, gdn_dt_bias, gdn_norm, diff_lambda, diff_norm,
           sgu_norm, sgu_ws, sgu_b, gqa_qnorm, gqa_knorm, w_branch, w_out, moe_wg, moe_bg, moe_we, moe_be,
           moe_w1, moe_w3, moe_w2, final_norm):
    nb, seq, d = x_prompt.shape
    nl, lat_len, _ = x_sample.shape
    depth = ada_w.shape[0]
    past = cache_diff_k.shape[2]
    ctx_rows = nb * seq
    lat_rows = nl * lat_len
    assert d == D_MODEL and seq % PREP_R == 0 and lat_len % TM == 0 and ctx_rows % lat_len == 0
    assert ctx_rows % TM == 0 and 1 + nl <= SUBLANES and lat_len % GRID_W == 0
    assert ctx_rows % TM_MOE == 0 and lat_len % TM_MOE == 0
    tiles_ctx = ctx_rows // TM
    tiles_per_lat = lat_len // TM
    tile_kw = dict(tiles_ctx=tiles_ctx, tiles_per_lat=tiles_per_lat)

    lane = jnp.arange(LANES)
    b64 = jnp.where((lane[:, None] // HEAD_W) == (lane[None, :] // HEAD_W), 1.0, 0.0).astype(BF16)
    tril = _chunk_tri(PREP_R, GDN_CHUNK, upper=False)
    triu = _chunk_tri(PREP_R, GDN_CHUNK, upper=True)
    rope_tab = _rope_tables(lat_len)

    w_mix = _mixer_weights(w_in)
    w_mg = _bf(w_in[:, :, _O_MG:])
    head_perm = jnp.asarray([r for h in (0, 2, 1, 3) for r in range(h * HEAD_W, (h + 1) * HEAD_W)], dtype=jnp.int32)
    wb = _bf(jnp.concatenate([w_branch[:, :3], jnp.take(w_branch[:, 3], head_perm, axis=1)[:, None]], axis=1))
    wo = _bf(w_out)
    w1, w3, w2 = _bf(moe_w1), _bf(moe_w3), _bf(moe_w2)
    wr = jnp.zeros((depth, ROUTER_ROWS, d), F32)
    wr = wr.at[:, 0:N_GROUPS].set(jnp.swapaxes(moe_wg, 1, 2))
    wr = wr.at[:, ROUTER_E0:ROUTER_E0 + N_EXPERTS].set(jnp.swapaxes(moe_we, 1, 2))
    wr_hi = _bf(wr)
    wr = jnp.concatenate([wr_hi, _bf(wr - wr_hi.astype(F32))], axis=1)
    br = jnp.zeros((depth, ROUTER_ROWS, 1), F32)
    br = br.at[:, 0:N_GROUPS, 0].set(moe_bg).at[:, ROUTER_E0:ROUTER_E0 + N_EXPERTS, 0].set(moe_be)
    conv_w = jnp.zeros((depth, SUBLANES, 3 * BRANCH_W), F32).at[:, :CONV_K].set(gdn_conv)
    ab_lane = jnp.zeros((depth, 1, LANES), F32)
    alog_row, dtb_row = ab_lane, ab_lane
    for h in range(N_HEADS):
        for dd in range(2):
            alog_row = alog_row.at[:, 0, 4 * h + dd].set(gdn_a_log[:, dd, h])
            dtb_row = dtb_row.at[:, 0, 4 * h + dd].set(gdn_dt_bias[:, dd, h])
    sgw = _bf(sgu_ws)
    sgb = jnp.repeat(jnp.swapaxes(sgu_b, 1, 2), HEAD_W, axis=2)
    gdn_gn = jnp.tile(gdn_norm, (1, N_HEADS))[:, None, :]
    diff_gn = jnp.tile(diff_norm, (1, N_HEADS))[:, None, :]
    gqn = jnp.tile(gqa_qnorm, (1, LANES // HEAD_W))[:, None, :]
    gkn = jnp.tile(gqa_knorm, (1, LANES // HEAD_W))[:, None, :]

    cond = jnp.zeros((SUBLANES, d), F32).at[0].set(c_ctx).at[1:1 + nl].set(c)
    mods = _ada_call(cond, ada_w, ada_b)[:, :, None, :]
    n1 = norm1[:, None, :]
    n2 = norm2[:, None, :]
    sgn = sgu_norm[:, None, :]

    h = jnp.concatenate([x_prompt.reshape(ctx_rows, d), x_sample.reshape(lat_rows, d)], axis=0)
    new_s, new_dk, new_dv, new_gk, new_gv = [], [], [], [], []
    for l in range(depth):
        lam_init = 0.8 - 0.6 * math.exp(-0.3 * l)
        (gqkv, ggate, gab, dqt, dk32, dv32, dk16, dvt, oc, gqt, gk32, gv32, gk16, gvt) = _inproj_call(
            h, mods, n1, w_mix, rope_tab, b64, sgn, sgw, sgb, gqn, gkn, layer=l, **tile_kw)

        qh, kh, vh, gb = _gdn_prep_call(gqkv, gab, conv_w, alog_row, dtb_row, b64, tril, triu, layer=l,
                                        ctx_rows=ctx_rows, ctx_len=seq, lat_len=lat_len)
        oa_ctx, s_ctx = _gdn_scan_call(qh, kh, vh, gb, None, row0=0, n_seq_total=nb, seq_len=seq,
                                       seqs_per_step=8 if nb % 8 == 0 else 1)
        oa_lat, _ = _gdn_scan_call(qh, kh, vh, gb, state_gdn[:, l], row0=ctx_rows, n_seq_total=nl,
                                   seq_len=lat_len, seqs_per_step=1)

        ob_ctx, od_ctx = _attn_ctx_call(dqt, dk16, dvt, gqt, gk16, gvt, diff_lambda, diff_gn, b64, layer=l,
                                        n_seq=nb, seq_len=seq, lam_init=lam_init)
        ob_lat, od_lat = _attn_lat_call(
            dqt, dk16, dvt, gqt, gk16, gvt,
            _bf(cache_diff_k[:, l].reshape(nl, past, BRANCH_W)),
            _bf(jnp.swapaxes(cache_diff_v[:, l].reshape(nl, past, BRANCH_W), 1, 2)),
            _bf(cache_gqa_k[:, l].reshape(nl, past, LANES)),
            _bf(jnp.swapaxes(cache_gqa_v[:, l].reshape(nl, past, LANES), 1, 2)),
            diff_lambda, diff_gn, b64, layer=l, row0=ctx_rows, n_seq=nl, seq_len=lat_len, lam_init=lam_init)

        h = _merge_call(h, mods, n1, oa_ctx, oa_lat, ggate, gdn_gn, ob_ctx, ob_lat, oc, od_ctx, od_lat,
                        w_mg, wb, wo, b64, layer=l, **tile_kw)
        h = _moe_call(h, mods, n2, wr, br, w1, w3, w2, layer=l, ctx_rows=ctx_rows, lat_len=lat_len)

        new_s.append(s_ctx)
        new_dk.append(dk32[:ctx_rows].reshape(nb, seq, N_HEADS, HEAD_W))
        new_dv.append(dv32[:ctx_rows].reshape(nb, seq, N_HEADS, HEAD_W))
        new_gk.append(gk32[:ctx_rows].reshape(nb, seq, GQA_KV_HEADS, HEAD_W))
        new_gv.append(gv32[:ctx_rows].reshape(nb, seq, GQA_KV_HEADS, HEAD_W))

    fg = final_norm[None, :]
    y_ctx = _final_norm_call(h, fg, row0=0, n_rows=ctx_rows)
    y_lat = _final_norm_call(h, fg, row0=ctx_rows, n_rows=lat_rows)
    return (y_ctx.reshape(nb, seq, d), y_lat.reshape(nl, lat_len, d),
            jnp.stack(new_s, axis=1), jnp.stack(new_dk, axis=1), jnp.stack(new_dv, axis=1),
            jnp.stack(new_gk, axis=1), jnp.stack(new_gv, axis=1))
```

```python
import functools
import math

import jax
import jax.numpy as jnp
from jax import lax
from jax.experimental import pallas as pl
from jax.experimental.pallas import tpu as pltpu

F32 = jnp.float32
BF16 = jnp.bfloat16

GRID_W = 64
ROPE_THETA = 10000.0
EPS = 1e-6
D_MODEL = 1024
BRANCH_W = 256
HEAD_W = 64
DIFF_DQK = 32
N_HEADS = 4
GQA_KV_HEADS = 2
GDN_CHUNK = 64
CONV_K = 5
SGU_CHUNK = 128
SGU_GROUPS = 4
N_GROUPS = 4
EXPERTS_PER_GROUP = 4
N_EXPERTS = 16
D_EXPERT = 256
LOG2E = 1.4426950408889634
DIFF_SCALE = DIFF_DQK ** -0.5 * LOG2E
GQA_SCALE = HEAD_W ** -0.5 * LOG2E
GDN_Q_SCALE = HEAD_W ** -0.5

LANES = 128
SUBLANES = 8
VMEM_LIMIT = 56 * 1024 * 1024

_IN_WIDTHS = (768, 256, 8, 8, 256, 256, 256, 256, 256, 256, 128, 128, 4096)
_IN_OFFS = tuple(sum(_IN_WIDTHS[:i]) for i in range(len(_IN_WIDTHS)))
(_O_QKV, _O_GATE, _O_A, _O_B, _O_DQ, _O_DK, _O_DV, _O_SU, _O_SV, _O_GQ, _O_GK, _O_GV, _O_MG) = _IN_OFFS
MIX_COLS = _O_MG

C_QKV, C_GATE, C_AB, C_DQ, C_DK, C_DV, C_SU, C_SV, C_GQ, C_GK, C_GV, C_END = (
    0, 768, 1024, 1152, 1408, 1664, 1920, 2176, 2432, 2688, 2816, 2944)

TM = 512
TM_MOE = 1024
MOE_EPS = 4
PREP_R = 256
HALO = SUBLANES
SCAN_G = 4
PREP_PAR = 4
ATTN_TQ = 256


def _bf(x):
    return x.astype(BF16)


def _dot(a, b):
    return jnp.dot(a, b, preferred_element_type=F32)


def _dot_nt(a, b):
    return lax.dot_general(a, b, (((1,), (1,)), ((), ())), preferred_element_type=F32)


def _dot_tn(a, b):
    return lax.dot_general(a, b, (((0,), (0,)), ((), ())), preferred_element_type=F32)


def _split_dot(x, m, terms):
    acc = None
    r = x
    for t in range(terms):
        p = _bf(r)
        d = _dot(p, m)
        acc = d if acc is None else acc + d
        if t + 1 < terms:
            r = r - p.astype(F32)
    return acc


def _split_dot_left(m, x, terms):
    acc = None
    r = x
    for t in range(terms):
        p = _bf(r)
        d = _dot(m, p)
        acc = d if acc is None else acc + d
        if t + 1 < terms:
            r = r - p.astype(F32)
    return acc


def _sigmoid(x):
    return 1.0 / (1.0 + jnp.exp(-x))


def _silu(x):
    return x * _sigmoid(x)


def _norm_mod(h, gamma, scale, shift):
    ms = jnp.mean(h * h, axis=-1, keepdims=True)
    return (h * lax.rsqrt(ms + EPS) * gamma) * (1.0 + scale) + shift


def _group_sum(sq, b64):
    return _split_dot(sq, b64, 2)


def _rope(x, cos, sin_signed, nf):
    lane = lax.broadcasted_iota(jnp.int32, (1, LANES), 1)
    first = (lane % (2 * nf)) < nf
    partner = jnp.where(first, pltpu.roll(x, LANES - nf, 1), pltpu.roll(x, nf, 1))
    return x * cos + partner * sin_signed


def _layer_spec(shape, layer):
    return pl.BlockSpec((None,) + tuple(shape), lambda *_: (layer,) + (0,) * len(shape))


def _mod_spec(layer, mod_row):
    return pl.BlockSpec((None, None, 1, 6 * D_MODEL), lambda i, *_: (layer, mod_row(i), 0, 0))


def _ada_kernel(c_ref, w_ref, b_ref, o_ref):
    x = _silu(c_ref[...])
    o_ref[...] = jnp.dot(x, w_ref[...], precision=lax.Precision.HIGHEST,
                         preferred_element_type=F32) + b_ref[...]


def _ada_call(cond, ada_w, ada_b):
    n_layers, d, n_out = ada_w.shape
    tn = 1536
    return pl.pallas_call(
        _ada_kernel,
        grid=(n_layers, n_out // tn),
        in_specs=[
            pl.BlockSpec((SUBLANES, d), lambda l, j: (0, 0)),
            pl.BlockSpec((None, d, tn), lambda l, j: (l, 0, j)),
            pl.BlockSpec((None, 1, tn), lambda l, j: (l, 0, j)),
        ],
        out_specs=pl.BlockSpec((None, SUBLANES, tn), lambda l, j: (l, 0, j)),
        out_shape=jax.ShapeDtypeStruct((n_layers, SUBLANES, n_out), F32),
        compiler_params=pltpu.CompilerParams(vmem_limit_bytes=VMEM_LIMIT),
        name="ada_mod",
    )(cond, ada_w, ada_b.reshape(n_layers, 1, n_out))


def _inproj_kernel(h_ref, mod_ref, n1_ref, w_ref, rope_ref, b64_ref, sgn_ref, sgw_ref, sgb_ref, gqn_ref, gkn_ref,
                   gqkv_ref, ggate_ref, gab_ref, dqt_ref, dk32_ref, dv32_ref, dk16_ref, dvt_ref, oc_ref,
                   gqt_ref, gk32_ref, gv32_ref, gk16_ref, gvt_ref):
    tm = h_ref.shape[0]
    xm = _bf(_norm_mod(h_ref[...], n1_ref[...], mod_ref[:, D_MODEL:2 * D_MODEL], mod_ref[:, 0:D_MODEL]))

    def proj(a, b):
        return _dot(xm, w_ref[:, a:b])

    gqkv_ref[...] = proj(C_QKV, C_GATE)
    ggate_ref[...] = proj(C_GATE, C_AB)
    gab_ref[...] = proj(C_AB, C_DQ)

    cos_d, sin_d, cos_g, sin_g = rope_ref[0], rope_ref[1], rope_ref[2], rope_ref[3]
    b64 = b64_ref[...]

    for j in range(BRANCH_W // LANES):
        sl = slice(j * LANES, (j + 1) * LANES)
        q = proj(C_DQ + j * LANES, C_DQ + (j + 1) * LANES)
        dqt_ref[sl, :] = _bf((_rope(q, cos_d, sin_d, DIFF_DQK // 4) * DIFF_SCALE).T)
        k = _rope(proj(C_DK + j * LANES, C_DK + (j + 1) * LANES), cos_d, sin_d, DIFF_DQK // 4)
        dk32_ref[:, sl] = k
        dk16_ref[:, sl] = _bf(k)
    v = proj(C_DV, C_SU)
    dv32_ref[...] = v
    dvt_ref[...] = _bf(v.T)

    u = proj(C_SU, C_SV)
    sv = proj(C_SV, C_GQ)
    vn = _bf(sv * lax.rsqrt(jnp.mean(sv * sv, axis=-1, keepdims=True) + EPS) * sgn_ref[...])
    group = lax.broadcasted_iota(jnp.int32, (1, BRANCH_W), 1) // HEAD_W
    for r in range(tm // SGU_CHUNK):
        rows = slice(r * SGU_CHUNK, (r + 1) * SGU_CHUNK)
        vc = vn[rows]
        z = sgb_ref[...]
        for g in range(SGU_GROUPS):
            z = z + jnp.where(group == g, _dot(sgw_ref[g], vc), 0.0)
        oc_ref[rows, :] = _bf(u[rows] * z)

    for j in range(BRANCH_W // LANES):
        sl = slice(j * LANES, (j + 1) * LANES)
        q = proj(C_GQ + j * LANES, C_GQ + (j + 1) * LANES)
        qn = q * lax.rsqrt(_group_sum(q * q, b64) * (1.0 / HEAD_W) + EPS) * gqn_ref[...]
        gqt_ref[sl, :] = _bf((_rope(qn, cos_g, sin_g, HEAD_W // 4) * GQA_SCALE).T)
    k = proj(C_GK, C_GV)
    kn = k * lax.rsqrt(_group_sum(k * k, b64) * (1.0 / HEAD_W) + EPS) * gkn_ref[...]
    kr = _rope(kn, cos_g, sin_g, HEAD_W // 4)
    gk32_ref[...] = kr
    gk16_ref[...] = _bf(kr)
    v = proj(C_GV, C_END)
    gv32_ref[...] = v
    gvt_ref[...] = _bf(v.T)


def _inproj_call(h, mods, n1, w_mix, rope_tab, b64, sgn, sgw, sgb, gqn, gkn, *, layer, tiles_ctx, tiles_per_lat):
    t = h.shape[0]
    n_tiles = t // TM

    def mod_row(i):
        return jnp.where(i < tiles_ctx, 0, 1 + (i - tiles_ctx) // tiles_per_lat)

    def rope_blk(i):
        return jnp.where(i < tiles_ctx, 0, 1 + (i - tiles_ctx) % tiles_per_lat)

    def const(shape):
        return pl.BlockSpec(shape, lambda i: (0,) * len(shape))

    def rows(width):
        return pl.BlockSpec((TM, width), lambda i: (i, 0))

    def cols(height):
        return pl.BlockSpec((height, TM), lambda i: (0, i))

    row_major = lambda w, dt: (rows(w), jax.ShapeDtypeStruct((t, w), dt))
    col_major = lambda hgt, dt: (cols(hgt), jax.ShapeDtypeStruct((hgt, t), dt))
    outs = [
        row_major(768, F32), row_major(256, F32), row_major(128, F32),
        col_major(256, BF16), row_major(256, F32), row_major(256, F32),
        row_major(256, BF16), col_major(256, BF16),
        row_major(256, BF16),
        col_major(256, BF16), row_major(128, F32), row_major(128, F32),
        row_major(128, BF16), col_major(128, BF16),
    ]
    return pl.pallas_call(
        _inproj_kernel,
        grid=(n_tiles,),
        in_specs=[
            rows(D_MODEL),
            _mod_spec(layer, mod_row),
            _layer_spec((1, D_MODEL), layer),
            _layer_spec((D_MODEL, C_END), layer),
            pl.BlockSpec((4, TM, LANES), lambda i: (0, rope_blk(i), 0)),
            const((LANES, LANES)),
            _layer_spec((1, BRANCH_W), layer),
            _layer_spec((SGU_GROUPS, SGU_CHUNK, SGU_CHUNK), layer),
            _layer_spec((SGU_CHUNK, BRANCH_W), layer),
            _layer_spec((1, LANES), layer),
            _layer_spec((1, LANES), layer),
        ],
        out_specs=[spec for spec, _ in outs],
        out_shape=[shape for _, shape in outs],
        compiler_params=pltpu.CompilerParams(dimension_semantics=("parallel",), vmem_limit_bytes=VMEM_LIMIT),
        name="inproj",
    )(h, mods, n1, w_mix, rope_tab, b64, sgn, sgw, sgb, gqn, gkn)


def _gdn_prep_kernel(x_ref, xp_ref, xn_ref, ab_ref, cw_ref, alog_ref, dtb_ref, b64_ref, tril_ref, triu_ref,
                     q_ref, k_ref, v_ref, gb_ref, *, ctx_blocks, ctx_bps, lat_bps):
    r = x_ref.shape[0]
    i = pl.program_id(0)
    is_ctx = i < ctx_blocks
    pos = jnp.where(is_ctx, i % ctx_bps, (i - ctx_blocks) % lat_bps)
    bps = jnp.where(is_ctx, ctx_bps, lat_bps)
    prev = jnp.where(pos > 0, xp_ref[...], 0.0)
    nxt = jnp.where(pos < bps - 1, xn_ref[...], 0.0)
    win = jnp.concatenate([prev, x_ref[...], nxt], axis=0)
    n = r + 2 * HALO
    y = None
    for j in range(CONV_K):
        s = (CONV_K // 2 - j) % n
        shifted = win if s == 0 else pltpu.roll(win, s, 0)
        term = cw_ref[j:j + 1, :] * shifted[HALO:HALO + r]
        y = term if y is None else y + term
    y = _silu(y)

    b64 = b64_ref[...]
    for part, ref, scale in ((0, q_ref, GDN_Q_SCALE), (1, k_ref, 1.0)):
        for j in range(BRANCH_W // LANES):
            x = y[:, part * BRANCH_W + j * LANES: part * BRANCH_W + (j + 1) * LANES]
            xn = x * (lax.rsqrt(_group_sum(x * x, b64) + EPS) * scale)
            for hh in range(LANES // HEAD_W):
                ref[j * (LANES // HEAD_W) + hh] = xn[:, hh * HEAD_W:(hh + 1) * HEAD_W]
    for h in range(N_HEADS):
        v_ref[h] = y[:, 2 * BRANCH_W + h * HEAD_W: 2 * BRANCH_W + (h + 1) * HEAD_W]

    ab = ab_ref[...]
    lane = lax.broadcasted_iota(jnp.int32, (1, LANES), 1)
    lm = lane % 4
    xa = ab + dtb_ref[...]
    softplus = jnp.maximum(xa, 0.0) + jnp.log1p(jnp.exp(-jnp.abs(xa)))
    la = jnp.where((lm < 2) & (lane < 4 * N_HEADS), -jnp.exp(alog_ref[...]) * softplus, 0.0)
    g_fwd = _split_dot_left(tril_ref[...], la, 3)
    g_bwd = _split_dot_left(triu_ref[...], la, 3)
    gb = jnp.where(lm == 0, g_fwd, jnp.where(lm == 1, g_bwd, _sigmoid(ab)))
    for h in range(N_HEADS):
        gb_ref[h] = gb if h == 0 else pltpu.roll(gb, LANES - 4 * h, 1)


def _gdn_prep_call(gqkv, gab, conv_w, alog_row, dtb_row, b64, tril, triu, *, layer, ctx_rows, ctx_len, lat_len):
    t = gqkv.shape[0]
    r = PREP_R
    n_blocks = t // r
    rh = r // HALO
    last_halo = t // HALO - 1
    kern = functools.partial(_gdn_prep_kernel, ctx_blocks=ctx_rows // r, ctx_bps=ctx_len // r, lat_bps=lat_len // r)

    def const(shape):
        return pl.BlockSpec(shape, lambda i: (0,) * len(shape))

    head_major = pl.BlockSpec((N_HEADS, r, HEAD_W), lambda i: (0, i, 0))
    return pl.pallas_call(
        kern,
        grid=(n_blocks,),
        in_specs=[
            pl.BlockSpec((r, 3 * BRANCH_W), lambda i: (i, 0)),
            pl.BlockSpec((HALO, 3 * BRANCH_W), lambda i: (jnp.maximum(i * rh - 1, 0), 0)),
            pl.BlockSpec((HALO, 3 * BRANCH_W), lambda i: (jnp.minimum((i + 1) * rh, last_halo), 0)),
            pl.BlockSpec((r, LANES), lambda i: (i, 0)),
            _layer_spec((SUBLANES, 3 * BRANCH_W), layer),
            _layer_spec((1, LANES), layer),
            _layer_spec((1, LANES), layer),
            const((LANES, LANES)),
            const((r, r)),
            const((r, r)),
        ],
        out_specs=[head_major, head_major, head_major,
                   pl.BlockSpec((N_HEADS, r, LANES), lambda i: (0, i, 0))],
        out_shape=[jax.ShapeDtypeStruct((N_HEADS, t, HEAD_W), F32)] * 3
        + [jax.ShapeDtypeStruct((N_HEADS, t, LANES), F32)],
        compiler_params=pltpu.CompilerParams(dimension_semantics=("parallel",), vmem_limit_bytes=VMEM_LIMIT),
        name="gdn_prep",
    )(gqkv, gqkv, gqkv, gab, conv_w, alog_row, dtb_row, b64, tril, triu)


def _gdn_scan_kernel(*refs, n_seq, seq_len, has_s0):
    if has_s0:
        q_ref, k_ref, v_ref, gb_ref, s0_ref, o_ref, s_ref, oi_scr, qg_scr, m_scr, b_scr = refs
    else:
        q_ref, k_ref, v_ref, gb_ref, o_ref, s_ref, oi_scr, qg_scr, m_scr, b_scr = refs
        s0_ref = None
    c = GDN_CHUNK
    gr = SCAN_G * c
    n_rows = n_seq * seq_len
    nc = seq_len // c
    row_i = lax.broadcasted_iota(jnp.int32, (gr, gr), 0)
    col_i = lax.broadcasted_iota(jnp.int32, (gr, gr), 1)
    eye = jnp.where(row_i == col_i, 1.0, 0.0)
    same = (row_i // c) == (col_i // c)
    incl = (same & (col_i <= row_i), same & (col_i >= row_i))
    strict = (same & (col_i < row_i), same & (col_i > row_i))
    blk16 = (row_i // 16) == (col_i // 16)
    blk32 = ((row_i // 32) == (col_i // 32)) & ~blk16
    blk64 = same & ((row_i // 32) != (col_i // 32))

    def prep_groups(gi, carry):
        probs = []
        for j in range(PREP_PAR):
            r0 = pl.multiple_of((gi * PREP_PAR + j) * gr, gr)
            rows = pl.ds(r0, gr)
            q = q_ref[rows, :]
            k = k_ref[rows, :]
            v = v_ref[rows, :]
            gbc = gb_ref[rows, :]
            gbt = gbc.T
            kb = _bf(k)
            kk = _dot_nt(kb, kb)
            qk = _dot_nt(_bf(q), kb)
            for d in range(2):
                gcol = gbc[:, d:d + 1]
                bcol = gbc[:, 2 + d:3 + d]
                dec = jnp.where(incl[d], jnp.exp(jnp.where(incl[d], gcol - gbt[d:d + 1, :], 0.0)), 0.0)
                probs.append(dict(d=d, r0=r0, rows=rows, q=q, k=k, v=v, gcol=gcol, bcol=bcol, dec=dec,
                                  qkm=_bf(qk * dec), nmat=jnp.where(strict[d], bcol * kk * dec, 0.0)))
        ps = [jnp.where(blk16, pr["nmat"], 0.0) for pr in probs]
        ts = [eye - p for p in ps]
        for _ in range(3):
            pbs = [_bf(p) for p in ps]
            ps = [_dot(pb, pb) for pb in pbs]
            ts = [t + _dot(_bf(t), _bf(p)) for t, p in zip(ts, ps)]
        for level in (blk32, blk64):
            tbs = [_bf(t) for t in ts]
            xs = [_dot(tb, _bf(jnp.where(level, pr["nmat"], 0.0))) for tb, pr in zip(tbs, probs)]
            ts = [t - _dot(_bf(x), tb) for t, x, tb in zip(ts, xs, tbs)]
        ainvs = [_bf(t) for t in ts]
        egs = [jnp.exp(pr["gcol"]) for pr in probs]
        rhs = [_bf(jnp.concatenate([pr["bcol"] * pr["v"], (pr["bcol"] * eg) * pr["k"]], axis=-1))
               for pr, eg in zip(probs, egs)]
        uwks = [_bf(_dot(ai, r)) for ai, r in zip(ainvs, rhs)]
        intra = [_dot(pr["qkm"], uwk) for pr, uwk in zip(probs, uwks)]
        for pr, eg, uwk, it in zip(probs, egs, uwks, intra):
            d, r0, rows, gcol = pr["d"], pr["r0"], pr["rows"], pr["gcol"]
            oi_scr[d, rows, :] = it[:, :HEAD_W]
            qg_scr[d, rows, :] = _bf(pr["q"] * eg - it[:, HEAD_W:])
            glast = jnp.concatenate(
                [jnp.broadcast_to(gcol[j * c + (c - 1 if d == 0 else 0):j * c + (c if d == 0 else 1), :], (c, 1))
                 for j in range(SCAN_G)], axis=0)
            kdb = _bf(pr["k"] * jnp.exp(glast - gcol))
            for j in range(SCAN_G):
                cs = slice(j * c, (j + 1) * c)
                crow = pl.ds(r0 + j * c, c)
                kd_uwk = _dot_tn(kdb[cs], uwk[cs])
                b_scr[d, crow, :] = kd_uwk[:, :HEAD_W]
                m_scr[d, crow, :] = _bf(kd_uwk[:, HEAD_W:])
        return carry

    lax.fori_loop(0, n_rows // (gr * PREP_PAR), prep_groups, 0)

    def scan_step(s, d, ci, state):
        r0 = s * seq_len + ci * c
        if not isinstance(r0, int):
            r0 = pl.multiple_of(r0, c)
        rows = pl.ds(r0, c)
        sb = _bf(state)
        oi_scr[d, rows, :] = oi_scr[d, rows, :] + _dot(qg_scr[d, rows, :], sb)
        last = r0 + (c - 1) if d == 0 else r0
        egl = jnp.exp(gb_ref[pl.ds(last, 1), d:d + 1])
        return egl * state - _dot(m_scr[d, rows, :], sb) + b_scr[d, rows, :]

    def init_state(s, d):
        return s0_ref[s, d] if has_s0 else jnp.zeros((c, c), F32)

    chains = [(s, d) for s in range(n_seq) for d in range(2)]
    states = [init_state(s, d) for s, d in chains]
    if nc <= 8:
        for t in range(nc):
            states = [scan_step(s, d, t if d == 0 else nc - 1 - t, st) for (s, d), st in zip(chains, states)]
    else:
        def body(t, sts):
            return tuple(scan_step(s, d, t if d == 0 else nc - 1 - t, st) for (s, d), st in zip(chains, sts))
        states = lax.fori_loop(0, nc, body, tuple(states))
    for (s, d), st in zip(chains, states):
        s_ref[s, d] = st
    o_ref[...] = oi_scr[0] + oi_scr[1]


def _gdn_scan_call(q, k, v, gb, s0, *, row0, n_seq_total, seq_len, seqs_per_step):
    n_rows = seqs_per_step * seq_len
    assert n_rows % (SCAN_G * GDN_CHUNK * PREP_PAR) == 0 and row0 % n_rows == 0
    blk0 = row0 // n_rows
    has_s0 = s0 is not None
    kern = functools.partial(_gdn_scan_kernel, n_seq=seqs_per_step, seq_len=seq_len, has_s0=has_s0)
    hm = pl.BlockSpec((None, n_rows, HEAD_W), lambda s, h: (h, blk0 + s, 0))
    st_spec = pl.BlockSpec((seqs_per_step, 2, None, HEAD_W, HEAD_W), lambda s, h: (s, 0, h, 0, 0))
    in_specs = [hm, hm, hm, pl.BlockSpec((None, n_rows, LANES), lambda s, h: (h, blk0 + s, 0))]
    args = [q, k, v, gb]
    if has_s0:
        in_specs.append(st_spec)
        args.append(s0)
    return pl.pallas_call(
        kern,
        grid=(n_seq_total // seqs_per_step, N_HEADS),
        in_specs=in_specs,
        out_specs=[pl.BlockSpec((None, n_rows, HEAD_W), lambda s, h: (h, s, 0)), st_spec],
        out_shape=[jax.ShapeDtypeStruct((N_HEADS, n_seq_total * seq_len, HEAD_W), F32),
                   jax.ShapeDtypeStruct((n_seq_total, 2, N_HEADS, HEAD_W, HEAD_W), F32)],
        scratch_shapes=[
            pltpu.VMEM((2, n_rows, HEAD_W), F32),
            pltpu.VMEM((2, n_rows, HEAD_W), BF16),
            pltpu.VMEM((2, n_rows, HEAD_W), BF16),
            pltpu.VMEM((2, n_rows, HEAD_W), F32),
        ],
        compiler_params=pltpu.CompilerParams(dimension_semantics=("parallel", "parallel"),
                                             vmem_limit_bytes=VMEM_LIMIT),
        name="gdn_scan",
    )(*args)


ONES_ROWS = 16


KEY_CHUNK = 1024


def _attend_all(tasks):
    outs = []
    sts = [_dot(k, tasks[0][0]) for k in tasks[0][1]]
    for i, (_, _, vts) in enumerate(tasks):
        mx = None
        for st in sts:
            m = jnp.max(st, axis=0, keepdims=True)
            mx = m if mx is None else jnp.maximum(mx, m)
        nxt = []
        acc = None
        for c, (st, vt) in enumerate(zip(sts, vts)):
            if i + 1 < len(tasks):
                nxt.append(_dot(tasks[i + 1][1][c], tasks[i + 1][0]))
            t = _dot(vt, _bf(jnp.exp2(st - mx)))
            acc = t if acc is None else acc + t
        outs.append(acc[:HEAD_W] * (1.0 / acc[HEAD_W:HEAD_W + 1]))
        sts = nxt
    return outs


def _key_pieces(k_ref, cache_ref):
    n = k_ref.shape[0]
    kc = min(KEY_CHUNK, n)
    own = [k_ref[c * kc:(c + 1) * kc, :] for c in range(n // kc)]
    return own if cache_ref is None else [cache_ref[...]] + own


def _value_pieces(vt_ref, cache_ref, rows):
    def with_ones(x):
        return jnp.concatenate([x, jnp.ones((ONES_ROWS, x.shape[1]), x.dtype)], axis=0)

    n = vt_ref.shape[1]
    kc = min(KEY_CHUNK, n)
    own = [with_ones(vt_ref[rows, c * kc:(c + 1) * kc]) for c in range(n // kc)]
    return own if cache_ref is None else [with_ones(cache_ref[rows, :])] + own


def _attn_kernel(*refs, has_cache, lam_init):
    if has_cache:
        (dqt_ref, dk_ref, dvt_ref, gqt_ref, gk_ref, gvt_ref, cdk_ref, cdvt_ref, cgk_ref, cgvt_ref,
         lp_ref, dn_ref, b64_ref, ob_ref, od_ref) = refs
    else:
        dqt_ref, dk_ref, dvt_ref, gqt_ref, gk_ref, gvt_ref, lp_ref, dn_ref, b64_ref, ob_ref, od_ref = refs
        cdk_ref = cdvt_ref = cgk_ref = cgvt_ref = None

    lp = lp_ref[...]
    lam = (jnp.exp(jnp.sum(lp[0:1] * lp[1:2], axis=-1, keepdims=True))
           - jnp.exp(jnp.sum(lp[2:3] * lp[3:4], axis=-1, keepdims=True)) + lam_init)

    chan = lax.broadcasted_iota(jnp.int32, (BRANCH_W, 1), 0)
    chan_h = lax.broadcasted_iota(jnp.int32, (LANES, 1), 0)
    dqt = dqt_ref[...]
    dks = _key_pieces(dk_ref, cdk_ref)
    gks = _key_pieces(gk_ref, cgk_ref)
    tasks = []
    for h in range(N_HEADS):
        vts = _value_pieces(dvt_ref, cdvt_ref, slice(h * HEAD_W, (h + 1) * HEAD_W))
        for m in range(2):
            lo = h * HEAD_W + m * DIFF_DQK
            tasks.append((dqt * jnp.where((chan >= lo) & (chan < lo + DIFF_DQK), 1.0, 0.0).astype(BF16), dks, vts))
    kv_vts = [_value_pieces(gvt_ref, cgvt_ref, slice(g * HEAD_W, (g + 1) * HEAD_W)) for g in range(GQA_KV_HEADS)]
    for half in range(2):
        qh = gqt_ref[half * LANES:(half + 1) * LANES, :]
        for g in range(GQA_KV_HEADS):
            tasks.append((qh * jnp.where(chan_h // HEAD_W == g, 1.0, 0.0).astype(BF16), gks, kv_vts[g]))
    outs = _attend_all(tasks)

    o = jnp.concatenate([outs[2 * h] - lam * outs[2 * h + 1] for h in range(N_HEADS)], axis=0).T
    b64 = b64_ref[...]
    for j in range(BRANCH_W // LANES):
        sl = slice(j * LANES, (j + 1) * LANES)
        x = o[:, sl]
        ms = _group_sum(x * x, b64) * (1.0 / HEAD_W)
        ob_ref[:, sl] = _bf(x * lax.rsqrt(ms + EPS) * dn_ref[:, sl] * (1.0 - lam_init))
    od_ref[...] = _bf(jnp.concatenate(outs[2 * N_HEADS:], axis=0).T)


def _attn_ctx_call(dqt, dk, dvt, gqt, gk, gvt, lp, dn, b64, *, layer, n_seq, seq_len, lam_init):
    kern = functools.partial(_attn_kernel, has_cache=False, lam_init=lam_init)

    def rows(width):
        return pl.BlockSpec((seq_len, width), lambda s: (s, 0))

    def cols(height):
        return pl.BlockSpec((height, seq_len), lambda s: (0, s))

    def const(shape):
        return pl.BlockSpec(shape, lambda s: (0,) * len(shape))

    return pl.pallas_call(
        kern,
        grid=(n_seq,),
        in_specs=[cols(256), rows(256), cols(256), cols(256), rows(128), cols(128),
                  _layer_spec((4, DIFF_DQK), layer), _layer_spec((1, BRANCH_W), layer), const((LANES, LANES))],
        out_specs=[rows(256), rows(256)],
        out_shape=[jax.ShapeDtypeStruct((n_seq * seq_len, BRANCH_W), BF16)] * 2,
        compiler_params=pltpu.CompilerParams(dimension_semantics=("parallel",), vmem_limit_bytes=VMEM_LIMIT),
        name="attn_ctx",
    )(dqt, dk, dvt, gqt, gk, gvt, lp, dn, b64)


def _attn_lat_call(dqt, dk, dvt, gqt, gk, gvt, cdk, cdvt, cgk, cgvt, lp, dn, b64, *, layer, row0, n_seq, seq_len,
                   lam_init):
    kern = functools.partial(_attn_kernel, has_cache=True, lam_init=lam_init)
    tq = ATTN_TQ
    qpb = seq_len // tq
    q0 = row0 // tq
    s0 = row0 // seq_len
    past = cdk.shape[1]

    def qcols(height):
        return pl.BlockSpec((height, tq), lambda b, i: (0, q0 + b * qpb + i))

    def kvrows(width):
        return pl.BlockSpec((seq_len, width), lambda b, i: (s0 + b, 0))

    def kvcols(height):
        return pl.BlockSpec((height, seq_len), lambda b, i: (0, s0 + b))

    def cache(width):
        return pl.BlockSpec((None, past, width), lambda b, i: (b, 0, 0))

    def cache_t(height):
        return pl.BlockSpec((None, height, past), lambda b, i: (b, 0, 0))

    def const(shape):
        return pl.BlockSpec(shape, lambda b, i: (0,) * len(shape))

    out = pl.BlockSpec((tq, BRANCH_W), lambda b, i: (b * qpb + i, 0))
    return pl.pallas_call(
        kern,
        grid=(n_seq, qpb),
        in_specs=[qcols(256), kvrows(256), kvcols(256), qcols(256), kvrows(128), kvcols(128),
                  cache(256), cache_t(256), cache(128), cache_t(128),
                  _layer_spec((4, DIFF_DQK), layer), _layer_spec((1, BRANCH_W), layer), const((LANES, LANES))],
        out_specs=[out, out],
        out_shape=[jax.ShapeDtypeStruct((n_seq * seq_len, BRANCH_W), BF16)] * 2,
        compiler_params=pltpu.CompilerParams(dimension_semantics=("parallel", "arbitrary"),
                                             vmem_limit_bytes=VMEM_LIMIT),
        name="attn_lat",
    )(dqt, dk, dvt, gqt, gk, gvt, cdk, cdvt, cgk, cgvt, lp, dn, b64)


def _merge_kernel(h_ref, mod_ref, n1_ref, oac_ref, oal_ref, gate_ref, gn_ref, obc_ref, obl_ref, oc_ref,
                  odc_ref, odl_ref, wmg_ref, wb_ref, wo_ref, b64_ref, out_ref, *, tiles_ctx):
    is_ctx = pl.program_id(0) < tiles_ctx
    h = h_ref[...]
    xm = _bf(_norm_mod(h, n1_ref[...], mod_ref[:, D_MODEL:2 * D_MODEL], mod_ref[:, 0:D_MODEL]))
    g1 = mod_ref[:, 2 * D_MODEL:3 * D_MODEL]
    b64 = b64_ref[...]
    oa_parts = []
    for j in range(BRANCH_W // LANES):
        sl = slice(j * LANES, (j + 1) * LANES)
        x = jnp.concatenate([jnp.where(is_ctx, oac_ref[2 * j + hh], oal_ref[2 * j + hh])
                             for hh in range(LANES // HEAD_W)], axis=-1)
        ms = _group_sum(x * x, b64) * (1.0 / HEAD_W)
        oa_parts.append(_bf(x * lax.rsqrt(ms + EPS) * gn_ref[:, sl] * _silu(gate_ref[:, sl])))
    branches = (None, jnp.where(is_ctx, obc_ref[...], obl_ref[...]), oc_ref[...],
                jnp.where(is_ctx, odc_ref[...], odl_ref[...]))
    merged = None
    for n in range(4):
        mg = _dot(xm, wmg_ref[:, n * D_MODEL:(n + 1) * D_MODEL])
        if n == 0:
            up = _dot(oa_parts[0], wb_ref[0, 0:LANES, :]) + _dot(oa_parts[1], wb_ref[0, LANES:2 * LANES, :])
        else:
            up = _dot(branches[n], wb_ref[n])
        term = _sigmoid(mg) * up
        merged = term if merged is None else merged + term
    out_ref[...] = h + g1 * _dot(_bf(merged), wo_ref[...])


def _merge_call(h, mods, n1, oa_ctx, oa_lat, gate, gn, ob_ctx, ob_lat, oc, od_ctx, od_lat, wmg, wb, wo, b64, *,
                layer, tiles_ctx, tiles_per_lat):
    t = h.shape[0]
    last_ctx = tiles_ctx - 1

    def mod_row(i):
        return jnp.where(i < tiles_ctx, 0, 1 + (i - tiles_ctx) // tiles_per_lat)

    def rows(width):
        return pl.BlockSpec((TM, width), lambda i: (i, 0))

    def ctx_rows(width):
        return pl.BlockSpec((TM, width), lambda i: (jnp.minimum(i, last_ctx), 0))

    def lat_rows(width):
        return pl.BlockSpec((TM, width), lambda i: (jnp.maximum(i - tiles_ctx, 0), 0))

    return pl.pallas_call(
        functools.partial(_merge_kernel, tiles_ctx=tiles_ctx),
        grid=(t // TM,),
        in_specs=[
            rows(D_MODEL),
            _mod_spec(layer, mod_row),
            _layer_spec((1, D_MODEL), layer),
            pl.BlockSpec((N_HEADS, TM, HEAD_W), lambda i: (0, jnp.minimum(i, last_ctx), 0)),
            pl.BlockSpec((N_HEADS, TM, HEAD_W), lambda i: (0, jnp.maximum(i - tiles_ctx, 0), 0)),
            rows(BRANCH_W),
            _layer_spec((1, BRANCH_W), layer),
            ctx_rows(BRANCH_W), lat_rows(BRANCH_W), rows(BRANCH_W), ctx_rows(BRANCH_W), lat_rows(BRANCH_W),
            _layer_spec((D_MODEL, 4 * D_MODEL), layer),
            _layer_spec((4, BRANCH_W, D_MODEL), layer),
            _layer_spec((D_MODEL, D_MODEL), layer),
            pl.BlockSpec((LANES, LANES), lambda i: (0, 0)),
        ],
        out_specs=rows(D_MODEL),
        out_shape=jax.ShapeDtypeStruct((t, D_MODEL), F32),
        compiler_params=pltpu.CompilerParams(dimension_semantics=("parallel",), vmem_limit_bytes=VMEM_LIMIT),
        name="merge",
    )(h, mods, n1, oa_ctx, oa_lat, gate, gn, ob_ctx, ob_lat, oc, od_ctx, od_lat, wmg, wb, wo, b64)


ROUTER_ROWS = 32
ROUTER_E0 = 8


def _moe_kernel(h_ref, mod_ref, n2_ref, wr_ref, br_ref, w1_ref, w3_ref, w2_ref, out_ref, xm_scr, gates_scr, acc_scr):
    e = pl.program_id(1)
    tm = h_ref.shape[0]

    @pl.when(e == 0)
    def _():
        xm = _norm_mod(h_ref[...], n2_ref[...], mod_ref[:, 4 * D_MODEL:5 * D_MODEL],
                       mod_ref[:, 3 * D_MODEL:4 * D_MODEL])
        x_hi = _bf(xm)
        xm_scr[...] = x_hi
        x_lo = _bf(xm - x_hi.astype(F32))
        hi = _dot_nt(wr_ref[...], x_hi)
        lt = (hi[0:ROUTER_ROWS] + hi[ROUTER_ROWS:2 * ROUTER_ROWS]
              + _dot_nt(wr_ref[0:ROUTER_ROWS, :], x_lo) + br_ref[...])
        gl = lt[0:N_GROUPS]
        gidx = lax.broadcasted_iota(jnp.int32, (N_GROUPS, 1), 0)
        gmax = jnp.max(gl, axis=0, keepdims=True)
        gsel = jnp.min(jnp.where(gl == gmax, gidx, N_GROUPS), axis=0, keepdims=True)
        gprob = 1.0 / jnp.sum(jnp.exp(gl - gmax), axis=0, keepdims=True)
        el = lt[ROUTER_E0:ROUTER_E0 + N_EXPERTS]
        eidx = lax.broadcasted_iota(jnp.int32, (N_EXPERTS, 1), 0)
        neg = -jnp.inf
        cand = jnp.where(eidx // EXPERTS_PER_GROUP == gsel, el, neg)
        v1 = jnp.max(cand, axis=0, keepdims=True)
        i1 = jnp.min(jnp.where(cand == v1, eidx, N_EXPERTS), axis=0, keepdims=True)
        cand2 = jnp.where(eidx == i1, neg, cand)
        v2 = jnp.max(cand2, axis=0, keepdims=True)
        i2 = jnp.min(jnp.where(cand2 == v2, eidx, N_EXPERTS), axis=0, keepdims=True)
        ex = jnp.exp(v2 - v1)
        wa = 1.0 / (1.0 + ex)
        gates_t = (jnp.where(eidx == i1, wa, 0.0) + jnp.where(eidx == i2, ex * wa, 0.0)) * gprob
        full = jnp.concatenate([gates_t, jnp.zeros((LANES - N_EXPERTS, tm), F32)], axis=0)
        gates_scr[...] = full.T
        acc_scr[...] = jnp.zeros_like(acc_scr)

    xm = xm_scr[...]
    lane = lax.broadcasted_iota(jnp.int32, (1, LANES), 1)
    gates = gates_scr[...]
    hds = []
    for j in range(MOE_EPS):
        h1 = _dot(xm, w1_ref[j])
        h3 = _dot(xm, w3_ref[j])
        gate = jnp.sum(jnp.where(lane == e * MOE_EPS + j, gates, 0.0), axis=-1, keepdims=True)
        hds.append(_bf(_silu(h1) * h3 * gate))
    acc_scr[...] += _dot(jnp.concatenate(hds, axis=-1), w2_ref[...].reshape(MOE_EPS * D_EXPERT, D_MODEL))

    @pl.when(e == N_EXPERTS // MOE_EPS - 1)
    def _():
        out_ref[...] = h_ref[...] + mod_ref[:, 5 * D_MODEL:6 * D_MODEL] * acc_scr[...]


def _moe_call(h, mods, n2, wr, br, w1, w3, w2, *, layer, ctx_rows, lat_len):
    t = h.shape[0]
    tm = TM_MOE
    tiles_ctx = ctx_rows // tm
    tiles_per_lat = lat_len // tm

    def mod_row(i):
        return jnp.where(i < tiles_ctx, 0, 1 + (i - tiles_ctx) // tiles_per_lat)

    def expert(shape):
        return pl.BlockSpec((None, MOE_EPS) + shape, lambda i, e: (layer, e, 0, 0))

    return pl.pallas_call(
        _moe_kernel,
        grid=(t // tm, N_EXPERTS // MOE_EPS),
        in_specs=[
            pl.BlockSpec((tm, D_MODEL), lambda i, e: (i, 0)),
            _mod_spec(layer, mod_row),
            _layer_spec((1, D_MODEL), layer),
            _layer_spec((2 * ROUTER_ROWS, D_MODEL), layer),
            _layer_spec((ROUTER_ROWS, 1), layer),
            expert((D_MODEL, D_EXPERT)),
            expert((D_MODEL, D_EXPERT)),
            expert((D_EXPERT, D_MODEL)),
        ],
        out_specs=pl.BlockSpec((tm, D_MODEL), lambda i, e: (i, 0)),
        out_shape=jax.ShapeDtypeStruct((t, D_MODEL), F32),
        scratch_shapes=[pltpu.VMEM((tm, D_MODEL), BF16), pltpu.VMEM((tm, LANES), F32),
                        pltpu.VMEM((tm, D_MODEL), F32)],
        compiler_params=pltpu.CompilerParams(dimension_semantics=("parallel", "arbitrary"),
                                             vmem_limit_bytes=VMEM_LIMIT),
        name="moe",
    )(h, mods, n2, wr, br, w1, w3, w2)


def _final_norm_kernel(h_ref, g_ref, o_ref):
    h = h_ref[...]
    o_ref[...] = h * lax.rsqrt(jnp.mean(h * h, axis=-1, keepdims=True) + EPS) * g_ref[...]


def _final_norm_call(h, g, *, row0, n_rows):
    blk0 = row0 // TM
    return pl.pallas_call(
        _final_norm_kernel,
        grid=(n_rows // TM,),
        in_specs=[pl.BlockSpec((TM, D_MODEL), lambda i: (blk0 + i, 0)), pl.BlockSpec((1, D_MODEL), lambda i: (0, 0))],
        out_specs=pl.BlockSpec((TM, D_MODEL), lambda i: (i, 0)),
        out_shape=jax.ShapeDtypeStruct((n_rows, D_MODEL), F32),
        compiler_params=pltpu.CompilerParams(dimension_semantics=("parallel",)),
        name="final_norm",
    )(h, g)


def _mixer_weights(w_in):
    depth, d, _ = w_in.shape
    ab_cols = []
    for h in range(N_HEADS):
        ab_cols += [_O_A + h, _O_A + N_HEADS + h, _O_B + h, _O_B + N_HEADS + h]
    ab = jnp.take(w_in[:, :, _O_A:_O_DQ], jnp.asarray([c - _O_A for c in ab_cols], dtype=jnp.int32), axis=2)
    parts = [w_in[:, :, _O_QKV:_O_A], ab, jnp.zeros((depth, d, LANES - len(ab_cols)), w_in.dtype),
             w_in[:, :, _O_DQ:_O_GQ]]
    for h in (0, 2, 1, 3):
        parts.append(w_in[:, :, _O_GQ + h * HEAD_W:_O_GQ + (h + 1) * HEAD_W])
    parts.append(w_in[:, :, _O_GK:_O_MG])
    w_mix = _bf(jnp.concatenate(parts, axis=2))
    assert w_mix.shape[2] == C_END
    return w_mix


def _rope_tables(lat_len):
    pos = jnp.arange(lat_len, dtype=jnp.int32)
    row = (pos // GRID_W).astype(F32)[:, None]
    col = (pos % GRID_W).astype(F32)[:, None]
    lane = jnp.arange(LANES, dtype=jnp.int32)[None, :]
    tabs = []
    for head_w in (DIFF_DQK, HEAD_W):
        half = head_w // 2
        nf = half // 2
        p = lane % head_w
        f = (p % half) % nf
        inv = ROPE_THETA ** (-f.astype(F32) / nf)
        ang = jnp.where(p < half, row, col) * inv
        sign = jnp.where((p % half) < nf, -1.0, 1.0)
        tabs += [jnp.cos(ang), jnp.sin(ang) * sign]
    tab = jnp.stack(tabs)
    ident = jnp.stack([jnp.ones((TM, LANES), F32), jnp.zeros((TM, LANES), F32)] * 2)
    return jnp.concatenate([ident, tab], axis=1)


def _chunk_tri(n, chunk, upper):
    i = jnp.arange(n)[:, None]
    j = jnp.arange(n)[None, :]
    same = (i // chunk) == (j // chunk)
    tri = (j >= i) if upper else (j <= i)
    return jnp.where(same & tri, 1.0, 0.0).astype(BF16)


def kernel(x_prompt, x_sample, state_gdn, cache_diff_k, cache_diff_v, cache_gqa_k, cache_gqa_v, c, c_ctx,
           ada_w, ada_b, norm1, norm2, w_in, gdn_conv, gdn_a_log, gdn_dt_bias, gdn_norm, diff_lambda, diff_norm,
           sgu_norm, sgu_ws, sgu_b, gqa_qnorm, gqa_knorm, w_branch, w_out, moe_wg, moe_bg, moe_we, moe_be,
           moe_w1, moe_w3, moe_w2, final_norm):
    nb, seq, d = x_prompt.shape
    nl, lat_len, _ = x_sample.shape
    depth = ada_w.shape[0]
    past = cache_diff_k.shape[2]
    ctx_rows = nb * seq
    lat_rows = nl * lat_len
    assert d == D_MODEL and seq % PREP_R == 0 and lat_len % TM == 0 and ctx_rows % lat_len == 0
    assert ctx_rows % TM == 0 and 1 + nl <= SUBLANES and lat_len % GRID_W == 0
    assert ctx_rows % TM_MOE == 0 and lat_len % TM_MOE == 0
    tiles_ctx = ctx_rows // TM
    tiles_per_lat = lat_len // TM
    tile_kw = dict(tiles_ctx=tiles_ctx, tiles_per_lat=tiles_per_lat)

    lane = jnp.arange(LANES)
    b64 = jnp.where((lane[:, None] // HEAD_W) == (lane[None, :] // HEAD_W), 1.0, 0.0).astype(BF16)
    tril = _chunk_tri(PREP_R, GDN_CHUNK, upper=False)
    triu = _chunk_tri(PREP_R, GDN_CHUNK, upper=True)
    rope_tab = _rope_tables(lat_len)

    w_mix = _mixer_weights(w_in)
    w_mg = _bf(w_in[:, :, _O_MG:])
    head_perm = jnp.asarray([r for h in (0, 2, 1, 3) for r in range(h * HEAD_W, (h + 1) * HEAD_W)], dtype=jnp.int32)
    wb = _bf(jnp.concatenate([w_branch[:, :3], jnp.take(w_branch[:, 3], head_perm, axis=1)[:, None]], axis=1))
    wo = _bf(w_out)
    w1, w3, w2 = _bf(moe_w1), _bf(moe_w3), _bf(moe_w2)
    wr = jnp.zeros((depth, ROUTER_ROWS, d), F32)
    wr = wr.at[:, 0:N_GROUPS].set(jnp.swapaxes(moe_wg, 1, 2))
    wr = wr.at[:, ROUTER_E0:ROUTER_E0 + N_EXPERTS].set(jnp.swapaxes(moe_we, 1, 2))
    wr_hi = _bf(wr)
    wr = jnp.concatenate([wr_hi, _bf(wr - wr_hi.astype(F32))], axis=1)
    br = jnp.zeros((depth, ROUTER_ROWS, 1), F32)
    br = br.at[:, 0:N_GROUPS, 0].set(moe_bg).at[:, ROUTER_E0:ROUTER_E0 + N_EXPERTS, 0].set(moe_be)
    conv_w = jnp.zeros((depth, SUBLANES, 3 * BRANCH_W), F32).at[:, :CONV_K].set(gdn_conv)
    ab_lane = jnp.zeros((depth, 1, LANES), F32)
    alog_row, dtb_row = ab_lane, ab_lane
    for h in range(N_HEADS):
        for dd in range(2):
            alog_row = alog_row.at[:, 0, 4 * h + dd].set(gdn_a_log[:, dd, h])
            dtb_row = dtb_row.at[:, 0, 4 * h + dd].set(gdn_dt_bias[:, dd, h])
    sgw = _bf(sgu_ws)
    sgb = jnp.repeat(jnp.swapaxes(sgu_b, 1, 2), HEAD_W, axis=2)
    gdn_gn = jnp.tile(gdn_norm, (1, N_HEADS))[:, None, :]
    diff_gn = jnp.tile(diff_norm, (1, N_HEADS))[:, None, :]
    gqn = jnp.tile(gqa_qnorm, (1, LANES // HEAD_W))[:, None, :]
    gkn = jnp.tile(gqa_knorm, (1, LANES // HEAD_W))[:, None, :]

    cond = jnp.zeros((SUBLANES, d), F32).at[0].set(c_ctx).at[1:1 + nl].set(c)
    mods = _ada_call(cond, ada_w, ada_b)[:, :, None, :]
    n1 = norm1[:, None, :]
    n2 = norm2[:, None, :]
    sgn = sgu_norm[:, None, :]

    h = jnp.concatenate([x_prompt.reshape(ctx_rows, d), x_sample.reshape(lat_rows, d)], axis=0)
    new_s, new_dk, new_dv, new_gk, new_gv = [], [], [], [], []
    for l in range(depth):
        lam_init = 0.8 - 0.6 * math.exp(-0.3 * l)
        (gqkv, ggate, gab, dqt, dk32, dv32, dk16, dvt, oc, gqt, gk32, gv32, gk16, gvt) = _inproj_call(
            h, mods, n1, w_mix, rope_tab, b64, sgn, sgw, sgb, gqn, gkn, layer=l, **tile_kw)

        qh, kh, vh, gb = _gdn_prep_call(gqkv, gab, conv_w, alog_row, dtb_row, b64, tril, triu, layer=l,
                                        ctx_rows=ctx_rows, ctx_len=seq, lat_len=lat_len)
        oa_ctx, s_ctx = _gdn_scan_call(qh, kh, vh, gb, None, row0=0, n_seq_total=nb, seq_len=seq,
                                       seqs_per_step=8 if nb % 8 == 0 else 1)
        oa_lat, _ = _gdn_scan_call(qh, kh, vh, gb, state_gdn[:, l], row0=ctx_rows, n_seq_total=nl,
                                   seq_len=lat_len, seqs_per_step=1)

        ob_ctx, od_ctx = _attn_ctx_call(dqt, dk16, dvt, gqt, gk16, gvt, diff_lambda, diff_gn, b64, layer=l,
                                        n_seq=nb, seq_len=seq, lam_init=lam_init)
        ob_lat, od_lat = _attn_lat_call(
            dqt, dk16, dvt, gqt, gk16, gvt,
            _bf(cache_diff_k[:, l].reshape(nl, past, BRANCH_W)),
            _bf(jnp.swapaxes(cache_diff_v[:, l].reshape(nl, past, BRANCH_W), 1, 2)),
            _bf(cache_gqa_k[:, l].reshape(nl, past, LANES)),
            _bf(jnp.swapaxes(cache_gqa_v[:, l].reshape(nl, past, LANES), 1, 2)),
            diff_lambda, diff_gn, b64, layer=l, row0=ctx_rows, n_seq=nl, seq_len=lat_len, lam_init=lam_init)

        h = _merge_call(h, mods, n1, oa_ctx, oa_lat, ggate, gdn_gn, ob_ctx, ob_lat, oc, od_ctx, od_lat,
                        w_mg, wb, wo, b64, layer=l, **tile_kw)
        h = _moe_call(h, mods, n2, wr, br, w1, w3, w2, layer=l, ctx_rows=ctx_rows, lat_len=lat_len)

        new_s.append(s_ctx)
        new_dk.append(dk32[:ctx_rows].reshape(nb, seq, N_HEADS, HEAD_W))
        new_dv.append(dv32[:ctx_rows].reshape(nb, seq, N_HEADS, HEAD_W))
        new_gk.append(gk32[:ctx_rows].reshape(nb, seq, GQA_KV_HEADS, HEAD_W))
        new_gv.append(gv32[:ctx_rows].reshape(nb, seq, GQA_KV_HEADS, HEAD_W))

    fg = final_norm[None, :]
    y_ctx = _final_norm_call(h, fg, row0=0, n_rows=ctx_rows)
    y_lat = _final_norm_call(h, fg, row0=ctx_rows, n_rows=lat_rows)
    return (y_ctx.reshape(nb, seq, d), y_lat.reshape(nl, lat_len, d),
            jnp.stack(new_s, axis=1), jnp.stack(new_dk, axis=1), jnp.stack(new_dv, axis=1),
            jnp.stack(new_gk, axis=1), jnp.stack(new_gv, axis=1))
```

```python
import functools
import math

import jax
import jax.numpy as jnp
from jax import lax
from jax.experimental import pallas as pl
from jax.experimental.pallas import tpu as pltpu

F32 = jnp.float32
BF16 = jnp.bfloat16

GRID_W = 64
ROPE_THETA = 10000.0
EPS = 1e-6
D_MODEL = 1024
BRANCH_W = 256
HEAD_W = 64
DIFF_DQK = 32
N_HEADS = 4
GQA_KV_HEADS = 2
GDN_CHUNK = 64
CONV_K = 5
SGU_CHUNK = 128
SGU_GROUPS = 4
N_GROUPS = 4
EXPERTS_PER_GROUP = 4
N_EXPERTS = 16
D_EXPERT = 256
LOG2E = 1.4426950408889634
DIFF_SCALE = DIFF_DQK ** -0.5 * LOG2E
GQA_SCALE = HEAD_W ** -0.5 * LOG2E
GDN_Q_SCALE = HEAD_W ** -0.5

LANES = 128
SUBLANES = 8
VMEM_LIMIT = 56 * 1024 * 1024

_IN_WIDTHS = (768, 256, 8, 8, 256, 256, 256, 256, 256, 256, 128, 128, 4096)
_IN_OFFS = tuple(sum(_IN_WIDTHS[:i]) for i in range(len(_IN_WIDTHS)))
(_O_QKV, _O_GATE, _O_A, _O_B, _O_DQ, _O_DK, _O_DV, _O_SU, _O_SV, _O_GQ, _O_GK, _O_GV, _O_MG) = _IN_OFFS
MIX_COLS = _O_MG

C_QKV, C_GATE, C_AB, C_DQ, C_DK, C_DV, C_SU, C_SV, C_GQ, C_GK, C_GV, C_END = (
    0, 768, 1024, 1152, 1408, 1664, 1920, 2176, 2432, 2688, 2816, 2944)

TM = 512
TM_MOE = 1024
MOE_EPS = 4
PREP_R = 256
HALO = SUBLANES
SCAN_G = 4
PREP_PAR = 4
ATTN_TQ = 256


def _bf(x):
    return x.astype(BF16)


def _dot(a, b):
    return jnp.dot(a, b, preferred_element_type=F32)


def _dot_nt(a, b):
    return lax.dot_general(a, b, (((1,), (1,)), ((), ())), preferred_element_type=F32)


def _dot_tn(a, b):
    return lax.dot_general(a, b, (((0,), (0,)), ((), ())), preferred_element_type=F32)


def _split_dot(x, m, terms):
    acc = None
    r = x
    for t in range(terms):
        p = _bf(r)
        d = _dot(p, m)
        acc = d if acc is None else acc + d
        if t + 1 < terms:
            r = r - p.astype(F32)
    return acc


def _split_dot_left(m, x, terms):
    acc = None
    r = x
    for t in range(terms):
        p = _bf(r)
        d = _dot(m, p)
        acc = d if acc is None else acc + d
        if t + 1 < terms:
            r = r - p.astype(F32)
    return acc


def _sigmoid(x):
    return 1.0 / (1.0 + jnp.exp(-x))


def _silu(x):
    return x * _sigmoid(x)


def _norm_mod(h, gamma, scale, shift):
    ms = jnp.mean(h * h, axis=-1, keepdims=True)
    return (h * lax.rsqrt(ms + EPS) * gamma) * (1.0 + scale) + shift


def _group_sum(sq, b64):
    return _split_dot(sq, b64, 2)


def _rope(x, cos, sin_signed, nf):
    lane = lax.broadcasted_iota(jnp.int32, (1, LANES), 1)
    first = (lane % (2 * nf)) < nf
    partner = jnp.where(first, pltpu.roll(x, LANES - nf, 1), pltpu.roll(x, nf, 1))
    return x * cos + partner * sin_signed


def _layer_spec(shape, layer):
    return pl.BlockSpec((None,) + tuple(shape), lambda *_: (layer,) + (0,) * len(shape))


def _mod_spec(layer, mod_row):
    return pl.BlockSpec((None, None, 1, 6 * D_MODEL), lambda i, *_: (layer, mod_row(i), 0, 0))


def _ada_kernel(c_ref, w_ref, b_ref, o_ref):
    x = _silu(c_ref[...])
    o_ref[...] = jnp.dot(x, w_ref[...], precision=lax.Precision.HIGHEST,
                         preferred_element_type=F32) + b_ref[...]


def _ada_call(cond, ada_w, ada_b):
    n_layers, d, n_out = ada_w.shape
    tn = 1536
    return pl.pallas_call(
        _ada_kernel,
        grid=(n_layers, n_out // tn),
        in_specs=[
            pl.BlockSpec((SUBLANES, d), lambda l, j: (0, 0)),
            pl.BlockSpec((None, d, tn), lambda l, j: (l, 0, j)),
            pl.BlockSpec((None, 1, tn), lambda l, j: (l, 0, j)),
        ],
        out_specs=pl.BlockSpec((None, SUBLANES, tn), lambda l, j: (l, 0, j)),
        out_shape=jax.ShapeDtypeStruct((n_layers, SUBLANES, n_out), F32),
        compiler_params=pltpu.CompilerParams(vmem_limit_bytes=VMEM_LIMIT),
        name="ada_mod",
    )(cond, ada_w, ada_b.reshape(n_layers, 1, n_out))


def _inproj_kernel(h_ref, mod_ref, n1_ref, w_ref, rope_ref, b64_ref, sgn_ref, sgw_ref, sgb_ref, gqn_ref, gkn_ref,
                   gqkv_ref, ggate_ref, gab_ref, dqt_ref, dk32_ref, dv32_ref, dk16_ref, dvt_ref, oc_ref,
                   gqt_ref, gk32_ref, gv32_ref, gk16_ref, gvt_ref):
    tm = h_ref.shape[0]
    xm = _bf(_norm_mod(h_ref[...], n1_ref[...], mod_ref[:, D_MODEL:2 * D_MODEL], mod_ref[:, 0:D_MODEL]))

    def proj(a, b):
        return _dot(xm, w_ref[:, a:b])

    gqkv_ref[...] = proj(C_QKV, C_GATE)
    ggate_ref[...] = proj(C_GATE, C_AB)
    gab_ref[...] = proj(C_AB, C_DQ)

    cos_d, sin_d, cos_g, sin_g = rope_ref[0], rope_ref[1], rope_ref[2], rope_ref[3]
    b64 = b64_ref[...]

    for j in range(BRANCH_W // LANES):
        sl = slice(j * LANES, (j + 1) * LANES)
        q = proj(C_DQ + j * LANES, C_DQ + (j + 1) * LANES)
        dqt_ref[sl, :] = _bf((_rope(q, cos_d, sin_d, DIFF_DQK // 4) * DIFF_SCALE).T)
        k = _rope(proj(C_DK + j * LANES, C_DK + (j + 1) * LANES), cos_d, sin_d, DIFF_DQK // 4)
        dk32_ref[:, sl] = k
        dk16_ref[:, sl] = _bf(k)
    v = proj(C_DV, C_SU)
    dv32_ref[...] = v
    dvt_ref[...] = _bf(v.T)

    u = proj(C_SU, C_SV)
    sv = proj(C_SV, C_GQ)
    vn = _bf(sv * lax.rsqrt(jnp.mean(sv * sv, axis=-1, keepdims=True) + EPS) * sgn_ref[...])
    group = lax.broadcasted_iota(jnp.int32, (1, BRANCH_W), 1) // HEAD_W
    for r in range(tm // SGU_CHUNK):
        rows = slice(r * SGU_CHUNK, (r + 1) * SGU_CHUNK)
        vc = vn[rows]
        z = sgb_ref[...]
        for g in range(SGU_GROUPS):
            z = z + jnp.where(group == g, _dot(sgw_ref[g], vc), 0.0)
        oc_ref[rows, :] = _bf(u[rows] * z)

    for j in range(BRANCH_W // LANES):
        sl = slice(j * LANES, (j + 1) * LANES)
        q = proj(C_GQ + j * LANES, C_GQ + (j + 1) * LANES)
        qn = q * lax.rsqrt(_group_sum(q * q, b64) * (1.0 / HEAD_W) + EPS) * gqn_ref[...]
        gqt_ref[sl, :] = _bf((_rope(qn, cos_g, sin_g, HEAD_W // 4) * GQA_SCALE).T)
    k = proj(C_GK, C_GV)
    kn = k * lax.rsqrt(_group_sum(k * k, b64) * (1.0 / HEAD_W) + EPS) * gkn_ref[...]
    kr = _rope(kn, cos_g, sin_g, HEAD_W // 4)
    gk32_ref[...] = kr
    gk16_ref[...] = _bf(kr)
    v = proj(C_GV, C_END)
    gv32_ref[...] = v
    gvt_ref[...] = _bf(v.T)


def _inproj_call(h, mods, n1, w_mix, rope_tab, b64, sgn, sgw, sgb, gqn, gkn, *, layer, tiles_ctx, tiles_per_lat):
    t = h.shape[0]
    n_tiles = t // TM

    def mod_row(i):
        return jnp.where(i < tiles_ctx, 0, 1 + (i - tiles_ctx) // tiles_per_lat)

    def rope_blk(i):
        return jnp.where(i < tiles_ctx, 0, 1 + (i - tiles_ctx) % tiles_per_lat)

    def const(shape):
        return pl.BlockSpec(shape, lambda i: (0,) * len(shape))

    def rows(width):
        return pl.BlockSpec((TM, width), lambda i: (i, 0))

    def cols(height):
        return pl.BlockSpec((height, TM), lambda i: (0, i))

    row_major = lambda w, dt: (rows(w), jax.ShapeDtypeStruct((t, w), dt))
    col_major = lambda hgt, dt: (cols(hgt), jax.ShapeDtypeStruct((hgt, t), dt))
    outs = [
        row_major(768, F32), row_major(256, F32), row_major(128, F32),
        col_major(256, BF16), row_major(256, F32), row_major(256, F32),
        row_major(256, BF16), col_major(256, BF16),
        row_major(256, BF16),
        col_major(256, BF16), row_major(128, F32), row_major(128, F32),
        row_major(128, BF16), col_major(128, BF16),
    ]
    return pl.pallas_call(
        _inproj_kernel,
        grid=(n_tiles,),
        in_specs=[
            rows(D_MODEL),
            _mod_spec(layer, mod_row),
            _layer_spec((1, D_MODEL), layer),
            _layer_spec((D_MODEL, C_END), layer),
            pl.BlockSpec((4, TM, LANES), lambda i: (0, rope_blk(i), 0)),
            const((LANES, LANES)),
            _layer_spec((1, BRANCH_W), layer),
            _layer_spec((SGU_GROUPS, SGU_CHUNK, SGU_CHUNK), layer),
            _layer_spec((SGU_CHUNK, BRANCH_W), layer),
            _layer_spec((1, LANES), layer),
            _layer_spec((1, LANES), layer),
        ],
        out_specs=[spec for spec, _ in outs],
        out_shape=[shape for _, shape in outs],
        compiler_params=pltpu.CompilerParams(dimension_semantics=("parallel",), vmem_limit_bytes=VMEM_LIMIT),
        name="inproj",
    )(h, mods, n1, w_mix, rope_tab, b64, sgn, sgw, sgb, gqn, gkn)


def _gdn_prep_kernel(x_ref, xp_ref, xn_ref, ab_ref, cw_ref, alog_ref, dtb_ref, b64_ref, tril_ref, triu_ref,
                     q_ref, k_ref, v_ref, gb_ref, *, ctx_blocks, ctx_bps, lat_bps):
    r = x_ref.shape[0]
    i = pl.program_id(0)
    is_ctx = i < ctx_blocks
    pos = jnp.where(is_ctx, i % ctx_bps, (i - ctx_blocks) % lat_bps)
    bps = jnp.where(is_ctx, ctx_bps, lat_bps)
    prev = jnp.where(pos > 0, xp_ref[...], 0.0)
    nxt = jnp.where(pos < bps - 1, xn_ref[...], 0.0)
    win = jnp.concatenate([prev, x_ref[...], nxt], axis=0)
    n = r + 2 * HALO
    y = None
    for j in range(CONV_K):
        s = (CONV_K // 2 - j) % n
        shifted = win if s == 0 else pltpu.roll(win, s, 0)
        term = cw_ref[j:j + 1, :] * shifted[HALO:HALO + r]
        y = term if y is None else y + term
    y = _silu(y)

    b64 = b64_ref[...]
    for part, ref, scale in ((0, q_ref, GDN_Q_SCALE), (1, k_ref, 1.0)):
        for j in range(BRANCH_W // LANES):
            x = y[:, part * BRANCH_W + j * LANES: part * BRANCH_W + (j + 1) * LANES]
            xn = x * (lax.rsqrt(_group_sum(x * x, b64) + EPS) * scale)
            for hh in range(LANES // HEAD_W):
                ref[j * (LANES // HEAD_W) + hh] = xn[:, hh * HEAD_W:(hh + 1) * HEAD_W]
    for h in range(N_HEADS):
        v_ref[h] = y[:, 2 * BRANCH_W + h * HEAD_W: 2 * BRANCH_W + (h + 1) * HEAD_W]

    ab = ab_ref[...]
    lane = lax.broadcasted_iota(jnp.int32, (1, LANES), 1)
    lm = lane % 4
    xa = ab + dtb_ref[...]
    softplus = jnp.maximum(xa, 0.0) + jnp.log1p(jnp.exp(-jnp.abs(xa)))
    la = jnp.where((lm < 2) & (lane < 4 * N_HEADS), -jnp.exp(alog_ref[...]) * softplus, 0.0)
    g_fwd = _split_dot_left(tril_ref[...], la, 3)
    g_bwd = _split_dot_left(triu_ref[...], la, 3)
    gb = jnp.where(lm == 0, g_fwd, jnp.where(lm == 1, g_bwd, _sigmoid(ab)))
    for h in range(N_HEADS):
        gb_ref[h] = gb if h == 0 else pltpu.roll(gb, LANES - 4 * h, 1)


def _gdn_prep_call(gqkv, gab, conv_w, alog_row, dtb_row, b64, tril, triu, *, layer, ctx_rows, ctx_len, lat_len):
    t = gqkv.shape[0]
    r = PREP_R
    n_blocks = t // r
    rh = r // HALO
    last_halo = t // HALO - 1
    kern = functools.partial(_gdn_prep_kernel, ctx_blocks=ctx_rows // r, ctx_bps=ctx_len // r, lat_bps=lat_len // r)

    def const(shape):
        return pl.BlockSpec(shape, lambda i: (0,) * len(shape))

    head_major = pl.BlockSpec((N_HEADS, r, HEAD_W), lambda i: (0, i, 0))
    return pl.pallas_call(
        kern,
        grid=(n_blocks,),
        in_specs=[
            pl.BlockSpec((r, 3 * BRANCH_W), lambda i: (i, 0)),
            pl.BlockSpec((HALO, 3 * BRANCH_W), lambda i: (jnp.maximum(i * rh - 1, 0), 0)),
            pl.BlockSpec((HALO, 3 * BRANCH_W), lambda i: (jnp.minimum((i + 1) * rh, last_halo), 0)),
            pl.BlockSpec((r, LANES), lambda i: (i, 0)),
            _layer_spec((SUBLANES, 3 * BRANCH_W), layer),
            _layer_spec((1, LANES), layer),
            _layer_spec((1, LANES), layer),
            const((LANES, LANES)),
            const((r, r)),
            const((r, r)),
        ],
        out_specs=[head_major, head_major, head_major,
                   pl.BlockSpec((N_HEADS, r, LANES), lambda i: (0, i, 0))],
        out_shape=[jax.ShapeDtypeStruct((N_HEADS, t, HEAD_W), F32)] * 3
        + [jax.ShapeDtypeStruct((N_HEADS, t, LANES), F32)],
        compiler_params=pltpu.CompilerParams(dimension_semantics=("parallel",), vmem_limit_bytes=VMEM_LIMIT),
        name="gdn_prep",
    )(gqkv, gqkv, gqkv, gab, conv_w, alog_row, dtb_row, b64, tril, triu)


def _gdn_scan_kernel(*refs, n_seq, seq_len, has_s0):
    if has_s0:
        q_ref, k_ref, v_ref, gb_ref, s0_ref, o_ref, s_ref, oi_scr, qg_scr, m_scr, b_scr = refs
    else:
        q_ref, k_ref, v_ref, gb_ref, o_ref, s_ref, oi_scr, qg_scr, m_scr, b_scr = refs
        s0_ref = None
    c = GDN_CHUNK
    gr = SCAN_G * c
    n_rows = n_seq * seq_len
    nc = seq_len // c
    row_i = lax.broadcasted_iota(jnp.int32, (gr, gr), 0)
    col_i = lax.broadcasted_iota(jnp.int32, (gr, gr), 1)
    eye = jnp.where(row_i == col_i, 1.0, 0.0)
    same = (row_i // c) == (col_i // c)
    incl = (same & (col_i <= row_i), same & (col_i >= row_i))
    strict = (same & (col_i < row_i), same & (col_i > row_i))
    blk16 = (row_i // 16) == (col_i // 16)
    blk32 = ((row_i // 32) == (col_i // 32)) & ~blk16
    blk64 = same & ((row_i // 32) != (col_i // 32))

    def prep_groups(gi, carry):
        probs = []
        for j in range(PREP_PAR):
            r0 = pl.multiple_of((gi * PREP_PAR + j) * gr, gr)
            rows = pl.ds(r0, gr)
            q = q_ref[rows, :]
            k = k_ref[rows, :]
            v = v_ref[rows, :]
            gbc = gb_ref[rows, :]
            gbt = gbc.T
            kb = _bf(k)
            kk = _dot_nt(kb, kb)
            qk = _dot_nt(_bf(q), kb)
            for d in range(2):
                gcol = gbc[:, d:d + 1]
                bcol = gbc[:, 2 + d:3 + d]
                dec = jnp.where(incl[d], jnp.exp(jnp.where(incl[d], gcol - gbt[d:d + 1, :], 0.0)), 0.0)
                probs.append(dict(d=d, r0=r0, rows=rows, q=q, k=k, v=v, gcol=gcol, bcol=bcol, dec=dec,
                                  qkm=_bf(qk * dec), nmat=jnp.where(strict[d], bcol * kk * dec, 0.0)))
        ps = [jnp.where(blk16, pr["nmat"], 0.0) for pr in probs]
        ts = [eye - p for p in ps]
        for _ in range(3):
            pbs = [_bf(p) for p in ps]
            ps = [_dot(pb, pb) for pb in pbs]
            ts = [t + _dot(_bf(t), _bf(p)) for t, p in zip(ts, ps)]
        for level in (blk32, blk64):
            tbs = [_bf(t) for t in ts]
            xs = [_dot(tb, _bf(jnp.where(level, pr["nmat"], 0.0))) for tb, pr in zip(tbs, probs)]
            ts = [t - _dot(_bf(x), tb) for t, x, tb in zip(ts, xs, tbs)]
        ainvs = [_bf(t) for t in ts]
        egs = [jnp.exp(pr["gcol"]) for pr in probs]
        rhs = [_bf(jnp.concatenate([pr["bcol"] * pr["v"], (pr["bcol"] * eg) * pr["k"]], axis=-1))
               for pr, eg in zip(probs, egs)]
        uwks = [_bf(_dot(ai, r)) for ai, r in zip(ainvs, rhs)]
        intra = [_dot(pr["qkm"], uwk) for pr, uwk in zip(probs, uwks)]
        for pr, eg, uwk, it in zip(probs, egs, uwks, intra):
            d, r0, rows, gcol = pr["d"], pr["r0"], pr["rows"], pr["gcol"]
            oi_scr[d, rows, :] = it[:, :HEAD_W]
            qg_scr[d, rows, :] = _bf(pr["q"] * eg - it[:, HEAD_W:])
            glast = jnp.concatenate(
                [jnp.broadcast_to(gcol[j * c + (c - 1 if d == 0 else 0):j * c + (c if d == 0 else 1), :], (c, 1))
                 for j in range(SCAN_G)], axis=0)
            kdb = _bf(pr["k"] * jnp.exp(glast - gcol))
            for j in range(SCAN_G):
                cs = slice(j * c, (j + 1) * c)
                crow = pl.ds(r0 + j * c, c)
                kd_uwk = _dot_tn(kdb[cs], uwk[cs])
                b_scr[d, crow, :] = kd_uwk[:, :HEAD_W]
                m_scr[d, crow, :] = _bf(kd_uwk[:, HEAD_W:])
        return carry

    lax.fori_loop(0, n_rows // (gr * PREP_PAR), prep_groups, 0)

    def scan_step(s, d, ci, state):
        r0 = s * seq_len + ci * c
        if not isinstance(r0, int):
            r0 = pl.multiple_of(r0, c)
        rows = pl.ds(r0, c)
        sb = _bf(state)
        oi_scr[d, rows, :] = oi_scr[d, rows, :] + _dot(qg_scr[d, rows, :], sb)
        last = r0 + (c - 1) if d == 0 else r0
        egl = jnp.exp(gb_ref[pl.ds(last, 1), d:d + 1])
        return egl * state - _dot(m_scr[d, rows, :], sb) + b_scr[d, rows, :]

    def init_state(s, d):
        return s0_ref[s, d] if has_s0 else jnp.zeros((c, c), F32)

    chains = [(s, d) for s in range(n_seq) for d in range(2)]
    states = [init_state(s, d) for s, d in chains]
    if nc <= 8:
        for t in range(nc):
            states = [scan_step(s, d, t if d == 0 else nc - 1 - t, st) for (s, d), st in zip(chains, states)]
    else:
        def body(t, sts):
            return tuple(scan_step(s, d, t if d == 0 else nc - 1 - t, st) for (s, d), st in zip(chains, sts))
        states = lax.fori_loop(0, nc, body, tuple(states))
    for (s, d), st in zip(chains, states):
        s_ref[s, d] = st
    o_ref[...] = oi_scr[0] + oi_scr[1]


def _gdn_scan_call(q, k, v, gb, s0, *, row0, n_seq_total, seq_len, seqs_per_step):
    n_rows = seqs_per_step * seq_len
    assert n_rows % (SCAN_G * GDN_CHUNK * PREP_PAR) == 0 and row0 % n_rows == 0
    blk0 = row0 // n_rows
    has_s0 = s0 is not None
    kern = functools.partial(_gdn_scan_kernel, n_seq=seqs_per_step, seq_len=seq_len, has_s0=has_s0)
    hm = pl.BlockSpec((None, n_rows, HEAD_W), lambda s, h: (h, blk0 + s, 0))
    st_spec = pl.BlockSpec((seqs_per_step, 2, None, HEAD_W, HEAD_W), lambda s, h: (s, 0, h, 0, 0))
    in_specs = [hm, hm, hm, pl.BlockSpec((None, n_rows, LANES), lambda s, h: (h, blk0 + s, 0))]
    args = [q, k, v, gb]
    if has_s0:
        in_specs.append(st_spec)
        args.append(s0)
    return pl.pallas_call(
        kern,
        grid=(n_seq_total // seqs_per_step, N_HEADS),
        in_specs=in_specs,
        out_specs=[pl.BlockSpec((None, n_rows, HEAD_W), lambda s, h: (h, s, 0)), st_spec],
        out_shape=[jax.ShapeDtypeStruct((N_HEADS, n_seq_total * seq_len, HEAD_W), F32),
                   jax.ShapeDtypeStruct((n_seq_total, 2, N_HEADS, HEAD_W, HEAD_W), F32)],
        scratch_shapes=[
            pltpu.VMEM((2, n_rows, HEAD_W), F32),
            pltpu.VMEM((2, n_rows, HEAD_W), BF16),
            pltpu.VMEM((2, n_rows, HEAD_W), BF16),
            pltpu.VMEM((2, n_rows, HEAD_W), F32),
        ],
        compiler_params=pltpu.CompilerParams(dimension_semantics=("parallel", "parallel"),
                                             vmem_limit_bytes=VMEM_LIMIT),
        name="gdn_scan",
    )(*args)


ONES_ROWS = 16


KEY_CHUNK = 1024
ATTN_WAVE_CTX = 4
ATTN_WAVE_LAT = 3


def _attend_all(tasks, width):
    waves = [tasks[i:i + width] for i in range(0, len(tasks), width)]

    def score_piece(wave, c):
        return [_dot(ks[c], qt) for qt, ks, _ in wave]

    n_pieces = len(tasks[0][1])
    outs = []
    sts = [score_piece(waves[0], c) for c in range(n_pieces)]
    for w, wave in enumerate(waves):
        mxs = []
        for j in range(len(wave)):
            mx = None
            for c in range(n_pieces):
                m = jnp.max(sts[c][j], axis=0, keepdims=True)
                mx = m if mx is None else jnp.maximum(mx, m)
            mxs.append(mx)
        nxt = []
        accs = [None] * len(wave)
        for c in range(n_pieces):
            if w + 1 < len(waves):
                nxt.append(score_piece(waves[w + 1], c))
            ps = [_bf(jnp.exp2(sts[c][j] - mxs[j])) for j in range(len(wave))]
            for j, (_, _, vts) in enumerate(wave):
                t = _dot(vts[c], ps[j])
                accs[j] = t if accs[j] is None else accs[j] + t
        outs += [acc[:HEAD_W] * (1.0 / acc[HEAD_W:HEAD_W + 1]) for acc in accs]
        sts = nxt
    return outs


def _key_pieces(k_ref, cache_ref):
    n = k_ref.shape[0]
    kc = min(KEY_CHUNK, n)
    own = [k_ref[c * kc:(c + 1) * kc, :] for c in range(n // kc)]
    return own if cache_ref is None else [cache_ref[...]] + own


def _value_pieces(vt_ref, cache_ref, rows):
    def with_ones(x):
        return jnp.concatenate([x, jnp.ones((ONES_ROWS, x.shape[1]), x.dtype)], axis=0)

    n = vt_ref.shape[1]
    kc = min(KEY_CHUNK, n)
    own = [with_ones(vt_ref[rows, c * kc:(c + 1) * kc]) for c in range(n // kc)]
    return own if cache_ref is None else [with_ones(cache_ref[rows, :])] + own


def _attn_kernel(*refs, has_cache, lam_init):
    if has_cache:
        (dqt_ref, dk_ref, dvt_ref, gqt_ref, gk_ref, gvt_ref, cdk_ref, cdvt_ref, cgk_ref, cgvt_ref,
         lp_ref, dn_ref, b64_ref, ob_ref, od_ref) = refs
    else:
        dqt_ref, dk_ref, dvt_ref, gqt_ref, gk_ref, gvt_ref, lp_ref, dn_ref, b64_ref, ob_ref, od_ref = refs
        cdk_ref = cdvt_ref = cgk_ref = cgvt_ref = None

    lp = lp_ref[...]
    lam = (jnp.exp(jnp.sum(lp[0:1] * lp[1:2], axis=-1, keepdims=True))
           - jnp.exp(jnp.sum(lp[2:3] * lp[3:4], axis=-1, keepdims=True)) + lam_init)

    chan = lax.broadcasted_iota(jnp.int32, (BRANCH_W, 1), 0)
    chan_h = lax.broadcasted_iota(jnp.int32, (LANES, 1), 0)
    dqt = dqt_ref[...]
    dks = _key_pieces(dk_ref, cdk_ref)
    gks = _key_pieces(gk_ref, cgk_ref)
    tasks = []
    for h in range(N_HEADS):
        vts = _value_pieces(dvt_ref, cdvt_ref, slice(h * HEAD_W, (h + 1) * HEAD_W))
        for m in range(2):
            lo = h * HEAD_W + m * DIFF_DQK
            tasks.append((dqt * jnp.where((chan >= lo) & (chan < lo + DIFF_DQK), 1.0, 0.0).astype(BF16), dks, vts))
    kv_vts = [_value_pieces(gvt_ref, cgvt_ref, slice(g * HEAD_W, (g + 1) * HEAD_W)) for g in range(GQA_KV_HEADS)]
    for half in range(2):
        qh = gqt_ref[half * LANES:(half + 1) * LANES, :]
        for g in range(GQA_KV_HEADS):
            tasks.append((qh * jnp.where(chan_h // HEAD_W == g, 1.0, 0.0).astype(BF16), gks, kv_vts[g]))
    outs = _attend_all(tasks, ATTN_WAVE_LAT if has_cache else ATTN_WAVE_CTX)

    o = jnp.concatenate([outs[2 * h] - lam * outs[2 * h + 1] for h in range(N_HEADS)], axis=0).T
    b64 = b64_ref[...]
    for j in range(BRANCH_W // LANES):
        sl = slice(j * LANES, (j + 1) * LANES)
        x = o[:, sl]
        ms = _group_sum(x * x, b64) * (1.0 / HEAD_W)
        ob_ref[:, sl] = _bf(x * lax.rsqrt(ms + EPS) * dn_ref[:, sl] * (1.0 - lam_init))
    od_ref[...] = _bf(jnp.concatenate(outs[2 * N_HEADS:], axis=0).T)


def _attn_ctx_call(dqt, dk, dvt, gqt, gk, gvt, lp, dn, b64, *, layer, n_seq, seq_len, lam_init):
    kern = functools.partial(_attn_kernel, has_cache=False, lam_init=lam_init)

    def rows(width):
        return pl.BlockSpec((seq_len, width), lambda s: (s, 0))

    def cols(height):
        return pl.BlockSpec((height, seq_len), lambda s: (0, s))

    def const(shape):
        return pl.BlockSpec(shape, lambda s: (0,) * len(shape))

    return pl.pallas_call(
        kern,
        grid=(n_seq,),
        in_specs=[cols(256), rows(256), cols(256), cols(256), rows(128), cols(128),
                  _layer_spec((4, DIFF_DQK), layer), _layer_spec((1, BRANCH_W), layer), const((LANES, LANES))],
        out_specs=[rows(256), rows(256)],
        out_shape=[jax.ShapeDtypeStruct((n_seq * seq_len, BRANCH_W), BF16)] * 2,
        compiler_params=pltpu.CompilerParams(dimension_semantics=("parallel",), vmem_limit_bytes=VMEM_LIMIT),
        name="attn_ctx",
    )(dqt, dk, dvt, gqt, gk, gvt, lp, dn, b64)


def _attn_lat_call(dqt, dk, dvt, gqt, gk, gvt, cdk, cdvt, cgk, cgvt, lp, dn, b64, *, layer, row0, n_seq, seq_len,
                   lam_init):
    kern = functools.partial(_attn_kernel, has_cache=True, lam_init=lam_init)
    tq = ATTN_TQ
    qpb = seq_len // tq
    q0 = row0 // tq
    s0 = row0 // seq_len
    past = cdk.shape[1]

    def qcols(height):
        return pl.BlockSpec((height, tq), lambda b, i: (0, q0 + b * qpb + i))

    def kvrows(width):
        return pl.BlockSpec((seq_len, width), lambda b, i: (s0 + b, 0))

    def kvcols(height):
        return pl.BlockSpec((height, seq_len), lambda b, i: (0, s0 + b))

    def cache(width):
        return pl.BlockSpec((None, past, width), lambda b, i: (b, 0, 0))

    def cache_t(height):
        return pl.BlockSpec((None, height, past), lambda b, i: (b, 0, 0))

    def const(shape):
        return pl.BlockSpec(shape, lambda b, i: (0,) * len(shape))

    out = pl.BlockSpec((tq, BRANCH_W), lambda b, i: (b * qpb + i, 0))
    return pl.pallas_call(
        kern,
        grid=(n_seq, qpb),
        in_specs=[qcols(256), kvrows(256), kvcols(256), qcols(256), kvrows(128), kvcols(128),
                  cache(256), cache_t(256), cache(128), cache_t(128),
                  _layer_spec((4, DIFF_DQK), layer), _layer_spec((1, BRANCH_W), layer), const((LANES, LANES))],
        out_specs=[out, out],
        out_shape=[jax.ShapeDtypeStruct((n_seq * seq_len, BRANCH_W), BF16)] * 2,
        compiler_params=pltpu.CompilerParams(dimension_semantics=("parallel", "arbitrary"),
                                             vmem_limit_bytes=VMEM_LIMIT),
        name="attn_lat",
    )(dqt, dk, dvt, gqt, gk, gvt, cdk, cdvt, cgk, cgvt, lp, dn, b64)


def _merge_kernel(h_ref, mod_ref, n1_ref, oac_ref, oal_ref, gate_ref, gn_ref, obc_ref, obl_ref, oc_ref,
                  odc_ref, odl_ref, wmg_ref, wb_ref, wo_ref, b64_ref, out_ref, *, tiles_ctx):
    is_ctx = pl.program_id(0) < tiles_ctx
    h = h_ref[...]
    xm = _bf(_norm_mod(h, n1_ref[...], mod_ref[:, D_MODEL:2 * D_MODEL], mod_ref[:, 0:D_MODEL]))
    g1 = mod_ref[:, 2 * D_MODEL:3 * D_MODEL]
    b64 = b64_ref[...]
    oa_parts = []
    for j in range(BRANCH_W // LANES):
        sl = slice(j * LANES, (j + 1) * LANES)
        x = jnp.concatenate([jnp.where(is_ctx, oac_ref[2 * j + hh], oal_ref[2 * j + hh])
                             for hh in range(LANES // HEAD_W)], axis=-1)
        ms = _group_sum(x * x, b64) * (1.0 / HEAD_W)
        oa_parts.append(_bf(x * lax.rsqrt(ms + EPS) * gn_ref[:, sl] * _silu(gate_ref[:, sl])))
    branches = (None, jnp.where(is_ctx, obc_ref[...], obl_ref[...]), oc_ref[...],
                jnp.where(is_ctx, odc_ref[...], odl_ref[...]))
    merged = None
    for n in range(4):
        mg = _dot(xm, wmg_ref[:, n * D_MODEL:(n + 1) * D_MODEL])
        if n == 0:
            up = _dot(oa_parts[0], wb_ref[0, 0:LANES, :]) + _dot(oa_parts[1], wb_ref[0, LANES:2 * LANES, :])
        else:
            up = _dot(branches[n], wb_ref[n])
        term = _sigmoid(mg) * up
        merged = term if merged is None else merged + term
    out_ref[...] = h + g1 * _dot(_bf(merged), wo_ref[...])


def _merge_call(h, mods, n1, oa_ctx, oa_lat, gate, gn, ob_ctx, ob_lat, oc, od_ctx, od_lat, wmg, wb, wo, b64, *,
                layer, tiles_ctx, tiles_per_lat):
    t = h.shape[0]
    last_ctx = tiles_ctx - 1

    def mod_row(i):
        return jnp.where(i < tiles_ctx, 0, 1 + (i - tiles_ctx) // tiles_per_lat)

    def rows(width):
        return pl.BlockSpec((TM, width), lambda i: (i, 0))

    def ctx_rows(width):
        return pl.BlockSpec((TM, width), lambda i: (jnp.minimum(i, last_ctx), 0))

    def lat_rows(width):
        return pl.BlockSpec((TM, width), lambda i: (jnp.maximum(i - tiles_ctx, 0), 0))

    return pl.pallas_call(
        functools.partial(_merge_kernel, tiles_ctx=tiles_ctx),
        grid=(t // TM,),
        in_specs=[
            rows(D_MODEL),
            _mod_spec(layer, mod_row),
            _layer_spec((1, D_MODEL), layer),
            pl.BlockSpec((N_HEADS, TM, HEAD_W), lambda i: (0, jnp.minimum(i, last_ctx), 0)),
            pl.BlockSpec((N_HEADS, TM, HEAD_W), lambda i: (0, jnp.maximum(i - tiles_ctx, 0), 0)),
            rows(BRANCH_W),
            _layer_spec((1, BRANCH_W), layer),
            ctx_rows(BRANCH_W), lat_rows(BRANCH_W), rows(BRANCH_W), ctx_rows(BRANCH_W), lat_rows(BRANCH_W),
            _layer_spec((D_MODEL, 4 * D_MODEL), layer),
            _layer_spec((4, BRANCH_W, D_MODEL), layer),
            _layer_spec((D_MODEL, D_MODEL), layer),
            pl.BlockSpec((LANES, LANES), lambda i: (0, 0)),
        ],
        out_specs=rows(D_MODEL),
        out_shape=jax.ShapeDtypeStruct((t, D_MODEL), F32),
        compiler_params=pltpu.CompilerParams(dimension_semantics=("parallel",), vmem_limit_bytes=VMEM_LIMIT),
        name="merge",
    )(h, mods, n1, oa_ctx, oa_lat, gate, gn, ob_ctx, ob_lat, oc, od_ctx, od_lat, wmg, wb, wo, b64)


ROUTER_ROWS = 32
ROUTER_E0 = 8


def _moe_kernel(h_ref, mod_ref, n2_ref, wr_ref, br_ref, w1_ref, w3_ref, w2_ref, out_ref, xm_scr, gates_scr, acc_scr):
    e = pl.program_id(1)
    tm = h_ref.shape[0]

    @pl.when(e == 0)
    def _():
        xm = _norm_mod(h_ref[...], n2_ref[...], mod_ref[:, 4 * D_MODEL:5 * D_MODEL],
                       mod_ref[:, 3 * D_MODEL:4 * D_MODEL])
        x_hi = _bf(xm)
        xm_scr[...] = x_hi
        x_lo = _bf(xm - x_hi.astype(F32))
        hi = _dot_nt(wr_ref[...], x_hi)
        lt = (hi[0:ROUTER_ROWS] + hi[ROUTER_ROWS:2 * ROUTER_ROWS]
              + _dot_nt(wr_ref[0:ROUTER_ROWS, :], x_lo) + br_ref[...])
        gl = lt[0:N_GROUPS]
        gidx = lax.broadcasted_iota(jnp.int32, (N_GROUPS, 1), 0)
        gmax = jnp.max(gl, axis=0, keepdims=True)
        gsel = jnp.min(jnp.where(gl == gmax, gidx, N_GROUPS), axis=0, keepdims=True)
        gprob = 1.0 / jnp.sum(jnp.exp(gl - gmax), axis=0, keepdims=True)
        el = lt[ROUTER_E0:ROUTER_E0 + N_EXPERTS]
        eidx = lax.broadcasted_iota(jnp.int32, (N_EXPERTS, 1), 0)
        neg = -jnp.inf
        cand = jnp.where(eidx // EXPERTS_PER_GROUP == gsel, el, neg)
        v1 = jnp.max(cand, axis=0, keepdims=True)
        i1 = jnp.min(jnp.where(cand == v1, eidx, N_EXPERTS), axis=0, keepdims=True)
        cand2 = jnp.where(eidx == i1, neg, cand)
        v2 = jnp.max(cand2, axis=0, keepdims=True)
        i2 = jnp.min(jnp.where(cand2 == v2, eidx, N_EXPERTS), axis=0, keepdims=True)
        ex = jnp.exp(v2 - v1)
        wa = 1.0 / (1.0 + ex)
        gates_t = (jnp.where(eidx == i1, wa, 0.0) + jnp.where(eidx == i2, ex * wa, 0.0)) * gprob
        full = jnp.concatenate([gates_t, jnp.zeros((LANES - N_EXPERTS, tm), F32)], axis=0)
        gates_scr[...] = full.T
        acc_scr[...] = jnp.zeros_like(acc_scr)

    xm = xm_scr[...]
    lane = lax.broadcasted_iota(jnp.int32, (1, LANES), 1)
    gates = gates_scr[...]
    hds = []
    for j in range(MOE_EPS):
        h1 = _dot(xm, w1_ref[j])
        h3 = _dot(xm, w3_ref[j])
        gate = jnp.sum(jnp.where(lane == e * MOE_EPS + j, gates, 0.0), axis=-1, keepdims=True)
        hds.append(_bf(_silu(h1) * h3 * gate))
    acc_scr[...] += _dot(jnp.concatenate(hds, axis=-1), w2_ref[...].reshape(MOE_EPS * D_EXPERT, D_MODEL))

    @pl.when(e == N_EXPERTS // MOE_EPS - 1)
    def _():
        out_ref[...] = h_ref[...] + mod_ref[:, 5 * D_MODEL:6 * D_MODEL] * acc_scr[...]


def _moe_call(h, mods, n2, wr, br, w1, w3, w2, *, layer, ctx_rows, lat_len):
    t = h.shape[0]
    tm = TM_MOE
    tiles_ctx = ctx_rows // tm
    tiles_per_lat = lat_len // tm

    def mod_row(i):
        return jnp.where(i < tiles_ctx, 0, 1 + (i - tiles_ctx) // tiles_per_lat)

    def expert(shape):
        return pl.BlockSpec((None, MOE_EPS) + shape, lambda i, e: (layer, e, 0, 0))

    return pl.pallas_call(
        _moe_kernel,
        grid=(t // tm, N_EXPERTS // MOE_EPS),
        in_specs=[
            pl.BlockSpec((tm, D_MODEL), lambda i, e: (i, 0)),
            _mod_spec(layer, mod_row),
            _layer_spec((1, D_MODEL), layer),
            _layer_spec((2 * ROUTER_ROWS, D_MODEL), layer),
            _layer_spec((ROUTER_ROWS, 1), layer),
            expert((D_MODEL, D_EXPERT)),
            expert((D_MODEL, D_EXPERT)),
            expert((D_EXPERT, D_MODEL)),
        ],
        out_specs=pl.BlockSpec((tm, D_MODEL), lambda i, e: (i, 0)),
        out_shape=jax.ShapeDtypeStruct((t, D_MODEL), F32),
        scratch_shapes=[pltpu.VMEM((tm, D_MODEL), BF16), pltpu.VMEM((tm, LANES), F32),
                        pltpu.VMEM((tm, D_MODEL), F32)],
        compiler_params=pltpu.CompilerParams(dimension_semantics=("parallel", "arbitrary"),
                                             vmem_limit_bytes=VMEM_LIMIT),
        name="moe",
    )(h, mods, n2, wr, br, w1, w3, w2)


def _final_norm_kernel(h_ref, g_ref, o_ref):
    h = h_ref[...]
    o_ref[...] = h * lax.rsqrt(jnp.mean(h * h, axis=-1, keepdims=True) + EPS) * g_ref[...]


def _final_norm_call(h, g, *, row0, n_rows):
    blk0 = row0 // TM
    return pl.pallas_call(
        _final_norm_kernel,
        grid=(n_rows // TM,),
        in_specs=[pl.BlockSpec((TM, D_MODEL), lambda i: (blk0 + i, 0)), pl.BlockSpec((1, D_MODEL), lambda i: (0, 0))],
        out_specs=pl.BlockSpec((TM, D_MODEL), lambda i: (i, 0)),
        out_shape=jax.ShapeDtypeStruct((n_rows, D_MODEL), F32),
        compiler_params=pltpu.CompilerParams(dimension_semantics=("parallel",)),
        name="final_norm",
    )(h, g)


def _mixer_weights(w_in):
    depth, d, _ = w_in.shape
    ab_cols = []
    for h in range(N_HEADS):
        ab_cols += [_O_A + h, _O_A + N_HEADS + h, _O_B + h, _O_B + N_HEADS + h]
    ab = jnp.take(w_in[:, :, _O_A:_O_DQ], jnp.asarray([c - _O_A for c in ab_cols], dtype=jnp.int32), axis=2)
    parts = [w_in[:, :, _O_QKV:_O_A], ab, jnp.zeros((depth, d, LANES - len(ab_cols)), w_in.dtype),
             w_in[:, :, _O_DQ:_O_GQ]]
    for h in (0, 2, 1, 3):
        parts.append(w_in[:, :, _O_GQ + h * HEAD_W:_O_GQ + (h + 1) * HEAD_W])
    parts.append(w_in[:, :, _O_GK:_O_MG])
    w_mix = _bf(jnp.concatenate(parts, axis=2))
    assert w_mix.shape[2] == C_END
    return w_mix


def _rope_tables(lat_len):
    pos = jnp.arange(lat_len, dtype=jnp.int32)
    row = (pos // GRID_W).astype(F32)[:, None]
    col = (pos % GRID_W).astype(F32)[:, None]
    lane = jnp.arange(LANES, dtype=jnp.int32)[None, :]
    tabs = []
    for head_w in (DIFF_DQK, HEAD_W):
        half = head_w // 2
        nf = half // 2
        p = lane % head_w
        f = (p % half) % nf
        inv = ROPE_THETA ** (-f.astype(F32) / nf)
        ang = jnp.where(p < half, row, col) * inv
        sign = jnp.where((p % half) < nf, -1.0, 1.0)
        tabs += [jnp.cos(ang), jnp.sin(ang) * sign]
    tab = jnp.stack(tabs)
    ident = jnp.stack([jnp.ones((TM, LANES), F32), jnp.zeros((TM, LANES), F32)] * 2)
    return jnp.concatenate([ident, tab], axis=1)


def _chunk_tri(n, chunk, upper):
    i = jnp.arange(n)[:, None]
    j = jnp.arange(n)[None, :]
    same = (i // chunk) == (j // chunk)
    tri = (j >= i) if upper else (j <= i)
    return jnp.where(same & tri, 1.0, 0.0).astype(BF16)


def kernel(x_prompt, x_sample, state_gdn, cache_diff_k, cache_diff_v, cache_gqa_k, cache_gqa_v, c, c_ctx,
           ada_w, ada_b, norm1, norm2, w_in, gdn_conv, gdn_a_log, gdn_dt_bias, gdn_norm, diff_lambda, diff_norm,
           sgu_norm, sgu_ws, sgu_b, gqa_qnorm, gqa_knorm, w_branch, w_out, moe_wg, moe_bg, moe_we, moe_be,
           moe_w1, moe_w3, moe_w2, final_norm):
    nb, seq, d = x_prompt.shape
    nl, lat_len, _ = x_sample.shape
    depth = ada_w.shape[0]
    past = cache_diff_k.shape[2]
    ctx_rows = nb * seq
    lat_rows = nl * lat_len
    assert d == D_MODEL and seq % PREP_R == 0 and lat_len % TM == 0 and ctx_rows % lat_len == 0
    assert ctx_rows % TM == 0 and 1 + nl <= SUBLANES and lat_len % GRID_W == 0
    assert ctx_rows % TM_MOE == 0 and lat_len % TM_MOE == 0
    tiles_ctx = ctx_rows // TM
    tiles_per_lat = lat_len // TM
    tile_kw = dict(tiles_ctx=tiles_ctx, tiles_per_lat=tiles_per_lat)

    lane = jnp.arange(LANES)
    b64 = jnp.where((lane[:, None] // HEAD_W) == (lane[None, :] // HEAD_W), 1.0, 0.0).astype(BF16)
    tril = _chunk_tri(PREP_R, GDN_CHUNK, upper=False)
    triu = _chunk_tri(PREP_R, GDN_CHUNK, upper=True)
    rope_tab = _rope_tables(lat_len)

    w_mix = _mixer_weights(w_in)
    w_mg = _bf(w_in[:, :, _O_MG:])
    head_perm = jnp.asarray([r for h in (0, 2, 1, 3) for r in range(h * HEAD_W, (h + 1) * HEAD_W)], dtype=jnp.int32)
    wb = _bf(jnp.concatenate([w_branch[:, :3], jnp.take(w_branch[:, 3], head_perm, axis=1)[:, None]], axis=1))
    wo = _bf(w_out)
    w1, w3, w2 = _bf(moe_w1), _bf(moe_w3), _bf(moe_w2)
    wr = jnp.zeros((depth, ROUTER_ROWS, d), F32)
    wr = wr.at[:, 0:N_GROUPS].set(jnp.swapaxes(moe_wg, 1, 2))
    wr = wr.at[:, ROUTER_E0:ROUTER_E0 + N_EXPERTS].set(jnp.swapaxes(moe_we, 1, 2))
    wr_hi = _bf(wr)
    wr = jnp.concatenate([wr_hi, _bf(wr - wr_hi.astype(F32))], axis=1)
    br = jnp.zeros((depth, ROUTER_ROWS, 1), F32)
    br = br.at[:, 0:N_GROUPS, 0].set(moe_bg).at[:, ROUTER_E0:ROUTER_E0 + N_EXPERTS, 0].set(moe_be)
    conv_w = jnp.zeros((depth, SUBLANES, 3 * BRANCH_W), F32).at[:, :CONV_K].set(gdn_conv)
    ab_lane = jnp.zeros((depth, 1, LANES), F32)
    alog_row, dtb_row = ab_lane, ab_lane
    for h in range(N_HEADS):
        for dd in range(2):
            alog_row = alog_row.at[:, 0, 4 * h + dd].set(gdn_a_log[:, dd, h])
            dtb_row = dtb_row.at[:, 0, 4 * h + dd].set(gdn_dt_bias[:, dd, h])
    sgw = _bf(sgu_ws)
    sgb = jnp.repeat(jnp.swapaxes(sgu_b, 1, 2), HEAD_W, axis=2)
    gdn_gn = jnp.tile(gdn_norm, (1, N_HEADS))[:, None, :]
    diff_gn = jnp.tile(diff_norm, (1, N_HEADS))[:, None, :]
    gqn = jnp.tile(gqa_qnorm, (1, LANES // HEAD_W))[:, None, :]
    gkn = jnp.tile(gqa_knorm, (1, LANES // HEAD_W))[:, None, :]

    cond = jnp.zeros((SUBLANES, d), F32).at[0].set(c_ctx).at[1:1 + nl].set(c)
    mods = _ada_call(cond, ada_w, ada_b)[:, :, None, :]
    n1 = norm1[:, None, :]
    n2 = norm2[:, None, :]
    sgn = sgu_norm[:, None, :]

    h = jnp.concatenate([x_prompt.reshape(ctx_rows, d), x_sample.reshape(lat_rows, d)], axis=0)
    new_s, new_dk, new_dv, new_gk, new_gv = [], [], [], [], []
    for l in range(depth):
        lam_init = 0.8 - 0.6 * math.exp(-0.3 * l)
        (gqkv, ggate, gab, dqt, dk32, dv32, dk16, dvt, oc, gqt, gk32, gv32, gk16, gvt) = _inproj_call(
            h, mods, n1, w_mix, rope_tab, b64, sgn, sgw, sgb, gqn, gkn, layer=l, **tile_kw)

        qh, kh, vh, gb = _gdn_prep_call(gqkv, gab, conv_w, alog_row, dtb_row, b64, tril, triu, layer=l,
                                        ctx_rows=ctx_rows, ctx_len=seq, lat_len=lat_len)
        oa_ctx, s_ctx = _gdn_scan_call(qh, kh, vh, gb, None, row0=0, n_seq_total=nb, seq_len=seq,
                                       seqs_per_step=8 if nb % 8 == 0 else 1)
        oa_lat, _ = _gdn_scan_call(qh, kh, vh, gb, state_gdn[:, l], row0=ctx_rows, n_seq_total=nl,
                                   seq_len=lat_len, seqs_per_step=1)

        ob_ctx, od_ctx = _attn_ctx_call(dqt, dk16, dvt, gqt, gk16, gvt, diff_lambda, diff_gn, b64, layer=l,
                                        n_seq=nb, seq_len=seq, lam_init=lam_init)
        ob_lat, od_lat = _attn_lat_call(
            dqt, dk16, dvt, gqt, gk16, gvt,
            _bf(cache_diff_k[:, l].reshape(nl, past, BRANCH_W)),
            _bf(jnp.swapaxes(cache_diff_v[:, l].reshape(nl, past, BRANCH_W), 1, 2)),
            _bf(cache_gqa_k[:, l].reshape(nl, past, LANES)),
            _bf(jnp.swapaxes(cache_gqa_v[:, l].reshape(nl, past, LANES), 1, 2)),
            diff_lambda, diff_gn, b64, layer=l, row0=ctx_rows, n_seq=nl, seq_len=lat_len, lam_init=lam_init)

        h = _merge_call(h, mods, n1, oa_ctx, oa_lat, ggate, gdn_gn, ob_ctx, ob_lat, oc, od_ctx, od_lat,
                        w_mg, wb, wo, b64, layer=l, **tile_kw)
        h = _moe_call(h, mods, n2, wr, br, w1, w3, w2, layer=l, ctx_rows=ctx_rows, lat_len=lat_len)

        new_s.append(s_ctx)
        new_dk.append(dk32[:ctx_rows].reshape(nb, seq, N_HEADS, HEAD_W))
        new_dv.append(dv32[:ctx_rows].reshape(nb, seq, N_HEADS, HEAD_W))
        new_gk.append(gk32[:ctx_rows].reshape(nb, seq, GQA_KV_HEADS, HEAD_W))
        new_gv.append(gv32[:ctx_rows].reshape(nb, seq, GQA_KV_HEADS, HEAD_W))

    fg = final_norm[None, :]
    y_ctx = _final_norm_call(h, fg, row0=0, n_rows=ctx_rows)
    y_lat = _final_norm_call(h, fg, row0=ctx_rows, n_rows=lat_rows)
    return (y_ctx.reshape(nb, seq, d), y_lat.reshape(nl, lat_len, d),
            jnp.stack(new_s, axis=1), jnp.stack(new_dk, axis=1), jnp.stack(new_dv, axis=1),
            jnp.stack(new_gk, axis=1), jnp.stack(new_gv, axis=1))
```

```python
import functools
import math

import jax
import jax.numpy as jnp
from jax import lax
from jax.experimental import pallas as pl
from jax.experimental.pallas import tpu as pltpu

F32 = jnp.float32
BF16 = jnp.bfloat16

GRID_W = 64
ROPE_THETA = 10000.0
EPS = 1e-6
D_MODEL = 1024
BRANCH_W = 256
HEAD_W = 64
DIFF_DQK = 32
N_HEADS = 4
GQA_KV_HEADS = 2
GDN_CHUNK = 64
CONV_K = 5
SGU_CHUNK = 128
SGU_GROUPS = 4
N_GROUPS = 4
EXPERTS_PER_GROUP = 4
N_EXPERTS = 16
D_EXPERT = 256
LOG2E = 1.4426950408889634
DIFF_SCALE = DIFF_DQK ** -0.5 * LOG2E
GQA_SCALE = HEAD_W ** -0.5 * LOG2E
GDN_Q_SCALE = HEAD_W ** -0.5

LANES = 128
SUBLANES = 8
VMEM_LIMIT = 56 * 1024 * 1024

_IN_WIDTHS = (768, 256, 8, 8, 256, 256, 256, 256, 256, 256, 128, 128, 4096)
_IN_OFFS = tuple(sum(_IN_WIDTHS[:i]) for i in range(len(_IN_WIDTHS)))
(_O_QKV, _O_GATE, _O_A, _O_B, _O_DQ, _O_DK, _O_DV, _O_SU, _O_SV, _O_GQ, _O_GK, _O_GV, _O_MG) = _IN_OFFS

C_QKV, C_GATE, C_AB, C_DQ, C_DK, C_DV, C_SU, C_SV, C_GQ, C_GK, C_GV, C_END = (
    0, 768, 1024, 1152, 1408, 1664, 1920, 2176, 2432, 2688, 2816, 2944)

TM = 512
TM_MOE = 1024
MOE_EPS = 4
PREP_R = 256
HALO = SUBLANES
SCAN_G = 4
PREP_PAR = 4
ATTN_TQ = 256


def _bf(x):
    return x.astype(BF16)


def _dot(a, b):
    return jnp.dot(a, b, preferred_element_type=F32)


def _dot_nt(a, b):
    return lax.dot_general(a, b, (((1,), (1,)), ((), ())), preferred_element_type=F32)


def _dot_tn(a, b):
    return lax.dot_general(a, b, (((0,), (0,)), ((), ())), preferred_element_type=F32)


def _split_dot_left(m, x, terms):
    acc = None
    r = x
    for t in range(terms):
        p = _bf(r)
        d = _dot(m, p)
        acc = d if acc is None else acc + d
        if t + 1 < terms:
            r = r - p.astype(F32)
    return acc


def _sigmoid(x):
    return 1.0 / (1.0 + jnp.exp(-x))


def _silu(x):
    return x * _sigmoid(x)


def _norm_mod(h, gamma, scale, shift):
    ms = jnp.mean(h * h, axis=-1, keepdims=True)
    return (h * lax.rsqrt(ms + EPS) * gamma) * (1.0 + scale) + shift


def _group_sum(sq, b64):
    return _dot(_bf(sq), b64)


def _rope(x, cos, sin_signed, nf):
    lane = lax.broadcasted_iota(jnp.int32, (1, LANES), 1)
    first = (lane % (2 * nf)) < nf
    partner = jnp.where(first, pltpu.roll(x, LANES - nf, 1), pltpu.roll(x, nf, 1))
    return x * cos + partner * sin_signed


def _layer_spec(shape, layer):
    return pl.BlockSpec((None,) + tuple(shape), lambda *_: (layer,) + (0,) * len(shape))


def _mod_spec(layer, mod_row):
    return pl.BlockSpec((None, None, 1, 6 * D_MODEL), lambda i, *_: (layer, mod_row(i), 0, 0))


def _ada_kernel(c_ref, w_ref, b_ref, o_ref):
    x = _silu(c_ref[...])
    o_ref[...] = jnp.dot(x, w_ref[...], precision=lax.Precision.HIGHEST,
                         preferred_element_type=F32) + b_ref[...]


def _ada_call(cond, ada_w, ada_b):
    n_layers, d, n_out = ada_w.shape
    tn = 1536
    return pl.pallas_call(
        _ada_kernel,
        grid=(n_layers, n_out // tn),
        in_specs=[
            pl.BlockSpec((SUBLANES, d), lambda l, j: (0, 0)),
            pl.BlockSpec((None, d, tn), lambda l, j: (l, 0, j)),
            pl.BlockSpec((None, 1, tn), lambda l, j: (l, 0, j)),
        ],
        out_specs=pl.BlockSpec((None, SUBLANES, tn), lambda l, j: (l, 0, j)),
        out_shape=jax.ShapeDtypeStruct((n_layers, SUBLANES, n_out), F32),
        compiler_params=pltpu.CompilerParams(vmem_limit_bytes=VMEM_LIMIT),
        name="ada_mod",
    )(cond, ada_w, ada_b.reshape(n_layers, 1, n_out))


def _inproj_kernel(h_ref, mod_ref, n1_ref, w_ref, rope_ref, b64_ref, sgn_ref, sgw_ref, sgb_ref, gqn_ref, gkn_ref,
                   gqkv_ref, ggate_ref, gab_ref, dqt_ref, dk32_ref, dv32_ref, dk16_ref, dvt_ref, oc_ref,
                   gqt_ref, gk32_ref, gv32_ref, gk16_ref, gvt_ref):
    tm = h_ref.shape[0]
    xm = _bf(_norm_mod(h_ref[...], n1_ref[...], mod_ref[:, D_MODEL:2 * D_MODEL], mod_ref[:, 0:D_MODEL]))

    def proj(a, b):
        return _dot(xm, w_ref[:, a:b])

    gqkv_ref[...] = proj(C_QKV, C_GATE)
    ggate_ref[...] = proj(C_GATE, C_AB)
    gab_ref[...] = proj(C_AB, C_DQ)

    cos_d, sin_d, cos_g, sin_g = rope_ref[0], rope_ref[1], rope_ref[2], rope_ref[3]
    b64 = b64_ref[...]

    for j in range(BRANCH_W // LANES):
        sl = slice(j * LANES, (j + 1) * LANES)
        q = proj(C_DQ + j * LANES, C_DQ + (j + 1) * LANES)
        dqt_ref[sl, :] = _bf((_rope(q, cos_d, sin_d, DIFF_DQK // 4) * DIFF_SCALE).T)
        k = _rope(proj(C_DK + j * LANES, C_DK + (j + 1) * LANES), cos_d, sin_d, DIFF_DQK // 4)
        dk32_ref[:, sl] = k
        dk16_ref[:, sl] = _bf(k)
    v = proj(C_DV, C_SU)
    dv32_ref[...] = v
    dvt_ref[...] = _bf(v.T)

    u = proj(C_SU, C_SV)
    sv = proj(C_SV, C_GQ)
    vn = _bf(sv * lax.rsqrt(jnp.mean(sv * sv, axis=-1, keepdims=True) + EPS) * sgn_ref[...])
    group = lax.broadcasted_iota(jnp.int32, (1, BRANCH_W), 1) // HEAD_W
    for r in range(tm // SGU_CHUNK):
        rows = slice(r * SGU_CHUNK, (r + 1) * SGU_CHUNK)
        vc = vn[rows]
        z = sgb_ref[...]
        for g in range(SGU_GROUPS):
            z = z + jnp.where(group == g, _dot(sgw_ref[g], vc), 0.0)
        oc_ref[rows, :] = _bf(u[rows] * z)

    for j in range(BRANCH_W // LANES):
        sl = slice(j * LANES, (j + 1) * LANES)
        q = proj(C_GQ + j * LANES, C_GQ + (j + 1) * LANES)
        qn = q * lax.rsqrt(_group_sum(q * q, b64) * (1.0 / HEAD_W) + EPS) * gqn_ref[...]
        gqt_ref[sl, :] = _bf((_rope(qn, cos_g, sin_g, HEAD_W // 4) * GQA_SCALE).T)
    k = proj(C_GK, C_GV)
    kn = k * lax.rsqrt(_group_sum(k * k, b64) * (1.0 / HEAD_W) + EPS) * gkn_ref[...]
    kr = _rope(kn, cos_g, sin_g, HEAD_W // 4)
    gk32_ref[...] = kr
    gk16_ref[...] = _bf(kr)
    v = proj(C_GV, C_END)
    gv32_ref[...] = v
    gvt_ref[...] = _bf(v.T)


def _inproj_call(h, mods, n1, w_mix, rope_tab, b64, sgn, sgw, sgb, gqn, gkn, *, layer, tiles_ctx, tiles_per_lat):
    t = h.shape[0]
    n_tiles = t // TM

    def mod_row(i):
        return jnp.where(i < tiles_ctx, 0, 1 + (i - tiles_ctx) // tiles_per_lat)

    def rope_blk(i):
        return jnp.where(i < tiles_ctx, 0, 1 + (i - tiles_ctx) % tiles_per_lat)

    def const(shape):
        return pl.BlockSpec(shape, lambda i: (0,) * len(shape))

    def rows(width):
        return pl.BlockSpec((TM, width), lambda i: (i, 0))

    def cols(height):
        return pl.BlockSpec((height, TM), lambda i: (0, i))

    row_major = lambda w, dt: (rows(w), jax.ShapeDtypeStruct((t, w), dt))
    col_major = lambda hgt, dt: (cols(hgt), jax.ShapeDtypeStruct((hgt, t), dt))
    outs = [
        row_major(768, F32), row_major(256, F32), row_major(128, F32),
        col_major(256, BF16), row_major(256, F32), row_major(256, F32),
        row_major(256, BF16), col_major(256, BF16),
        row_major(256, BF16),
        col_major(256, BF16), row_major(128, F32), row_major(128, F32),
        row_major(128, BF16), col_major(128, BF16),
    ]
    return pl.pallas_call(
        _inproj_kernel,
        grid=(n_tiles,),
        in_specs=[
            rows(D_MODEL),
            _mod_spec(layer, mod_row),
            _layer_spec((1, D_MODEL), layer),
            _layer_spec((D_MODEL, C_END), layer),
            pl.BlockSpec((4, TM, LANES), lambda i: (0, rope_blk(i), 0)),
            const((LANES, LANES)),
            _layer_spec((1, BRANCH_W), layer),
            _layer_spec((SGU_GROUPS, SGU_CHUNK, SGU_CHUNK), layer),
            _layer_spec((SGU_CHUNK, BRANCH_W), layer),
            _layer_spec((1, LANES), layer),
            _layer_spec((1, LANES), layer),
        ],
        out_specs=[spec for spec, _ in outs],
        out_shape=[shape for _, shape in outs],
        compiler_params=pltpu.CompilerParams(dimension_semantics=("parallel",), vmem_limit_bytes=VMEM_LIMIT),
        name="inproj",
    )(h, mods, n1, w_mix, rope_tab, b64, sgn, sgw, sgb, gqn, gkn)


def _gdn_prep_kernel(x_ref, xp_ref, xn_ref, ab_ref, cw_ref, alog_ref, dtb_ref, b64_ref, tril_ref, triu_ref,
                     q_ref, k_ref, v_ref, gb_ref, *, ctx_blocks, ctx_bps, lat_bps):
    r = x_ref.shape[0]
    i = pl.program_id(0)
    is_ctx = i < ctx_blocks
    pos = jnp.where(is_ctx, i % ctx_bps, (i - ctx_blocks) % lat_bps)
    bps = jnp.where(is_ctx, ctx_bps, lat_bps)
    prev = jnp.where(pos > 0, xp_ref[...], 0.0)
    nxt = jnp.where(pos < bps - 1, xn_ref[...], 0.0)
    win = jnp.concatenate([prev, x_ref[...], nxt], axis=0)
    n = r + 2 * HALO
    y = None
    for j in range(CONV_K):
        s = (CONV_K // 2 - j) % n
        shifted = win if s == 0 else pltpu.roll(win, s, 0)
        term = cw_ref[j:j + 1, :] * shifted[HALO:HALO + r]
        y = term if y is None else y + term
    y = _silu(y)

    b64 = b64_ref[...]
    for part, ref, scale in ((0, q_ref, GDN_Q_SCALE), (1, k_ref, 1.0)):
        for j in range(BRANCH_W // LANES):
            x = y[:, part * BRANCH_W + j * LANES: part * BRANCH_W + (j + 1) * LANES]
            xn = x * (lax.rsqrt(_group_sum(x * x, b64) + EPS) * scale)
            for hh in range(LANES // HEAD_W):
                ref[j * (LANES // HEAD_W) + hh] = xn[:, hh * HEAD_W:(hh + 1) * HEAD_W]
    for h in range(N_HEADS):
        v_ref[h] = y[:, 2 * BRANCH_W + h * HEAD_W: 2 * BRANCH_W + (h + 1) * HEAD_W]

    ab = ab_ref[...]
    lane = lax.broadcasted_iota(jnp.int32, (1, LANES), 1)
    lm = lane % 4
    xa = ab + dtb_ref[...]
    softplus = jnp.maximum(xa, 0.0) + jnp.log1p(jnp.exp(-jnp.abs(xa)))
    la = jnp.where((lm < 2) & (lane < 4 * N_HEADS), -jnp.exp(alog_ref[...]) * softplus, 0.0)
    g_fwd = _split_dot_left(tril_ref[...], la, 3)
    g_bwd = _split_dot_left(triu_ref[...], la, 3)
    gb = jnp.where(lm == 0, g_fwd, jnp.where(lm == 1, g_bwd, _sigmoid(ab)))
    for h in range(N_HEADS):
        gb_ref[h] = gb if h == 0 else pltpu.roll(gb, LANES - 4 * h, 1)


def _gdn_prep_call(gqkv, gab, conv_w, alog_row, dtb_row, b64, tril, triu, *, layer, ctx_rows, ctx_len, lat_len):
    t = gqkv.shape[0]
    r = PREP_R
    n_blocks = t // r
    rh = r // HALO
    last_halo = t // HALO - 1
    kern = functools.partial(_gdn_prep_kernel, ctx_blocks=ctx_rows // r, ctx_bps=ctx_len // r, lat_bps=lat_len // r)

    def const(shape):
        return pl.BlockSpec(shape, lambda i: (0,) * len(shape))

    head_major = pl.BlockSpec((N_HEADS, r, HEAD_W), lambda i: (0, i, 0))
    return pl.pallas_call(
        kern,
        grid=(n_blocks,),
        in_specs=[
            pl.BlockSpec((r, 3 * BRANCH_W), lambda i: (i, 0)),
            pl.BlockSpec((HALO, 3 * BRANCH_W), lambda i: (jnp.maximum(i * rh - 1, 0), 0)),
            pl.BlockSpec((HALO, 3 * BRANCH_W), lambda i: (jnp.minimum((i + 1) * rh, last_halo), 0)),
            pl.BlockSpec((r, LANES), lambda i: (i, 0)),
            _layer_spec((SUBLANES, 3 * BRANCH_W), layer),
            _layer_spec((1, LANES), layer),
            _layer_spec((1, LANES), layer),
            const((LANES, LANES)),
            const((r, r)),
            const((r, r)),
        ],
        out_specs=[head_major, head_major, head_major,
                   pl.BlockSpec((N_HEADS, r, LANES), lambda i: (0, i, 0))],
        out_shape=[jax.ShapeDtypeStruct((N_HEADS, t, HEAD_W), F32)] * 3
        + [jax.ShapeDtypeStruct((N_HEADS, t, LANES), F32)],
        compiler_params=pltpu.CompilerParams(dimension_semantics=("parallel",), vmem_limit_bytes=VMEM_LIMIT),
        name="gdn_prep",
    )(gqkv, gqkv, gqkv, gab, conv_w, alog_row, dtb_row, b64, tril, triu)


def _gdn_scan_kernel(*refs, n_seq, seq_len, has_s0):
    if has_s0:
        q_ref, k_ref, v_ref, gb_ref, s0_ref, o_ref, s_ref, oi_scr, qg_scr, m_scr, b_scr = refs
    else:
        q_ref, k_ref, v_ref, gb_ref, o_ref, s_ref, oi_scr, qg_scr, m_scr, b_scr = refs
        s0_ref = None
    c = GDN_CHUNK
    gr = SCAN_G * c
    n_rows = n_seq * seq_len
    nc = seq_len // c
    row_i = lax.broadcasted_iota(jnp.int32, (gr, gr), 0)
    col_i = lax.broadcasted_iota(jnp.int32, (gr, gr), 1)
    eye = jnp.where(row_i == col_i, 1.0, 0.0)
    same = (row_i // c) == (col_i // c)
    incl = (same & (col_i <= row_i), same & (col_i >= row_i))
    strict = (same & (col_i < row_i), same & (col_i > row_i))
    blk16 = (row_i // 16) == (col_i // 16)
    blk32 = ((row_i // 32) == (col_i // 32)) & ~blk16
    blk64 = same & ((row_i // 32) != (col_i // 32))

    def prep_groups(gi, carry):
        probs = []
        for j in range(PREP_PAR):
            r0 = pl.multiple_of((gi * PREP_PAR + j) * gr, gr)
            rows = pl.ds(r0, gr)
            q = q_ref[rows, :]
            k = k_ref[rows, :]
            v = v_ref[rows, :]
            gbc = gb_ref[rows, :]
            gbt = gbc.T
            kb = _bf(k)
            kk = _dot_nt(kb, kb)
            qk = _dot_nt(_bf(q), kb)
            for d in range(2):
                gcol = gbc[:, d:d + 1]
                bcol = gbc[:, 2 + d:3 + d]
                dec = jnp.where(incl[d], jnp.exp(jnp.where(incl[d], gcol - gbt[d:d + 1, :], 0.0)), 0.0)
                probs.append(dict(d=d, r0=r0, rows=rows, q=q, k=k, v=v, gcol=gcol, bcol=bcol, dec=dec,
                                  qkm=_bf(qk * dec), nmat=jnp.where(strict[d], bcol * kk * dec, 0.0)))
        ps = [jnp.where(blk16, pr["nmat"], 0.0) for pr in probs]
        ts = [eye - p for p in ps]
        for _ in range(3):
            pbs = [_bf(p) for p in ps]
            ps = [_dot(pb, pb) for pb in pbs]
            ts = [t + _dot(_bf(t), _bf(p)) for t, p in zip(ts, ps)]
        for level in (blk32, blk64):
            tbs = [_bf(t) for t in ts]
            xs = [_dot(tb, _bf(jnp.where(level, pr["nmat"], 0.0))) for tb, pr in zip(tbs, probs)]
            ts = [t - _dot(_bf(x), tb) for t, x, tb in zip(ts, xs, tbs)]
        ainvs = [_bf(t) for t in ts]
        egs = [jnp.exp(pr["gcol"]) for pr in probs]
        rhs = [_bf(jnp.concatenate([pr["bcol"] * pr["v"], (pr["bcol"] * eg) * pr["k"]], axis=-1))
               for pr, eg in zip(probs, egs)]
        uwks = [_bf(_dot(ai, r)) for ai, r in zip(ainvs, rhs)]
        intra = [_dot(pr["qkm"], uwk) for pr, uwk in zip(probs, uwks)]
        for pr, eg, uwk, it in zip(probs, egs, uwks, intra):
            d, r0, rows, gcol = pr["d"], pr["r0"], pr["rows"], pr["gcol"]
            oi_scr[d, rows, :] = it[:, :HEAD_W]
            qg_scr[d, rows, :] = _bf(pr["q"] * eg - it[:, HEAD_W:])
            glast = jnp.concatenate(
                [jnp.broadcast_to(gcol[j * c + (c - 1 if d == 0 else 0):j * c + (c if d == 0 else 1), :], (c, 1))
                 for j in range(SCAN_G)], axis=0)
            kdb = _bf(pr["k"] * jnp.exp(glast - gcol))
            for j in range(SCAN_G):
                cs = slice(j * c, (j + 1) * c)
                crow = pl.ds(r0 + j * c, c)
                kd_uwk = _dot_tn(kdb[cs], uwk[cs])
                b_scr[d, crow, :] = kd_uwk[:, :HEAD_W]
                m_scr[d, crow, :] = _bf(kd_uwk[:, HEAD_W:])
        return carry

    lax.fori_loop(0, n_rows // (gr * PREP_PAR), prep_groups, 0)

    def scan_step(s, d, ci, state):
        r0 = s * seq_len + ci * c
        if not isinstance(r0, int):
            r0 = pl.multiple_of(r0, c)
        rows = pl.ds(r0, c)
        sb = _bf(state)
        oi_scr[d, rows, :] = oi_scr[d, rows, :] + _dot(qg_scr[d, rows, :], sb)
        last = r0 + (c - 1) if d == 0 else r0
        egl = jnp.exp(gb_ref[pl.ds(last, 1), d:d + 1])
        return egl * state - _dot(m_scr[d, rows, :], sb) + b_scr[d, rows, :]

    def init_state(s, d):
        return s0_ref[s, d] if has_s0 else jnp.zeros((c, c), F32)

    chains = [(s, d) for s in range(n_seq) for d in range(2)]
    states = [init_state(s, d) for s, d in chains]
    if nc <= 8:
        for t in range(nc):
            states = [scan_step(s, d, t if d == 0 else nc - 1 - t, st) for (s, d), st in zip(chains, states)]
    else:
        def body(t, sts):
            return tuple(scan_step(s, d, t if d == 0 else nc - 1 - t, st) for (s, d), st in zip(chains, sts))
        states = lax.fori_loop(0, nc, body, tuple(states))
    for (s, d), st in zip(chains, states):
        s_ref[s, d] = st
    o_ref[...] = oi_scr[0] + oi_scr[1]


def _gdn_scan_call(q, k, v, gb, s0, *, row0, n_seq_total, seq_len, seqs_per_step):
    n_rows = seqs_per_step * seq_len
    assert n_rows % (SCAN_G * GDN_CHUNK * PREP_PAR) == 0 and row0 % n_rows == 0
    blk0 = row0 // n_rows
    has_s0 = s0 is not None
    kern = functools.partial(_gdn_scan_kernel, n_seq=seqs_per_step, seq_len=seq_len, has_s0=has_s0)
    hm = pl.BlockSpec((None, n_rows, HEAD_W), lambda s, h: (h, blk0 + s, 0))
    st_spec = pl.BlockSpec((seqs_per_step, 2, None, HEAD_W, HEAD_W), lambda s, h: (s, 0, h, 0, 0))
    in_specs = [hm, hm, hm, pl.BlockSpec((None, n_rows, LANES), lambda s, h: (h, blk0 + s, 0))]
    args = [q, k, v, gb]
    if has_s0:
        in_specs.append(st_spec)
        args.append(s0)
    return pl.pallas_call(
        kern,
        grid=(n_seq_total // seqs_per_step, N_HEADS),
        in_specs=in_specs,
        out_specs=[pl.BlockSpec((None, n_rows, HEAD_W), lambda s, h: (h, s, 0)), st_spec],
        out_shape=[jax.ShapeDtypeStruct((N_HEADS, n_seq_total * seq_len, HEAD_W), F32),
                   jax.ShapeDtypeStruct((n_seq_total, 2, N_HEADS, HEAD_W, HEAD_W), F32)],
        scratch_shapes=[
            pltpu.VMEM((2, n_rows, HEAD_W), F32),
            pltpu.VMEM((2, n_rows, HEAD_W), BF16),
            pltpu.VMEM((2, n_rows, HEAD_W), BF16),
            pltpu.VMEM((2, n_rows, HEAD_W), F32),
        ],
        compiler_params=pltpu.CompilerParams(dimension_semantics=("parallel", "parallel"),
                                             vmem_limit_bytes=VMEM_LIMIT),
        name="gdn_scan",
    )(*args)


ONES_ROWS = 16


KEY_CHUNK = 1024
ATTN_WAVE_CTX = 4
ATTN_WAVE_LAT = 3


def _attend_all(tasks, width):
    waves = [tasks[i:i + width] for i in range(0, len(tasks), width)]

    def score_piece(wave, c):
        return [_dot(ks[c], qt) for qt, ks, _ in wave]

    n_pieces = len(tasks[0][1])
    outs = []
    sts = [score_piece(waves[0], c) for c in range(n_pieces)]
    for w, wave in enumerate(waves):
        mxs = []
        for j in range(len(wave)):
            mx = None
            for c in range(n_pieces):
                m = jnp.max(sts[c][j], axis=0, keepdims=True)
                mx = m if mx is None else jnp.maximum(mx, m)
            mxs.append(mx)
        nxt = []
        accs = [None] * len(wave)
        for c in range(n_pieces):
            if w + 1 < len(waves):
                nxt.append(score_piece(waves[w + 1], c))
            ps = [_bf(jnp.exp2(sts[c][j] - mxs[j])) for j in range(len(wave))]
            for j, (_, _, vts) in enumerate(wave):
                t = _dot(vts[c], ps[j])
                accs[j] = t if accs[j] is None else accs[j] + t
        outs += [acc[:HEAD_W] * (1.0 / acc[HEAD_W:HEAD_W + 1]) for acc in accs]
        sts = nxt
    return outs


def _key_pieces(k_ref, cache_ref):
    n = k_ref.shape[0]
    kc = min(KEY_CHUNK, n)
    own = [k_ref[c * kc:(c + 1) * kc, :] for c in range(n // kc)]
    return own if cache_ref is None else [cache_ref[...]] + own


def _value_pieces(vt_ref, cache_ref, rows):
    def with_ones(x):
        return jnp.concatenate([x, jnp.ones((ONES_ROWS, x.shape[1]), x.dtype)], axis=0)

    n = vt_ref.shape[1]
    kc = min(KEY_CHUNK, n)
    own = [with_ones(vt_ref[rows, c * kc:(c + 1) * kc]) for c in range(n // kc)]
    return own if cache_ref is None else [with_ones(cache_ref[rows, :])] + own


def _attn_kernel(*refs, has_cache, lam_init):
    if has_cache:
        (dqt_ref, dk_ref, dvt_ref, gqt_ref, gk_ref, gvt_ref, cdk_ref, cdvt_ref, cgk_ref, cgvt_ref,
         lp_ref, dn_ref, b64_ref, ob_ref, od_ref) = refs
    else:
        dqt_ref, dk_ref, dvt_ref, gqt_ref, gk_ref, gvt_ref, lp_ref, dn_ref, b64_ref, ob_ref, od_ref = refs
        cdk_ref = cdvt_ref = cgk_ref = cgvt_ref = None

    lp = lp_ref[...]
    lam = (jnp.exp(jnp.sum(lp[0:1] * lp[1:2], axis=-1, keepdims=True))
           - jnp.exp(jnp.sum(lp[2:3] * lp[3:4], axis=-1, keepdims=True)) + lam_init)

    chan = lax.broadcasted_iota(jnp.int32, (BRANCH_W, 1), 0)
    chan_h = lax.broadcasted_iota(jnp.int32, (LANES, 1), 0)
    dqt = dqt_ref[...]
    dks = _key_pieces(dk_ref, cdk_ref)
    gks = _key_pieces(gk_ref, cgk_ref)
    tasks = []
    for h in range(N_HEADS):
        vts = _value_pieces(dvt_ref, cdvt_ref, slice(h * HEAD_W, (h + 1) * HEAD_W))
        for m in range(2):
            lo = h * HEAD_W + m * DIFF_DQK
            tasks.append((dqt * jnp.where((chan >= lo) & (chan < lo + DIFF_DQK), 1.0, 0.0).astype(BF16), dks, vts))
    kv_vts = [_value_pieces(gvt_ref, cgvt_ref, slice(g * HEAD_W, (g + 1) * HEAD_W)) for g in range(GQA_KV_HEADS)]
    for half in range(2):
        qh = gqt_ref[half * LANES:(half + 1) * LANES, :]
        for g in range(GQA_KV_HEADS):
            tasks.append((qh * jnp.where(chan_h // HEAD_W == g, 1.0, 0.0).astype(BF16), gks, kv_vts[g]))
    outs = _attend_all(tasks, ATTN_WAVE_LAT if has_cache else ATTN_WAVE_CTX)

    o = jnp.concatenate([outs[2 * h] - lam * outs[2 * h + 1] for h in range(N_HEADS)], axis=0).T
    b64 = b64_ref[...]
    for j in range(BRANCH_W // LANES):
        sl = slice(j * LANES, (j + 1) * LANES)
        x = o[:, sl]
        ms = _group_sum(x * x, b64) * (1.0 / HEAD_W)
        ob_ref[:, sl] = _bf(x * lax.rsqrt(ms + EPS) * dn_ref[:, sl] * (1.0 - lam_init))
    od_ref[...] = _bf(jnp.concatenate(outs[2 * N_HEADS:], axis=0).T)


def _attn_ctx_call(dqt, dk, dvt, gqt, gk, gvt, lp, dn, b64, *, layer, n_seq, seq_len, lam_init):
    kern = functools.partial(_attn_kernel, has_cache=False, lam_init=lam_init)

    def rows(width):
        return pl.BlockSpec((seq_len, width), lambda s: (s, 0))

    def cols(height):
        return pl.BlockSpec((height, seq_len), lambda s: (0, s))

    def const(shape):
        return pl.BlockSpec(shape, lambda s: (0,) * len(shape))

    return pl.pallas_call(
        kern,
        grid=(n_seq,),
        in_specs=[cols(256), rows(256), cols(256), cols(256), rows(128), cols(128),
                  _layer_spec((4, DIFF_DQK), layer), _layer_spec((1, BRANCH_W), layer), const((LANES, LANES))],
        out_specs=[rows(256), rows(256)],
        out_shape=[jax.ShapeDtypeStruct((n_seq * seq_len, BRANCH_W), BF16)] * 2,
        compiler_params=pltpu.CompilerParams(dimension_semantics=("parallel",), vmem_limit_bytes=VMEM_LIMIT),
        name="attn_ctx",
    )(dqt, dk, dvt, gqt, gk, gvt, lp, dn, b64)


def _attn_lat_call(dqt, dk, dvt, gqt, gk, gvt, cdk, cdvt, cgk, cgvt, lp, dn, b64, *, layer, row0, n_seq, seq_len,
                   lam_init):
    kern = functools.partial(_attn_kernel, has_cache=True, lam_init=lam_init)
    tq = ATTN_TQ
    qpb = seq_len // tq
    q0 = row0 // tq
    s0 = row0 // seq_len
    past = cdk.shape[1]

    def qcols(height):
        return pl.BlockSpec((height, tq), lambda b, i: (0, q0 + b * qpb + i))

    def kvrows(width):
        return pl.BlockSpec((seq_len, width), lambda b, i: (s0 + b, 0))

    def kvcols(height):
        return pl.BlockSpec((height, seq_len), lambda b, i: (0, s0 + b))

    def cache(width):
        return pl.BlockSpec((None, past, width), lambda b, i: (b, 0, 0))

    def cache_t(height):
        return pl.BlockSpec((None, height, past), lambda b, i: (b, 0, 0))

    def const(shape):
        return pl.BlockSpec(shape, lambda b, i: (0,) * len(shape))

    out = pl.BlockSpec((tq, BRANCH_W), lambda b, i: (b * qpb + i, 0))
    return pl.pallas_call(
        kern,
        grid=(n_seq, qpb),
        in_specs=[qcols(256), kvrows(256), kvcols(256), qcols(256), kvrows(128), kvcols(128),
                  cache(256), cache_t(256), cache(128), cache_t(128),
                  _layer_spec((4, DIFF_DQK), layer), _layer_spec((1, BRANCH_W), layer), const((LANES, LANES))],
        out_specs=[out, out],
        out_shape=[jax.ShapeDtypeStruct((n_seq * seq_len, BRANCH_W), BF16)] * 2,
        compiler_params=pltpu.CompilerParams(dimension_semantics=("parallel", "arbitrary"),
                                             vmem_limit_bytes=VMEM_LIMIT),
        name="attn_lat",
    )(dqt, dk, dvt, gqt, gk, gvt, cdk, cdvt, cgk, cgvt, lp, dn, b64)


def _merge_kernel(h_ref, mod_ref, n1_ref, oac_ref, oal_ref, gate_ref, gn_ref, obc_ref, obl_ref, oc_ref,
                  odc_ref, odl_ref, wmg_ref, wb_ref, wo_ref, b64_ref, out_ref, *, tiles_ctx):
    is_ctx = pl.program_id(0) < tiles_ctx
    h = h_ref[...]
    xm = _bf(_norm_mod(h, n1_ref[...], mod_ref[:, D_MODEL:2 * D_MODEL], mod_ref[:, 0:D_MODEL]))
    g1 = mod_ref[:, 2 * D_MODEL:3 * D_MODEL]
    b64 = b64_ref[...]
    oa_parts = []
    for j in range(BRANCH_W // LANES):
        sl = slice(j * LANES, (j + 1) * LANES)
        x = jnp.concatenate([jnp.where(is_ctx, oac_ref[2 * j + hh], oal_ref[2 * j + hh])
                             for hh in range(LANES // HEAD_W)], axis=-1)
        ms = _group_sum(x * x, b64) * (1.0 / HEAD_W)
        oa_parts.append(_bf(x * lax.rsqrt(ms + EPS) * gn_ref[:, sl] * _silu(gate_ref[:, sl])))
    branches = (None, jnp.where(is_ctx, obc_ref[...], obl_ref[...]), oc_ref[...],
                jnp.where(is_ctx, odc_ref[...], odl_ref[...]))
    merged = None
    for n in range(4):
        mg = _dot(xm, wmg_ref[:, n * D_MODEL:(n + 1) * D_MODEL])
        if n == 0:
            up = _dot(oa_parts[0], wb_ref[0, 0:LANES, :]) + _dot(oa_parts[1], wb_ref[0, LANES:2 * LANES, :])
        else:
            up = _dot(branches[n], wb_ref[n])
        term = _sigmoid(mg) * up
        merged = term if merged is None else merged + term
    out_ref[...] = h + g1 * _dot(_bf(merged), wo_ref[...])


def _merge_call(h, mods, n1, oa_ctx, oa_lat, gate, gn, ob_ctx, ob_lat, oc, od_ctx, od_lat, wmg, wb, wo, b64, *,
                layer, tiles_ctx, tiles_per_lat):
    t = h.shape[0]
    last_ctx = tiles_ctx - 1

    def mod_row(i):
        return jnp.where(i < tiles_ctx, 0, 1 + (i - tiles_ctx) // tiles_per_lat)

    def rows(width):
        return pl.BlockSpec((TM, width), lambda i: (i, 0))

    def ctx_rows(width):
        return pl.BlockSpec((TM, width), lambda i: (jnp.minimum(i, last_ctx), 0))

    def lat_rows(width):
        return pl.BlockSpec((TM, width), lambda i: (jnp.maximum(i - tiles_ctx, 0), 0))

    return pl.pallas_call(
        functools.partial(_merge_kernel, tiles_ctx=tiles_ctx),
        grid=(t // TM,),
        in_specs=[
            rows(D_MODEL),
            _mod_spec(layer, mod_row),
            _layer_spec((1, D_MODEL), layer),
            pl.BlockSpec((N_HEADS, TM, HEAD_W), lambda i: (0, jnp.minimum(i, last_ctx), 0)),
            pl.BlockSpec((N_HEADS, TM, HEAD_W), lambda i: (0, jnp.maximum(i - tiles_ctx, 0), 0)),
            rows(BRANCH_W),
            _layer_spec((1, BRANCH_W), layer),
            ctx_rows(BRANCH_W), lat_rows(BRANCH_W), rows(BRANCH_W), ctx_rows(BRANCH_W), lat_rows(BRANCH_W),
            _layer_spec((D_MODEL, 4 * D_MODEL), layer),
            _layer_spec((4, BRANCH_W, D_MODEL), layer),
            _layer_spec((D_MODEL, D_MODEL), layer),
            pl.BlockSpec((LANES, LANES), lambda i: (0, 0)),
        ],
        out_specs=rows(D_MODEL),
        out_shape=jax.ShapeDtypeStruct((t, D_MODEL), F32),
        compiler_params=pltpu.CompilerParams(dimension_semantics=("parallel",), vmem_limit_bytes=VMEM_LIMIT),
        name="merge",
    )(h, mods, n1, oa_ctx, oa_lat, gate, gn, ob_ctx, ob_lat, oc, od_ctx, od_lat, wmg, wb, wo, b64)


ROUTER_ROWS = 32
ROUTER_E0 = 8


def _moe_kernel(h_ref, mod_ref, n2_ref, wr_ref, br_ref, w1_ref, w3_ref, w2_ref, out_ref, xm_scr, gates_scr, acc_scr):
    e = pl.program_id(1)
    tm = h_ref.shape[0]

    @pl.when(e == 0)
    def _():
        xm = _norm_mod(h_ref[...], n2_ref[...], mod_ref[:, 4 * D_MODEL:5 * D_MODEL],
                       mod_ref[:, 3 * D_MODEL:4 * D_MODEL])
        x_hi = _bf(xm)
        xm_scr[...] = x_hi
        x_lo = _bf(xm - x_hi.astype(F32))
        hi = _dot_nt(wr_ref[...], x_hi)
        lt = (hi[0:ROUTER_ROWS] + hi[ROUTER_ROWS:2 * ROUTER_ROWS]
              + _dot_nt(wr_ref[0:ROUTER_ROWS, :], x_lo) + br_ref[...])
        gl = lt[0:N_GROUPS]
        gidx = lax.broadcasted_iota(jnp.int32, (N_GROUPS, 1), 0)
        gmax = jnp.max(gl, axis=0, keepdims=True)
        gsel = jnp.min(jnp.where(gl == gmax, gidx, N_GROUPS), axis=0, keepdims=True)
        gprob = 1.0 / jnp.sum(jnp.exp(gl - gmax), axis=0, keepdims=True)
        el = lt[ROUTER_E0:ROUTER_E0 + N_EXPERTS]
        eidx = lax.broadcasted_iota(jnp.int32, (N_EXPERTS, 1), 0)
        neg = -jnp.inf
        cand = jnp.where(eidx // EXPERTS_PER_GROUP == gsel, el, neg)
        v1 = jnp.max(cand, axis=0, keepdims=True)
        i1 = jnp.min(jnp.where(cand == v1, eidx, N_EXPERTS), axis=0, keepdims=True)
        cand2 = jnp.where(eidx == i1, neg, cand)
        v2 = jnp.max(cand2, axis=0, keepdims=True)
        i2 = jnp.min(jnp.where(cand2 == v2, eidx, N_EXPERTS), axis=0, keepdims=True)
        ex = jnp.exp(v2 - v1)
        wa = 1.0 / (1.0 + ex)
        gates_t = (jnp.where(eidx == i1, wa, 0.0) + jnp.where(eidx == i2, ex * wa, 0.0)) * gprob
        full = jnp.concatenate([gates_t, jnp.zeros((LANES - N_EXPERTS, tm), F32)], axis=0)
        gates_scr[...] = full.T
        acc_scr[...] = jnp.zeros_like(acc_scr)

    xm = xm_scr[...]
    lane = lax.broadcasted_iota(jnp.int32, (1, LANES), 1)
    gates = gates_scr[...]
    hds = []
    for j in range(MOE_EPS):
        h1 = _dot(xm, w1_ref[j])
        h3 = _dot(xm, w3_ref[j])
        gate = jnp.sum(jnp.where(lane == e * MOE_EPS + j, gates, 0.0), axis=-1, keepdims=True)
        hds.append(_bf(_silu(h1) * h3 * gate))
    acc_scr[...] += _dot(jnp.concatenate(hds, axis=-1), w2_ref[...].reshape(MOE_EPS * D_EXPERT, D_MODEL))

    @pl.when(e == N_EXPERTS // MOE_EPS - 1)
    def _():
        out_ref[...] = h_ref[...] + mod_ref[:, 5 * D_MODEL:6 * D_MODEL] * acc_scr[...]


def _moe_call(h, mods, n2, wr, br, w1, w3, w2, *, layer, ctx_rows, lat_len):
    t = h.shape[0]
    tm = TM_MOE
    tiles_ctx = ctx_rows // tm
    tiles_per_lat = lat_len // tm

    def mod_row(i):
        return jnp.where(i < tiles_ctx, 0, 1 + (i - tiles_ctx) // tiles_per_lat)

    def expert(shape):
        return pl.BlockSpec((None, MOE_EPS) + shape, lambda i, e: (layer, e, 0, 0))

    return pl.pallas_call(
        _moe_kernel,
        grid=(t // tm, N_EXPERTS // MOE_EPS),
        in_specs=[
            pl.BlockSpec((tm, D_MODEL), lambda i, e: (i, 0)),
            _mod_spec(layer, mod_row),
            _layer_spec((1, D_MODEL), layer),
            _layer_spec((2 * ROUTER_ROWS, D_MODEL), layer),
            _layer_spec((ROUTER_ROWS, 1), layer),
            expert((D_MODEL, D_EXPERT)),
            expert((D_MODEL, D_EXPERT)),
            expert((D_EXPERT, D_MODEL)),
        ],
        out_specs=pl.BlockSpec((tm, D_MODEL), lambda i, e: (i, 0)),
        out_shape=jax.ShapeDtypeStruct((t, D_MODEL), F32),
        scratch_shapes=[pltpu.VMEM((tm, D_MODEL), BF16), pltpu.VMEM((tm, LANES), F32),
                        pltpu.VMEM((tm, D_MODEL), F32)],
        compiler_params=pltpu.CompilerParams(dimension_semantics=("parallel", "arbitrary"),
                                             vmem_limit_bytes=VMEM_LIMIT),
        name="moe",
    )(h, mods, n2, wr, br, w1, w3, w2)


def _final_norm_kernel(h_ref, g_ref, o_ref):
    h = h_ref[...]
    o_ref[...] = h * lax.rsqrt(jnp.mean(h * h, axis=-1, keepdims=True) + EPS) * g_ref[...]


def _final_norm_call(h, g, *, row0, n_rows):
    blk0 = row0 // TM
    return pl.pallas_call(
        _final_norm_kernel,
        grid=(n_rows // TM,),
        in_specs=[pl.BlockSpec((TM, D_MODEL), lambda i: (blk0 + i, 0)), pl.BlockSpec((1, D_MODEL), lambda i: (0, 0))],
        out_specs=pl.BlockSpec((TM, D_MODEL), lambda i: (i, 0)),
        out_shape=jax.ShapeDtypeStruct((n_rows, D_MODEL), F32),
        compiler_params=pltpu.CompilerParams(dimension_semantics=("parallel",)),
        name="final_norm",
    )(h, g)


def _mixer_weights(w_in):
    depth, d, _ = w_in.shape
    ab_cols = []
    for h in range(N_HEADS):
        ab_cols += [_O_A + h, _O_A + N_HEADS + h, _O_B + h, _O_B + N_HEADS + h]
    ab = jnp.take(w_in[:, :, _O_A:_O_DQ], jnp.asarray([c - _O_A for c in ab_cols], dtype=jnp.int32), axis=2)
    parts = [w_in[:, :, _O_QKV:_O_A], ab, jnp.zeros((depth, d, LANES - len(ab_cols)), w_in.dtype),
             w_in[:, :, _O_DQ:_O_GQ]]
    for h in (0, 2, 1, 3):
        parts.append(w_in[:, :, _O_GQ + h * HEAD_W:_O_GQ + (h + 1) * HEAD_W])
    parts.append(w_in[:, :, _O_GK:_O_MG])
    w_mix = _bf(jnp.concatenate(parts, axis=2))
    assert w_mix.shape[2] == C_END
    return w_mix


def _rope_tables(lat_len):
    pos = jnp.arange(lat_len, dtype=jnp.int32)
    row = (pos // GRID_W).astype(F32)[:, None]
    col = (pos % GRID_W).astype(F32)[:, None]
    lane = jnp.arange(LANES, dtype=jnp.int32)[None, :]
    tabs = []
    for head_w in (DIFF_DQK, HEAD_W):
        half = head_w // 2
        nf = half // 2
        p = lane % head_w
        f = (p % half) % nf
        inv = ROPE_THETA ** (-f.astype(F32) / nf)
        ang = jnp.where(p < half, row, col) * inv
        sign = jnp.where((p % half) < nf, -1.0, 1.0)
        tabs += [jnp.cos(ang), jnp.sin(ang) * sign]
    tab = jnp.stack(tabs)
    ident = jnp.stack([jnp.ones((TM, LANES), F32), jnp.zeros((TM, LANES), F32)] * 2)
    return jnp.concatenate([ident, tab], axis=1)


def _chunk_tri(n, chunk, upper):
    i = jnp.arange(n)[:, None]
    j = jnp.arange(n)[None, :]
    same = (i // chunk) == (j // chunk)
    tri = (j >= i) if upper else (j <= i)
    return jnp.where(same & tri, 1.0, 0.0).astype(BF16)


def kernel(x_prompt, x_sample, state_gdn, cache_diff_k, cache_diff_v, cache_gqa_k, cache_gqa_v, c, c_ctx,
           ada_w, ada_b, norm1, norm2, w_in, gdn_conv, gdn_a_log, gdn_dt_bias, gdn_norm, diff_lambda, diff_norm,
           sgu_norm, sgu_ws, sgu_b, gqa_qnorm, gqa_knorm, w_branch, w_out, moe_wg, moe_bg, moe_we, moe_be,
           moe_w1, moe_w3, moe_w2, final_norm):
    nb, seq, d = x_prompt.shape
    nl, lat_len, _ = x_sample.shape
    depth = ada_w.shape[0]
    past = cache_diff_k.shape[2]
    ctx_rows = nb * seq
    lat_rows = nl * lat_len
    assert d == D_MODEL and seq % PREP_R == 0 and lat_len % TM == 0 and ctx_rows % lat_len == 0
    assert ctx_rows % TM == 0 and 1 + nl <= SUBLANES and lat_len % GRID_W == 0
    assert ctx_rows % TM_MOE == 0 and lat_len % TM_MOE == 0
    tiles_ctx = ctx_rows // TM
    tiles_per_lat = lat_len // TM
    tile_kw = dict(tiles_ctx=tiles_ctx, tiles_per_lat=tiles_per_lat)

    lane = jnp.arange(LANES)
    b64 = jnp.where((lane[:, None] // HEAD_W) == (lane[None, :] // HEAD_W), 1.0, 0.0).astype(BF16)
    tril = _chunk_tri(PREP_R, GDN_CHUNK, upper=False)
    triu = _chunk_tri(PREP_R, GDN_CHUNK, upper=True)
    rope_tab = _rope_tables(lat_len)

    w_mix = _mixer_weights(w_in)
    w_mg = _bf(w_in[:, :, _O_MG:])
    head_perm = jnp.asarray([r for h in (0, 2, 1, 3) for r in range(h * HEAD_W, (h + 1) * HEAD_W)], dtype=jnp.int32)
    wb = _bf(jnp.concatenate([w_branch[:, :3], jnp.take(w_branch[:, 3], head_perm, axis=1)[:, None]], axis=1))
    wo = _bf(w_out)
    w1, w3, w2 = _bf(moe_w1), _bf(moe_w3), _bf(moe_w2)
    wr = jnp.zeros((depth, ROUTER_ROWS, d), F32)
    wr = wr.at[:, 0:N_GROUPS].set(jnp.swapaxes(moe_wg, 1, 2))
    wr = wr.at[:, ROUTER_E0:ROUTER_E0 + N_EXPERTS].set(jnp.swapaxes(moe_we, 1, 2))
    wr_hi = _bf(wr)
    wr = jnp.concatenate([wr_hi, _bf(wr - wr_hi.astype(F32))], axis=1)
    br = jnp.zeros((depth, ROUTER_ROWS, 1), F32)
    br = br.at[:, 0:N_GROUPS, 0].set(moe_bg).at[:, ROUTER_E0:ROUTER_E0 + N_EXPERTS, 0].set(moe_be)
    conv_w = jnp.zeros((depth, SUBLANES, 3 * BRANCH_W), F32).at[:, :CONV_K].set(gdn_conv)
    ab_lane = jnp.zeros((depth, 1, LANES), F32)
    alog_row, dtb_row = ab_lane, ab_lane
    for h in range(N_HEADS):
        for dd in range(2):
            alog_row = alog_row.at[:, 0, 4 * h + dd].set(gdn_a_log[:, dd, h])
            dtb_row = dtb_row.at[:, 0, 4 * h + dd].set(gdn_dt_bias[:, dd, h])
    sgw = _bf(sgu_ws)
    sgb = jnp.repeat(jnp.swapaxes(sgu_b, 1, 2), HEAD_W, axis=2)
    gdn_gn = jnp.tile(gdn_norm, (1, N_HEADS))[:, None, :]
    diff_gn = jnp.tile(diff_norm, (1, N_HEADS))[:, None, :]
    gqn = jnp.tile(gqa_qnorm, (1, LANES // HEAD_W))[:, None, :]
    gkn = jnp.tile(gqa_knorm, (1, LANES // HEAD_W))[:, None, :]

    cond = jnp.zeros((SUBLANES, d), F32).at[0].set(c_ctx).at[1:1 + nl].set(c)
    mods = _ada_call(cond, ada_w, ada_b)[:, :, None, :]
    n1 = norm1[:, None, :]
    n2 = norm2[:, None, :]
    sgn = sgu_norm[:, None, :]

    h = jnp.concatenate([x_prompt.reshape(ctx_rows, d), x_sample.reshape(lat_rows, d)], axis=0)
    new_s, new_dk, new_dv, new_gk, new_gv = [], [], [], [], []
    for l in range(depth):
        lam_init = 0.8 - 0.6 * math.exp(-0.3 * l)
        (gqkv, ggate, gab, dqt, dk32, dv32, dk16, dvt, oc, gqt, gk32, gv32, gk16, gvt) = _inproj_call(
            h, mods, n1, w_mix, rope_tab, b64, sgn, sgw, sgb, gqn, gkn, layer=l, **tile_kw)

        qh, kh, vh, gb = _gdn_prep_call(gqkv, gab, conv_w, alog_row, dtb_row, b64, tril, triu, layer=l,
                                        ctx_rows=ctx_rows, ctx_len=seq, lat_len=lat_len)
        oa_ctx, s_ctx = _gdn_scan_call(qh, kh, vh, gb, None, row0=0, n_seq_total=nb, seq_len=seq,
                                       seqs_per_step=8 if nb % 8 == 0 else 1)
        oa_lat, _ = _gdn_scan_call(qh, kh, vh, gb, state_gdn[:, l], row0=ctx_rows, n_seq_total=nl,
                                   seq_len=lat_len, seqs_per_step=1)

        ob_ctx, od_ctx = _attn_ctx_call(dqt, dk16, dvt, gqt, gk16, gvt, diff_lambda, diff_gn, b64, layer=l,
                                        n_seq=nb, seq_len=seq, lam_init=lam_init)
        ob_lat, od_lat = _attn_lat_call(
            dqt, dk16, dvt, gqt, gk16, gvt,
            _bf(cache_diff_k[:, l].reshape(nl, past, BRANCH_W)),
            _bf(jnp.swapaxes(cache_diff_v[:, l].reshape(nl, past, BRANCH_W), 1, 2)),
            _bf(cache_gqa_k[:, l].reshape(nl, past, LANES)),
            _bf(jnp.swapaxes(cache_gqa_v[:, l].reshape(nl, past, LANES), 1, 2)),
            diff_lambda, diff_gn, b64, layer=l, row0=ctx_rows, n_seq=nl, seq_len=lat_len, lam_init=lam_init)

        h = _merge_call(h, mods, n1, oa_ctx, oa_lat, ggate, gdn_gn, ob_ctx, ob_lat, oc, od_ctx, od_lat,
                        w_mg, wb, wo, b64, layer=l, **tile_kw)
        h = _moe_call(h, mods, n2, wr, br, w1, w3, w2, layer=l, ctx_rows=ctx_rows, lat_len=lat_len)

        new_s.append(s_ctx)
        new_dk.append(dk32[:ctx_rows].reshape(nb, seq, N_HEADS, HEAD_W))
        new_dv.append(dv32[:ctx_rows].reshape(nb, seq, N_HEADS, HEAD_W))
        new_gk.append(gk32[:ctx_rows].reshape(nb, seq, GQA_KV_HEADS, HEAD_W))
        new_gv.append(gv32[:ctx_rows].reshape(nb, seq, GQA_KV_HEADS, HEAD_W))

    fg = final_norm[None, :]
    y_ctx = _final_norm_call(h, fg, row0=0, n_rows=ctx_rows)
    y_lat = _final_norm_call(h, fg, row0=ctx_rows, n_rows=lat_rows)
    return (y_ctx.reshape(nb, seq, d), y_lat.reshape(nl, lat_len, d),
            jnp.stack(new_s, axis=1), jnp.stack(new_dk, axis=1), jnp.stack(new_dv, axis=1),
            jnp.stack(new_gk, axis=1), jnp.stack(new_gv, axis=1))
```
